```python
import math
import jax, jax.numpy as jnp
from jax import lax
import numpy as np

D_MODEL = 4096
BATCH = 1
SEQ = 8192
DEPTH = 2

N_EVEN = (DEPTH + 1) // 2
N_ODD = DEPTH // 2

S5_WIDTH = D_MODEL // 2
S5_GROUP = 16
S5_GROUPS = S5_WIDTH // S5_GROUP
S5_STATE = 64
S5_DT_MIN = 0.001
S5_DT_MAX = 0.1

RET_HEADS = 8
RET_HEAD_DIM = (D_MODEL // 2) // RET_HEADS
RET_WIDTH = RET_HEADS * RET_HEAD_DIM
RET_CHUNK = 128
ROPE_BASE = 10000.0

EVEN_IN_WIDTH = S5_WIDTH + 4 * RET_WIDTH

DIFF_HEADS = 16
DIFF_HEAD_DIM = D_MODEL // (2 * DIFF_HEADS)
DIFF_V_DIM = 2 * DIFF_HEAD_DIM
ODD_IN_WIDTH = 3 * D_MODEL
Q_BLOCK = 128

REL_BUCKETS = 32
REL_MAX_DIST = 128

PEER_HEADS = 8
PEER_N_KEYS = 128
PEER_N_EXPERTS = PEER_N_KEYS * PEER_N_KEYS
PEER_KEY_DIM = 128
PEER_TOPK = 16
PEER_TOKEN_BLOCK = 128

EPS = 1e-6
F32 = jnp.float32

kernel_name = "hybrid_s5_retnet_diffattn_peer"


def rmsnorm(x, g):
    x32 = x.astype(F32)
    y = x32 * lax.rsqrt(jnp.mean(x32 * x32, axis=-1, keepdims=True) + EPS)
    return (y * g.astype(F32)).astype(x.dtype)


def rotary(x, pos):
    half = x.shape[-1] // 2
    freqs = ROPE_BASE ** (-jnp.arange(half, dtype=F32) / half)
    ang = pos[:, None] * freqs[None, :]
    cos = jnp.cos(ang)[None, :, None, :]
    sin = jnp.sin(ang)[None, :, None, :]
    x32 = x.astype(F32)
    x1, x2 = x32[..., :half], x32[..., half:]
    return jnp.concatenate([x1 * cos - x2 * sin, x1 * sin + x2 * cos], axis=-1)


def s5_mixer(u, log_step, a_re, a_im, b_re, b_im, c_re, c_im, d_skip, glu_w, glu_b):
    bsz, t, _ = u.shape
    dt = u.dtype
    u32 = u.astype(F32)
    ug = u32.reshape(bsz, t, S5_GROUPS, S5_GROUP)
    step = jnp.exp(log_step.astype(F32))[:, None]
    lr, li = a_re.astype(F32), a_im.astype(F32)
    mag = jnp.exp(lr * step)
    abar_re = mag * jnp.cos(li * step)
    abar_im = mag * jnp.sin(li * step)
    den = lr * lr + li * li
    num_re = abar_re - 1.0
    f_re = (num_re * lr + abar_im * li) / den
    f_im = (abar_im * lr - num_re * li) / den
    br, bi = b_re.astype(F32), b_im.astype(F32)
    bbar_re = f_re[..., None] * br - f_im[..., None] * bi
    bbar_im = f_re[..., None] * bi + f_im[..., None] * br
    xr = jnp.einsum('btgc,gpc->btgp', ug, bbar_re)
    xi = jnp.einsum('btgc,gpc->btgp', ug, bbar_im)
    ar = jnp.broadcast_to(abar_re, xr.shape)
    ai = jnp.broadcast_to(abar_im, xi.shape)

    def combine(e1, e2):
        a1r, a1i, b1r, b1i = e1
        a2r, a2i, b2r, b2i = e2
        return (a2r * a1r - a2i * a1i,
                a2r * a1i + a2i * a1r,
                a2r * b1r - a2i * b1i + b2r,
                a2r * b1i + a2i * b1r + b2i)

    _, _, sr, si = lax.associative_scan(combine, (ar, ai, xr, xi), axis=1)
    y = (jnp.einsum('btgp,gcp->btgc', sr, c_re.astype(F32))
         - jnp.einsum('btgp,gcp->btgc', si, c_im.astype(F32)))
    y = y.reshape(bsz, t, S5_WIDTH) + d_skip.astype(F32) * u32
    y = jax.nn.gelu(y).astype(dt)
    return y * jax.nn.sigmoid(y @ glu_w + glu_b)


def retention_mixer(q, k, v, gate, gn_g, gn_b, pos):
    bsz, t, _ = q.shape
    dt = q.dtype
    shp = (bsz, t, RET_HEADS, RET_HEAD_DIM)
    q = rotary(q.reshape(shp), pos)
    k = rotary(k.reshape(shp), pos) * (RET_HEAD_DIM ** -0.5)
    v = v.reshape(shp).astype(F32)
    gamma = 1.0 - 2.0 ** (-5.0 - jnp.arange(RET_HEADS, dtype=F32))
    log_g = jnp.log(gamma)
    idx = jnp.arange(RET_CHUNK, dtype=F32)
    rel = idx[:, None] - idx[None, :]
    intra = jnp.where(rel >= 0, jnp.exp(log_g[:, None, None] * jnp.maximum(rel, 0.0)), 0.0)
    q_decay = jnp.exp(log_g[:, None] * (idx + 1.0))[..., None]
    k_decay = jnp.exp(log_g[:, None] * (RET_CHUNK - 1.0 - idx))[..., None]
    chunk_decay = jnp.exp(log_g * RET_CHUNK)[:, None, None]
    n_chunks = t // RET_CHUNK

    def to_chunks(z):
        return z.reshape(bsz, n_chunks, RET_CHUNK, RET_HEADS, RET_HEAD_DIM).transpose(1, 0, 3, 2, 4)

    qc, kc, vc = to_chunks(q), to_chunks(k), to_chunks(v)

    def step(state, inp):
        qi, ki, vi = inp
        scores = jnp.einsum('bhid,bhjd->bhij', qi, ki) * intra
        inner = jnp.einsum('bhij,bhjd->bhid', scores, vi)
        cross = jnp.einsum('bhid,bhde->bhie', qi * q_decay, state)
        new_state = chunk_decay * state + jnp.einsum('bhjd,bhje->bhde', ki * k_decay, vi)
        return new_state, inner + cross

    state0 = jnp.zeros((bsz, RET_HEADS, RET_HEAD_DIM, RET_HEAD_DIM), F32)
    _, out = lax.scan(step, state0, (qc, kc, vc))
    out = out.transpose(1, 0, 3, 2, 4).reshape(shp)
    mu = jnp.mean(out, axis=-1, keepdims=True)
    var = jnp.mean((out - mu) ** 2, axis=-1, keepdims=True)
    out = ((out - mu) * lax.rsqrt(var + EPS)).reshape(bsz, t, RET_WIDTH)
    out = out * gn_g.astype(F32) + gn_b.astype(F32)
    return (jax.nn.silu(gate.astype(F32)) * out).astype(dt)


def even_mixer(h, w_in, log_step, a_re, a_im, b_re, b_im, c_re, c_im, d_skip,
               glu_w, glu_b, gn_g, gn_b, w_out):
    t = h.shape[1]
    pos = jnp.arange(t, dtype=F32)
    z = h @ w_in
    u, q, k, v, g = jnp.split(z, [S5_WIDTH, S5_WIDTH + RET_WIDTH, S5_WIDTH + 2 * RET_WIDTH,
                                  S5_WIDTH + 3 * RET_WIDTH], axis=-1)
    ya = s5_mixer(u, log_step, a_re, a_im, b_re, b_im, c_re, c_im, d_skip, glu_w, glu_b)
    yb = retention_mixer(q, k, v, g, gn_g, gn_b, pos)
    return jnp.concatenate([ya, yb], axis=-1) @ w_out


def t5_bucket(rel):
    n = jnp.maximum(rel, 0)
    max_exact = REL_BUCKETS // 2
    nf = jnp.maximum(n, 1).astype(F32)
    large = max_exact + (jnp.log(nf / max_exact) / math.log(REL_MAX_DIST / max_exact)
                         * (REL_BUCKETS - max_exact)).astype(jnp.int32)
    large = jnp.minimum(large, REL_BUCKETS - 1)
    return jnp.where(n < max_exact, n, large)


def odd_mixer(h, w_in, q_norm_g, k_norm_g, lq1, lk1, lq2, lk2, sub_norm_g, w_out,
              rel_bias, lambda_init):
    bsz, t, _ = h.shape
    dt = h.dtype
    q, k, v = jnp.split(h @ w_in, 3, axis=-1)
    q = q.reshape(bsz, t, DIFF_HEADS, 2, DIFF_HEAD_DIM)
    k = k.reshape(bsz, t, DIFF_HEADS, 2, DIFF_HEAD_DIM)
    v = v.reshape(bsz, t, DIFF_HEADS, DIFF_V_DIM)
    q = rmsnorm(q, q_norm_g) * (DIFF_HEAD_DIM ** -0.5)
    k = rmsnorm(k, k_norm_g)
    lam = (jnp.exp(jnp.sum(lq1.astype(F32) * lk1.astype(F32)))
           - jnp.exp(jnp.sum(lq2.astype(F32) * lk2.astype(F32))) + lambda_init)
    kh = k.transpose(0, 2, 3, 1, 4)
    vh = v.transpose(0, 2, 1, 3)
    n_blocks = t // Q_BLOCK
    qb = q.transpose(0, 2, 3, 1, 4).reshape(bsz, DIFF_HEADS, 2, n_blocks, Q_BLOCK, DIFF_HEAD_DIM)
    qb = qb.transpose(3, 0, 1, 2, 4, 5)
    k_pos = jnp.arange(t, dtype=jnp.int32)

    def block(args):
        qi, bi = args
        q_pos = bi * Q_BLOCK + jnp.arange(Q_BLOCK, dtype=jnp.int32)
        rel = q_pos[:, None] - k_pos[None, :]
        bias = rel_bias[t5_bucket(rel)].transpose(2, 0, 1).astype(F32)
        logits = jnp.einsum('bhmqd,bhmkd->bhmqk', qi, kh).astype(F32) + bias[None, :, None]
        logits = jnp.where(rel >= 0, logits, -jnp.inf)
        p = jax.nn.softmax(logits, axis=-1)
        attn = p[:, :, 0] - lam * p[:, :, 1]
        return jnp.einsum('bhqk,bhkd->bhqd', attn.astype(dt), vh)

    out = lax.map(block, (qb, jnp.arange(n_blocks, dtype=jnp.int32)))
    out = out.transpose(1, 0, 3, 2, 4).reshape(bsz, t, DIFF_HEADS, DIFF_V_DIM)
    out = rmsnorm(out, sub_norm_g) * (1.0 - lambda_init)
    return out.reshape(bsz, t, D_MODEL) @ w_out


def peer(h, w_q, sub_keys, u_tab, v_tab):
    bsz, t, d = h.shape
    dt = h.dtype
    n_tok = bsz * t
    x = h.reshape(n_tok, d)
    q = (x @ w_q).reshape(n_tok, PEER_HEADS, 2, PEER_KEY_DIM)
    s = jnp.einsum('nhpk,hpek->nhpe', q, sub_keys).astype(F32)
    s1, i1 = lax.top_k(s[:, :, 0], PEER_TOPK)
    s2, i2 = lax.top_k(s[:, :, 1], PEER_TOPK)
    cand = (s1[..., :, None] + s2[..., None, :]).reshape(n_tok, PEER_HEADS, PEER_TOPK * PEER_TOPK)
    cand_idx = (i1[..., :, None] * PEER_N_KEYS + i2[..., None, :]).reshape(n_tok, PEER_HEADS, PEER_TOPK * PEER_TOPK)
    best, sel = lax.top_k(cand, PEER_TOPK)
    expert = jnp.take_along_axis(cand_idx, sel, axis=-1)
    gate = jax.nn.softmax(best, axis=-1)
    nb = n_tok // PEER_TOKEN_BLOCK

    def block(args):
        xb, eb, gb = args
        act = jax.nn.gelu(jnp.einsum('ted,td->te', u_tab[eb], xb).astype(F32))
        w = (gb * act).astype(dt)
        return jnp.einsum('te,ted->td', w, v_tab[eb])

    out = lax.map(block, (x.reshape(nb, PEER_TOKEN_BLOCK, d),
                          expert.reshape(nb, PEER_TOKEN_BLOCK, PEER_HEADS * PEER_TOPK),
                          gate.reshape(nb, PEER_TOKEN_BLOCK, PEER_HEADS * PEER_TOPK)))
    return out.reshape(bsz, t, d)


def setup_inputs(seed: int = 0) -> dict:
    key = jax.random.key(seed)
    ks = iter(jax.random.split(key, 40))
    nrm = lambda shape, scale: jax.random.normal(next(ks), shape, F32) * scale
    D = D_MODEL
    inp = {}
    inp["x"] = nrm((BATCH, SEQ, D), 1.0)
    inp["ev_norm_g"] = 1.0 + nrm((N_EVEN, D), 0.02)
    inp["ev_w_in"] = nrm((N_EVEN, D, EVEN_IN_WIDTH), D ** -0.5)
    inp["s5_log_step"] = jax.random.uniform(next(ks), (N_EVEN, S5_GROUPS), F32,
                                            math.log(S5_DT_MIN), math.log(S5_DT_MAX))
    inp["s5_a_re"] = -0.5 + nrm((N_EVEN, S5_GROUPS, S5_STATE), 0.01)
    inp["s5_a_im"] = (math.pi * jnp.arange(S5_STATE, dtype=F32))[None, None, :] + nrm((N_EVEN, S5_GROUPS, S5_STATE), 0.01)
    inp["s5_b_re"] = nrm((N_EVEN, S5_GROUPS, S5_STATE, S5_GROUP), (2 * S5_GROUP) ** -0.5)
    inp["s5_b_im"] = nrm((N_EVEN, S5_GROUPS, S5_STATE, S5_GROUP), (2 * S5_GROUP) ** -0.5)
    inp["s5_c_re"] = nrm((N_EVEN, S5_GROUPS, S5_GROUP, S5_STATE), (2 * S5_STATE) ** -0.5)
    inp["s5_c_im"] = nrm((N_EVEN, S5_GROUPS, S5_GROUP, S5_STATE), (2 * S5_STATE) ** -0.5)
    inp["s5_d"] = nrm((N_EVEN, S5_WIDTH), 1.0)
    inp["s5_glu_w"] = nrm((N_EVEN, S5_WIDTH, S5_WIDTH), S5_WIDTH ** -0.5)
    inp["s5_glu_b"] = nrm((N_EVEN, S5_WIDTH), 0.01)
    inp["ret_gn_g"] = 1.0 + nrm((N_EVEN, RET_WIDTH), 0.02)
    inp["ret_gn_b"] = nrm((N_EVEN, RET_WIDTH), 0.01)
    inp["ev_w_out"] = nrm((N_EVEN, D, D), D ** -0.5)
    inp["od_norm_g"] = 1.0 + nrm((N_ODD, D), 0.02)
    inp["od_w_in"] = nrm((N_ODD, D, ODD_IN_WIDTH), D ** -0.5)
    inp["diff_q_norm_g"] = 1.0 + nrm((N_ODD, DIFF_HEAD_DIM), 0.02)
    inp["diff_k_norm_g"] = 1.0 + nrm((N_ODD, DIFF_HEAD_DIM), 0.02)
    inp["diff_lq1"] = nrm((N_ODD, DIFF_HEAD_DIM), 0.1)
    inp["diff_lk1"] = nrm((N_ODD, DIFF_HEAD_DIM), 0.1)
    inp["diff_lq2"] = nrm((N_ODD, DIFF_HEAD_DIM), 0.1)
    inp["diff_lk2"] = nrm((N_ODD, DIFF_HEAD_DIM), 0.1)
    inp["diff_sub_norm_g"] = 1.0 + nrm((N_ODD, DIFF_V_DIM), 0.02)
    inp["od_w_out"] = nrm((N_ODD, D, D), D ** -0.5)
    inp["rel_bias"] = nrm((REL_BUCKETS, DIFF_HEADS), 0.5)
    inp["ffn_norm_g"] = 1.0 + nrm((DEPTH, D), 0.02)
    inp["peer_w_q"] = nrm((DEPTH, D, PEER_HEADS * 2 * PEER_KEY_DIM), D ** -0.5)
    inp["peer_sub_keys"] = nrm((DEPTH, PEER_HEADS, 2, PEER_N_KEYS, PEER_KEY_DIM), PEER_KEY_DIM ** -0.5)
    inp["peer_u"] = nrm((DEPTH, PEER_N_EXPERTS, D), D ** -0.5)
    inp["peer_v"] = nrm((DEPTH, PEER_N_EXPERTS, D), PEER_HEADS ** -0.5)
    return inp


def reference(x, ev_norm_g, ev_w_in, s5_log_step, s5_a_re, s5_a_im, s5_b_re, s5_b_im,
              s5_c_re, s5_c_im, s5_d, s5_glu_w, s5_glu_b, ret_gn_g, ret_gn_b, ev_w_out,
              od_norm_g, od_w_in, diff_q_norm_g, diff_k_norm_g, diff_lq1, diff_lk1,
              diff_lq2, diff_lk2, diff_sub_norm_g, od_w_out, rel_bias,
              ffn_norm_g, peer_w_q, peer_sub_keys, peer_u, peer_v):
    for layer in range(DEPTH):
        i = layer // 2
        if layer % 2 == 0:
            x = x + even_mixer(rmsnorm(x, ev_norm_g[i]), ev_w_in[i], s5_log_step[i],
                               s5_a_re[i], s5_a_im[i], s5_b_re[i], s5_b_im[i],
                               s5_c_re[i], s5_c_im[i], s5_d[i], s5_glu_w[i], s5_glu_b[i],
                               ret_gn_g[i], ret_gn_b[i], ev_w_out[i])
        else:
            lambda_init = 0.8 - 0.6 * math.exp(-0.3 * layer)
            x = x + odd_mixer(rmsnorm(x, od_norm_g[i]), od_w_in[i], diff_q_norm_g[i],
                              diff_k_norm_g[i], diff_lq1[i], diff_lk1[i], diff_lq2[i],
                              diff_lk2[i], diff_sub_norm_g[i], od_w_out[i], rel_bias,
                              lambda_init)
        x = x + peer(rmsnorm(x, ffn_norm_g[layer]), peer_w_q[layer], peer_sub_keys[layer],
                     peer_u[layer], peer_v[layer])
    return x
```

```python
import functools
import math

import jax
import jax.numpy as jnp
from jax import lax
from jax.experimental import pallas as pl
from jax.experimental.pallas import tpu as pltpu

F32 = jnp.float32
BF16 = jnp.bfloat16

EPS = 1e-6
LANES = 128
VMEM_LIMIT_BYTES = 56 * 1024 * 1024

S5_GROUP = 16
S5_STATE = 64
S5_CHUNK = 16
RET_HEADS = 8
RET_CHUNK = 128
ROPE_BASE = 10000.0
DIFF_HEADS = 16
DIFF_HEAD_DIM = 128
ATTN_BLOCK = 512
REL_BUCKETS = 32
REL_MAX_DIST = 128
PEER_HEADS = 8
PEER_N_KEYS = 128
PEER_TOPK = 16


def _cparams(*sem):
    return pltpu.CompilerParams(dimension_semantics=sem, vmem_limit_bytes=VMEM_LIMIT_BYTES)


def _dot(a, b):
    return jnp.dot(a, b, preferred_element_type=F32)


def _dot_nt(a, b):
    return lax.dot_general(a, b, (((1,), (1,)), ((), ())), preferred_element_type=F32)


def _dot_tn(a, b):
    return lax.dot_general(a, b, (((0,), (0,)), ((), ())), preferred_element_type=F32)


def _gelu(x):
    return 0.5 * x * (1.0 + jnp.tanh(0.7978845608028654 * (x + 0.044715 * (x * x * x))))


def _norm_kernel(x_ref, g_ref, o_ref):
    x = x_ref[...]
    r = lax.rsqrt(jnp.mean(x * x, axis=-1, keepdims=True) + EPS)
    o_ref[...] = (x * r * g_ref[...]).astype(o_ref.dtype)


def _rmsnorm(x, g, bt=256):
    t, d = x.shape
    return pl.pallas_call(
        _norm_kernel,
        grid=(t // bt,),
        in_specs=[pl.BlockSpec((bt, d), lambda i: (i, 0)), pl.BlockSpec((1, d), lambda i: (0, 0))],
        out_specs=pl.BlockSpec((bt, d), lambda i: (i, 0)),
        out_shape=jax.ShapeDtypeStruct((t, d), BF16),
        compiler_params=_cparams("parallel"),
        name="rmsnorm",
    )(x, g.reshape(1, d).astype(F32))


def _mm_kernel(*refs, n_lhs, has_resid, tile_major):
    a_refs, b_refs = refs[:n_lhs], refs[n_lhs:2 * n_lhs]
    r_ref = refs[2 * n_lhs] if has_resid else None
    o_ref = refs[-1]
    acc = _dot(a_refs[0][...], b_refs[0][...])
    for a_ref, b_ref in zip(a_refs[1:], b_refs[1:]):
        acc = acc + _dot(a_ref[...], b_ref[...])
    if has_resid:
        acc = acc + r_ref[...]
    if tile_major:
        for jj in range(o_ref.shape[0]):
            o_ref[jj] = acc[:, jj * LANES:(jj + 1) * LANES].astype(o_ref.dtype)
    else:
        o_ref[...] = acc.astype(o_ref.dtype)


def _matmul(lhs, rhs, *, resid=None, out_dtype=BF16, bm=512, bn=1024, tile_major=False, name="mm"):
    m, n = lhs[0].shape[0], rhs[0].shape[1]
    bm, bn = min(bm, m), min(bn, n)
    in_specs = [pl.BlockSpec((bm, a.shape[1]), lambda i, j: (i, 0)) for a in lhs]
    in_specs += [pl.BlockSpec((b.shape[0], bn), lambda i, j: (0, j)) for b in rhs]
    args = list(lhs) + list(rhs)
    if resid is not None:
        in_specs.append(pl.BlockSpec((bm, bn), lambda i, j: (i, j)))
        args.append(resid)
    if tile_major:
        out_spec = pl.BlockSpec((bn // LANES, bm, LANES), lambda i, j: (j, i, 0))
        out_shape = jax.ShapeDtypeStruct((n // LANES, m, LANES), out_dtype)
    else:
        out_spec = pl.BlockSpec((bm, bn), lambda i, j: (i, j))
        out_shape = jax.ShapeDtypeStruct((m, n), out_dtype)
    kern = functools.partial(_mm_kernel, n_lhs=len(lhs), has_resid=resid is not None, tile_major=tile_major)
    return pl.pallas_call(
        kern, grid=(m // bm, n // bn), in_specs=in_specs, out_specs=out_spec, out_shape=out_shape,
        compiler_params=_cparams("parallel", "parallel"), name=name,
    )(*args)


def _s5_tables(log_step, a_re, a_im, b_re, b_im, c_re, c_im, d_skip):
    L = S5_CHUNK
    g, p = a_re.shape
    gt = LANES // S5_GROUP
    nt = g // gt
    hp = lax.Precision.HIGHEST
    step = jnp.exp(log_step)[:, None]
    lr, li = a_re, a_im
    mag = jnp.exp(lr * step)
    abar_re = mag * jnp.cos(li * step)
    abar_im = mag * jnp.sin(li * step)
    den = lr * lr + li * li
    num_re = abar_re - 1.0
    f_re = (num_re * lr + abar_im * li) / den
    f_im = (abar_im * lr - num_re * li) / den
    bb_re = f_re[..., None] * b_re - f_im[..., None] * b_im
    bb_im = f_re[..., None] * b_im + f_im[..., None] * b_re
    k = jnp.arange(L + 1, dtype=F32)[:, None, None]
    pmag = jnp.exp(k * (lr * step)[None])
    pang = k * (li * step)[None]
    pw_re = pmag * jnp.cos(pang)
    pw_im = pmag * jnp.sin(pang)
    eye = jnp.eye(gt, dtype=F32)[None, None, :, None, None, :, None]

    cp_re = c_re[None] * pw_re[:L, :, None, :] - c_im[None] * pw_im[:L, :, None, :]
    cp_im = c_re[None] * pw_im[:L, :, None, :] + c_im[None] * pw_re[:L, :, None, :]
    kk = (jnp.einsum("dgop,gpi->dgoi", cp_re, bb_re, precision=hp)
          - jnp.einsum("dgop,gpi->dgoi", cp_im, bb_im, precision=hp))
    tau = jnp.arange(L)[:, None]
    tt = jnp.arange(L)[None, :]
    dd = tt - tau
    kt = kk[jnp.maximum(dd, 0)] * (dd >= 0).astype(F32)[..., None, None, None]
    kt = kt.reshape(L, L, nt, gt, S5_GROUP, S5_GROUP).transpose(2, 0, 3, 5, 1, 4)
    m_op = (kt[:, :, :, :, :, None, :] * eye).reshape(nt, L * LANES, L * LANES)

    rev = L - 1 - jnp.arange(L)
    pwr, pwi = pw_re[rev], pw_im[rev]
    pb_re = pwr[..., None] * bb_re[None] - pwi[..., None] * bb_im[None]
    pb_im = pwr[..., None] * bb_im[None] + pwi[..., None] * bb_re[None]
    pb = jnp.stack([pb_re, pb_im], 0).reshape(2, L, nt, gt, p, S5_GROUP)
    pb = pb.transpose(2, 1, 3, 5, 0, 4)
    p_op = (pb[:, :, :, :, :, None, :] * eye).reshape(nt, L * LANES, 2 * gt * p)

    qr, qi = pw_re[1:L + 1], pw_im[1:L + 1]
    cn_re = c_re[None] * qr[:, :, None, :] - c_im[None] * qi[:, :, None, :]
    cn_im = c_re[None] * qi[:, :, None, :] + c_im[None] * qr[:, :, None, :]
    nb = jnp.stack([cn_re, -cn_im], 0).reshape(2, L, nt, gt, S5_GROUP, p)
    nb = nb.transpose(2, 0, 3, 5, 1, 4)
    n_op = (nb[:, :, :, :, :, None, :] * eye).reshape(nt, 2 * gt * p, L * LANES)

    a_l = jnp.stack([pw_re[L], pw_im[L]], 0).reshape(2, nt, gt * p).transpose(1, 0, 2)
    d_t = jnp.tile(d_skip.reshape(nt, 1, LANES), (1, 1, L))
    return m_op.astype(BF16), p_op.astype(BF16), n_op.astype(BF16), a_l, d_t


def _s5_kernel(x_ref, m_ref, p_ref, n_ref, al_ref, d_ref, o_ref, sloc_ref, sprev_ref):
    x = x_ref[0]
    nc, ns = sloc_ref.shape
    half = ns // 2
    sloc_ref[...] = _dot(x, p_ref[0])
    a_re = al_ref[0, 0:1, :]
    a_im = al_ref[0, 1:2, :]

    def body(c, carry):
        s_re, s_im = carry
        row = pl.ds(c, 1)
        sprev_ref[row, 0:half] = s_re
        sprev_ref[row, half:ns] = s_im
        l_re = sloc_ref[row, 0:half]
        l_im = sloc_ref[row, half:ns]
        return (a_re * s_re - a_im * s_im + l_re, a_re * s_im + a_im * s_re + l_im)

    zero = jnp.zeros((1, half), F32)
    lax.fori_loop(0, nc, body, (zero, zero))
    y = _dot(x, m_ref[0]) + _dot(sprev_ref[...].astype(BF16), n_ref[0])
    y = y + d_ref[0] * x.astype(F32)
    o_ref[0] = _gelu(y).astype(o_ref.dtype)


def _s5_core(u_t, tables):
    m_op, p_op, n_op, a_l, d_t = tables
    nt, t, _ = u_t.shape
    L = S5_CHUNK
    nc, w, ns = t // L, L * LANES, p_op.shape[2]
    x = u_t.reshape(nt, nc, w)
    out = pl.pallas_call(
        _s5_kernel,
        grid=(nt,),
        in_specs=[
            pl.BlockSpec((1, nc, w), lambda j: (j, 0, 0)),
            pl.BlockSpec((1, w, w), lambda j: (j, 0, 0)),
            pl.BlockSpec((1, w, ns), lambda j: (j, 0, 0)),
            pl.BlockSpec((1, ns, w), lambda j: (j, 0, 0)),
            pl.BlockSpec((1, 2, ns // 2), lambda j: (j, 0, 0)),
            pl.BlockSpec((1, 1, w), lambda j: (j, 0, 0)),
        ],
        out_specs=pl.BlockSpec((1, nc, w), lambda j: (j, 0, 0)),
        out_shape=jax.ShapeDtypeStruct((nt, nc, w), BF16),
        scratch_shapes=[pltpu.VMEM((nc, ns), F32), pltpu.VMEM((nc, ns), F32)],
        compiler_params=_cparams("parallel"),
        name="s5_core",
    )(x, m_op, p_op, n_op, a_l, d_t)
    return out.reshape(nt, t, LANES)


def _glu_kernel(y_ref, w_ref, b_ref, o_ref):
    y = jnp.concatenate([y_ref[j] for j in range(y_ref.shape[0])], axis=1)
    acc = _dot(y, w_ref[...]) + b_ref[...]
    o_ref[...] = (y.astype(F32) * jax.nn.sigmoid(acc)).astype(o_ref.dtype)


def _glu(y_t, w, b, bm=512):
    nt, t, _ = y_t.shape
    width = nt * LANES
    return pl.pallas_call(
        _glu_kernel,
        grid=(t // bm,),
        in_specs=[
            pl.BlockSpec((nt, bm, LANES), lambda i: (0, i, 0)),
            pl.BlockSpec((width, width), lambda i: (0, 0)),
            pl.BlockSpec((1, width), lambda i: (0, 0)),
        ],
        out_specs=pl.BlockSpec((bm, width), lambda i: (i, 0)),
        out_shape=jax.ShapeDtypeStruct((t, width), BF16),
        compiler_params=_cparams("parallel"),
        name="s5_glu",
    )(y_t, w, b.reshape(1, width).astype(F32))


def _ret_kernel(q_ref, k_ref, v_ref, g_ref, cos_ref, sin_ref, intra_ref, qd_ref, kd_ref, cd_ref,
                gg_ref, gb_ref, o_ref, state_ref, *, chunk, scale):
    @pl.when(pl.program_id(1) == 0)
    def _():
        state_ref[...] = jnp.zeros_like(state_ref)

    intra, qd, kd, cd = intra_ref[0], qd_ref[0], kd_ref[0], cd_ref[0]
    gg, gb = gg_ref[...], gb_ref[...]
    half = cos_ref.shape[1]

    def body(c, carry):
        rows = pl.ds(pl.multiple_of(c * chunk, chunk), chunk)
        cos, sin = cos_ref[rows, :], sin_ref[rows, :]

        def rot(x):
            x1, x2 = x[:, :half], x[:, half:]
            return jnp.concatenate([x1 * cos - x2 * sin, x1 * sin + x2 * cos], axis=1)

        q = rot(q_ref[rows, :].astype(F32))
        k = rot(k_ref[rows, :].astype(F32)) * scale
        v = v_ref[rows, :]
        st = state_ref[...]
        scores = _dot_nt(q.astype(BF16), k.astype(BF16)) * intra
        out = _dot(scores.astype(BF16), v) + _dot((q * qd).astype(BF16), st.astype(BF16))
        state_ref[...] = cd * st + _dot_tn((k * kd).astype(BF16), v)
        mu = jnp.mean(out, axis=-1, keepdims=True)
        cen = out - mu
        var = jnp.mean(cen * cen, axis=-1, keepdims=True)
        o = cen * lax.rsqrt(var + EPS) * gg + gb
        gt = g_ref[rows, :].astype(F32)
        o_ref[rows, :] = (gt * jax.nn.sigmoid(gt) * o).astype(o_ref.dtype)
        return carry

    lax.fori_loop(0, q_ref.shape[0] // chunk, body, 0)


def _retention(z4, gn_g, gn_b, tb=2048):
    t = z4.shape[0]
    w = z4.shape[1] // 4
    hd = w // RET_HEADS
    half = hd // 2
    c = RET_CHUNK
    tb = min(tb, t)
    pos = jnp.arange(t, dtype=F32)
    freqs = ROPE_BASE ** (-jnp.arange(half, dtype=F32) / half)
    ang = pos[:, None] * freqs[None, :]
    cos, sin = jnp.cos(ang), jnp.sin(ang)
    gamma = 1.0 - 2.0 ** (-5.0 - jnp.arange(RET_HEADS, dtype=F32))
    log_g = jnp.log(gamma)
    idx = jnp.arange(c, dtype=F32)
    rel = idx[:, None] - idx[None, :]
    intra = jnp.where(rel >= 0, jnp.exp(log_g[:, None, None] * jnp.maximum(rel, 0.0)), 0.0)
    q_decay = jnp.exp(log_g[:, None] * (idx + 1.0))[..., None]
    k_decay = jnp.exp(log_g[:, None] * (c - 1.0 - idx))[..., None]
    chunk_decay = jnp.exp(log_g * c)[:, None, None]
    nh = RET_HEADS
    blk = lambda off: pl.BlockSpec((tb, hd), lambda h, s: (s, off + h))
    per_head = lambda shape: pl.BlockSpec((1,) + shape, lambda h, s: (h, 0, 0))
    return pl.pallas_call(
        functools.partial(_ret_kernel, chunk=c, scale=hd ** -0.5),
        grid=(nh, t // tb),
        in_specs=[
            blk(0), blk(nh), blk(2 * nh), blk(3 * nh),
            pl.BlockSpec((tb, half), lambda h, s: (s, 0)),
            pl.BlockSpec((tb, half), lambda h, s: (s, 0)),
            per_head((c, c)), per_head((c, 1)), per_head((c, 1)), per_head((1, 1)),
            pl.BlockSpec((1, hd), lambda h, s: (0, h)),
            pl.BlockSpec((1, hd), lambda h, s: (0, h)),
        ],
        out_specs=pl.BlockSpec((tb, hd), lambda h, s: (s, h)),
        out_shape=jax.ShapeDtypeStruct((t, w), BF16),
        scratch_shapes=[pltpu.VMEM((hd, hd), F32)],
        compiler_params=_cparams("parallel", "arbitrary"),
        name="retention",
    )(z4, z4, z4, z4, cos, sin, intra, q_decay, k_decay, chunk_decay,
      gn_g.reshape(1, w).astype(F32), gn_b.reshape(1, w).astype(F32))


def _qknorm_kernel(x_ref, gain_ref, ones_ref, o_ref, *, inv_dim):
    x = x_ref[...].astype(F32)
    xx = x * x
    hi = xx.astype(BF16)
    lo = (xx - hi.astype(F32)).astype(BF16)
    ss = _dot(hi, ones_ref[...]) + _dot(lo, ones_ref[...])
    o_ref[...] = (x * lax.rsqrt(ss * inv_dim + EPS) * gain_ref[...]).astype(o_ref.dtype)


def _qknorm(z, gain, bt=1024, bw=256):
    t = z.shape[0]
    w = gain.shape[0]
    bt = min(bt, t)
    grp = jnp.arange(bw) // DIFF_HEAD_DIM
    ones = (grp[:, None] == grp[None, :]).astype(BF16)
    return pl.pallas_call(
        functools.partial(_qknorm_kernel, inv_dim=1.0 / DIFF_HEAD_DIM),
        grid=(t // bt, w // bw),
        in_specs=[
            pl.BlockSpec((bt, bw), lambda i, j: (i, j)),
            pl.BlockSpec((1, bw), lambda i, j: (0, j)),
            pl.BlockSpec((bw, bw), lambda i, j: (0, 0)),
        ],
        out_specs=pl.BlockSpec((bt, bw), lambda i, j: (i, j)),
        out_shape=jax.ShapeDtypeStruct((t, w), BF16),
        compiler_params=_cparams("parallel", "parallel"),
        name="qk_norm",
    )(z, gain.reshape(1, w).astype(F32), ones)


def _attn_kernel(q_ref, k_ref, v_ref, d0_ref, d1_ref, lam_ref, sg_ref, o_ref, m_ref, l_ref, acc_ref,
                 *, blk, out_scale):
    i = pl.program_id(1)
    hd = DIFF_HEAD_DIM
    m_ref[...] = jnp.full(m_ref.shape, -1e30, F32)
    l_ref[...] = jnp.zeros_like(l_ref)
    acc_ref[...] = jnp.zeros_like(acc_ref)

    def step(j, bias):
        rows = pl.ds(pl.multiple_of(j * blk, blk), blk)
        v = v_ref[rows, :]
        for mi in range(2):
            s = _dot_nt(q_ref[:, mi * hd:(mi + 1) * hd], k_ref[rows, mi * hd:(mi + 1) * hd])
            if bias is not None:
                s = s + bias
            m_old = m_ref[mi]
            m_new = jnp.maximum(m_old, jnp.max(s, axis=-1, keepdims=True))
            p = jnp.exp(s - m_new)
            alpha = jnp.exp(m_old - m_new)
            l_ref[mi] = alpha * l_ref[mi] + jnp.sum(p, axis=-1, keepdims=True)
            acc_ref[mi] = alpha * acc_ref[mi] + _dot(p.astype(BF16), v)
            m_ref[mi] = m_new

    def far(j, carry):
        step(j, None)
        return carry

    lax.fori_loop(0, jnp.maximum(i - 1, 0), far, 0)

    @pl.when(i >= 1)
    def _():
        step(i - 1, d1_ref[0])

    step(i, d0_ref[0])
    o = acc_ref[0] / l_ref[0] - lam_ref[...] * (acc_ref[1] / l_ref[1])
    ms = jnp.mean(o * o, axis=-1, keepdims=True)
    o_ref[...] = (o * lax.rsqrt(ms + EPS) * sg_ref[...] * out_scale).astype(o_ref.dtype)


def _t5_bucket(n):
    max_exact = REL_BUCKETS // 2
    nf = jnp.maximum(n, 1).astype(F32)
    large = max_exact + (jnp.log(nf / max_exact) / math.log(REL_MAX_DIST / max_exact)
                         * (REL_BUCKETS - max_exact)).astype(jnp.int32)
    large = jnp.minimum(large, REL_BUCKETS - 1)
    return jnp.where(n < max_exact, n, large)


def _diff_attention(qkn, z, rel_bias, lam, sub_g, lambda_init):
    t = qkn.shape[0]
    nh = DIFF_HEADS
    dv = 2 * DIFF_HEAD_DIM
    blk = min(ATTN_BLOCK, t)
    assert blk >= REL_MAX_DIST
    r = jnp.arange(blk)[:, None]
    c = jnp.arange(blk)[None, :]
    far_bias = rel_bias[REL_BUCKETS - 1]
    rel0 = r - c
    d0 = rel_bias[_t5_bucket(jnp.maximum(rel0, 0))] - far_bias
    d0 = jnp.where((rel0 >= 0)[..., None], d0, -jnp.inf).transpose(2, 0, 1)
    d1 = (rel_bias[_t5_bucket(blk + r - c)] - far_bias).transpose(2, 0, 1)
    lam_row = jnp.full((1, dv), lam, F32)
    return pl.pallas_call(
        functools.partial(_attn_kernel, blk=blk, out_scale=1.0 - lambda_init),
        grid=(nh, t // blk),
        in_specs=[
            pl.BlockSpec((blk, dv), lambda h, i: (i, h)),
            pl.BlockSpec((t, dv), lambda h, i: (0, nh + h)),
            pl.BlockSpec((t, dv), lambda h, i: (0, 2 * nh + h)),
            pl.BlockSpec((1, blk, blk), lambda h, i: (h, 0, 0)),
            pl.BlockSpec((1, blk, blk), lambda h, i: (h, 0, 0)),
            pl.BlockSpec((1, dv), lambda h, i: (0, 0)),
            pl.BlockSpec((1, dv), lambda h, i: (0, 0)),
        ],
        out_specs=pl.BlockSpec((blk, dv), lambda h, i: (i, h)),
        out_shape=jax.ShapeDtypeStruct((t, nh * dv), BF16),
        scratch_shapes=[pltpu.VMEM((2, blk, 1), F32), pltpu.VMEM((2, blk, 1), F32),
                        pltpu.VMEM((2, blk, dv), F32)],
        compiler_params=_cparams("parallel", "parallel"),
        name="diff_attn",
    )(qkn, qkn, z, d0.astype(F32), d1.astype(F32), lam_row, sub_g.reshape(1, dv).astype(F32))


def _pscore_kernel(w_ref, h_ref, sk_ref, o_ref):
    q_t = _dot_nt(w_ref[...], h_ref[...])
    for b in range(sk_ref.shape[0]):
        rows = slice(b * LANES, (b + 1) * LANES)
        o_ref[rows, :] = _dot(sk_ref[b], q_t[rows, :].astype(BF16))


def _peer_scores(hq, wq_t, sub_keys, tm=512, bn=512):
    t, d = hq.shape
    n = wq_t.shape[0]
    tm = min(tm, t)
    nb = bn // LANES
    return pl.pallas_call(
        _pscore_kernel,
        grid=(t // tm, n // bn),
        in_specs=[
            pl.BlockSpec((bn, d), lambda i, j: (j, 0)),
            pl.BlockSpec((tm, d), lambda i, j: (i, 0)),
            pl.BlockSpec((nb, LANES, LANES), lambda i, j: (j, 0, 0)),
        ],
        out_specs=pl.BlockSpec((bn, tm), lambda i, j: (j, i)),
        out_shape=jax.ShapeDtypeStruct((n, t), F32),
        compiler_params=_cparams("parallel", "parallel"),
        name="peer_scores",
    )(wq_t, hq, sub_keys)


def _top_values(s, n):
    vals = []
    for r in range(n):
        m = jnp.max(s, axis=0, keepdims=True)
        vals.append(m)
        if r < n - 1:
            s = jnp.where(s == m, -jnp.inf, s)
    return vals


def _route_kernel(s_ref, e1_ref, e2_ref, c_ref):
    nk = PEER_N_KEYS
    k = PEER_TOPK
    tl = s_ref.shape[1]
    pad = 24
    row = lax.broadcasted_iota(jnp.int32, (pad, tl), 0)
    hrow = lax.broadcasted_iota(jnp.int32, (PEER_HEADS, tl), 0)
    c_all = jnp.zeros((PEER_HEADS, tl), F32)
    for h in range(PEER_HEADS):
        s1 = s_ref[2 * h * nk:(2 * h + 1) * nk, :]
        s2 = s_ref[(2 * h + 1) * nk:(2 * h + 2) * nk, :]
        a = _top_values(s1, k + 1)
        b = _top_values(s2, k + 1)
        ea = [jnp.exp(x - a[0]) for x in a]
        eb = [jnp.exp(x - b[0]) for x in b]
        ea_m = jnp.full((pad, tl), -1.0, F32)
        eb_m = jnp.full((pad, tl), -1.0, F32)
        for r in range(k + 1):
            ea_m = jnp.where(row == r, ea[r], ea_m)
            eb_m = jnp.where(row == r, eb[r], eb_m)
        cand = jnp.concatenate(
            [ea[0] * eb_m, ea[1] * eb_m[:8], ea[2] * eb_m[:8], ea[3] * eb_m[:8],
             eb[0] * ea_m, eb[1] * ea_m[:8], eb[2] * ea_m[:8]], axis=0)
        v = _top_values(cand, k + 1)
        z = v[0]
        for r in range(1, k):
            z = z + v[r]
        rz = 1.0 / z
        e1_ref[h] = jnp.exp(s1 - a[0])
        e2_ref[h] = jnp.exp(s2 - b[0]) * rz
        c_all = jnp.where(hrow == h, 0.5 * (v[k - 1] + v[k]) * rz, c_all)
    c_ref[...] = c_all


def _peer_route(s_t, tl=256):
    n, t = s_t.shape
    tl = min(tl, t)
    nh, nk = PEER_HEADS, PEER_N_KEYS
    return pl.pallas_call(
        _route_kernel,
        grid=(t // tl,),
        in_specs=[pl.BlockSpec((n, tl), lambda i: (0, i))],
        out_specs=[
            pl.BlockSpec((nh, nk, tl), lambda i: (0, 0, i)),
            pl.BlockSpec((nh, nk, tl), lambda i: (0, 0, i)),
            pl.BlockSpec((nh, tl), lambda i: (0, i)),
        ],
        out_shape=[
            jax.ShapeDtypeStruct((nh, nk, t), F32),
            jax.ShapeDtypeStruct((nh, nk, t), F32),
            jax.ShapeDtypeStruct((nh, t), F32),
        ],
        compiler_params=_cparams("parallel"),
        name="peer_route",
    )(s_t)


def _peer_dense_kernel(u_ref, v_ref, h_ref, x_ref, e1_ref, e2_ref, c_ref, o_ref, w_ref):
    @pl.when(pl.program_id(1) == 0)
    def _():
        o_ref[...] = x_ref[...]

    a_t = _dot_nt(u_ref[...], h_ref[...])
    tm = a_t.shape[1]
    for ii in range(e1_ref.shape[0]):
        rows = slice(ii * LANES, (ii + 1) * LANES)
        g = jnp.zeros((LANES, tm), F32)
        for h in range(PEER_HEADS):
            vv = e2_ref[h] * e1_ref[ii, h:h + 1, :]
            g = g + jnp.where(vv >= c_ref[h:h + 1, :], vv, 0.0)
        w_ref[rows, :] = (g * _gelu(a_t[rows, :])).astype(w_ref.dtype)
    o_ref[...] += _dot_tn(w_ref[...], v_ref[...])


def _peer_dense(x, hq, u_tab, v_tab, e1_t, e2, c, tm=512, te=512):
    t, d = x.shape
    ne = u_tab.shape[0]
    tm = min(tm, t)
    nh, nk = PEER_HEADS, PEER_N_KEYS
    once = pl.Buffered(1)
    return pl.pallas_call(
        _peer_dense_kernel,
        grid=(t // tm, ne // te),
        in_specs=[
            pl.BlockSpec((te, d), lambda i, e: (e, 0)),
            pl.BlockSpec((te, d), lambda i, e: (e, 0)),
            pl.BlockSpec((tm, d), lambda i, e: (i, 0), pipeline_mode=once),
            pl.BlockSpec((tm, d), lambda i, e: (i, 0), pipeline_mode=once),
            pl.BlockSpec((te // nk, nh, tm), lambda i, e: (e, 0, i)),
            pl.BlockSpec((nh, nk, tm), lambda i, e: (0, 0, i), pipeline_mode=once),
            pl.BlockSpec((nh, tm), lambda i, e: (0, i)),
        ],
        out_specs=pl.BlockSpec((tm, d), lambda i, e: (i, 0)),
        out_shape=jax.ShapeDtypeStruct((t, d), F32),
        scratch_shapes=[pltpu.VMEM((te, tm), BF16)],
        compiler_params=_cparams("parallel", "arbitrary"),
        name="peer_dense",
    )(u_tab, v_tab, hq, x, e1_t, e2, c)


def _peer(x, norm_g, w_q, sub_keys, u_tab, v_tab):
    hq = _rmsnorm(x, norm_g)
    nh, nk = PEER_HEADS, PEER_N_KEYS
    s_t = _peer_scores(hq, w_q.T.astype(BF16), sub_keys.reshape(2 * nh, nk, -1).astype(BF16))
    e1, e2, c = _peer_route(s_t)
    return _peer_dense(x, hq, u_tab.astype(BF16), v_tab.astype(BF16), e1.transpose(1, 0, 2), e2, c)


def _even_layer(x, norm_g, w_in, log_step, a_re, a_im, b_re, b_im, c_re, c_im, d_skip,
                glu_w, glu_b, gn_g, gn_b, w_out):
    s5w = d_skip.shape[0]
    h = _rmsnorm(x, norm_g)
    u_t = _matmul([h], [w_in[:, :s5w].astype(BF16)], tile_major=True, name="ev_in_s5")
    z4 = _matmul([h], [w_in[:, s5w:].astype(BF16)], name="ev_in_ret")
    tables = _s5_tables(log_step, a_re, a_im, b_re, b_im, c_re, c_im, d_skip)
    ya = _glu(_s5_core(u_t, tables), glu_w.astype(BF16), glu_b)
    yb = _retention(z4, gn_g, gn_b)
    w_out = w_out.astype(BF16)
    return _matmul([ya, yb], [w_out[:s5w], w_out[s5w:]], resid=x, out_dtype=F32, name="ev_out")


def _odd_layer(x, norm_g, w_in, q_norm_g, k_norm_g, lq1, lk1, lq2, lk2, sub_norm_g, w_out,
               rel_bias, lambda_init):
    d = x.shape[1]
    h = _rmsnorm(x, norm_g)
    z = _matmul([h], [w_in.astype(BF16)], name="od_in")
    reps = d // DIFF_HEAD_DIM
    gain = jnp.concatenate([jnp.tile(q_norm_g, reps) * (DIFF_HEAD_DIM ** -0.5), jnp.tile(k_norm_g, reps)])
    qkn = _qknorm(z, gain)
    lam = jnp.exp(jnp.sum(lq1 * lk1)) - jnp.exp(jnp.sum(lq2 * lk2)) + lambda_init
    o = _diff_attention(qkn, z, rel_bias, lam, sub_norm_g, lambda_init)
    return _matmul([o], [w_out.astype(BF16)], resid=x, out_dtype=F32, name="od_out")


def kernel(x, ev_norm_g, ev_w_in, s5_log_step, s5_a_re, s5_a_im, s5_b_re, s5_b_im, s5_c_re, s5_c_im,
           s5_d, s5_glu_w, s5_glu_b, ret_gn_g, ret_gn_b, ev_w_out, od_norm_g, od_w_in, diff_q_norm_g,
           diff_k_norm_g, diff_lq1, diff_lk1, diff_lq2, diff_lk2, diff_sub_norm_g, od_w_out, rel_bias,
           ffn_norm_g, peer_w_q, peer_sub_keys, peer_u, peer_v):
    bsz, t, d = x.shape
    depth = ffn_norm_g.shape[0]
    outs = []
    for b in range(bsz):
        xb = x[b]
        for layer in range(depth):
            i = layer // 2
            if layer % 2 == 0:
                xb = _even_layer(xb, ev_norm_g[i], ev_w_in[i], s5_log_step[i], s5_a_re[i], s5_a_im[i],
                                 s5_b_re[i], s5_b_im[i], s5_c_re[i], s5_c_im[i], s5_d[i], s5_glu_w[i],
                                 s5_glu_b[i], ret_gn_g[i], ret_gn_b[i], ev_w_out[i])
            else:
                lambda_init = 0.8 - 0.6 * math.exp(-0.3 * layer)
                xb = _odd_layer(xb, od_norm_g[i], od_w_in[i], diff_q_norm_g[i], diff_k_norm_g[i],
                                diff_lq1[i], diff_lk1[i], diff_lq2[i], diff_lk2[i], diff_sub_norm_g[i],
                                od_w_out[i], rel_bias, lambda_init)
            xb = _peer(xb, ffn_norm_g[layer], peer_w_q[layer], peer_sub_keys[layer], peer_u[layer],
                       peer_v[layer])
        outs.append(xb)
    return jnp.stack(outs, 0)
```

```python
import functools
import math

import jax
import jax.numpy as jnp
from jax import lax
from jax.experimental import pallas as pl
from jax.experimental.pallas import tpu as pltpu

F32 = jnp.float32
BF16 = jnp.bfloat16

EPS = 1e-6
LOG2E = 1.4426950408889634
LANES = 128
VMEM_LIMIT_BYTES = 56 * 1024 * 1024

S5_GROUP = 16
S5_STATE = 64
S5_CHUNK = 16
RET_HEADS = 8
RET_CHUNK = 128
ROPE_BASE = 10000.0
DIFF_HEADS = 16
DIFF_HEAD_DIM = 128
ATTN_BLOCK = 512
REL_BUCKETS = 32
REL_MAX_DIST = 128
PEER_HEADS = 8
PEER_N_KEYS = 128
PEER_TOPK = 16


def _cparams(*sem):
    return pltpu.CompilerParams(dimension_semantics=sem, vmem_limit_bytes=VMEM_LIMIT_BYTES)


def _dot(a, b):
    return jnp.dot(a, b, preferred_element_type=F32)


def _dot_nt(a, b):
    return lax.dot_general(a, b, (((1,), (1,)), ((), ())), preferred_element_type=F32)


def _dot_tn(a, b):
    return lax.dot_general(a, b, (((0,), (0,)), ((), ())), preferred_element_type=F32)


def _gelu(x):
    return 0.5 * x * (1.0 + jnp.tanh(0.7978845608028654 * (x + 0.044715 * (x * x * x))))


def _norm_kernel(x_ref, g_ref, o_ref):
    x = x_ref[...]
    r = lax.rsqrt(jnp.mean(x * x, axis=-1, keepdims=True) + EPS)
    o_ref[...] = (x * r * g_ref[...]).astype(o_ref.dtype)


def _rmsnorm(x, g, bt=256):
    t, d = x.shape
    return pl.pallas_call(
        _norm_kernel,
        grid=(t // bt,),
        in_specs=[pl.BlockSpec((bt, d), lambda i: (i, 0)), pl.BlockSpec((1, d), lambda i: (0, 0))],
        out_specs=pl.BlockSpec((bt, d), lambda i: (i, 0)),
        out_shape=jax.ShapeDtypeStruct((t, d), BF16),
        compiler_params=_cparams("parallel"),
        name="rmsnorm",
    )(x, g.reshape(1, d).astype(F32))


def _mm_kernel(*refs, n_lhs, has_resid, tile_major):
    a_refs, b_refs = refs[:n_lhs], refs[n_lhs:2 * n_lhs]
    r_ref = refs[2 * n_lhs] if has_resid else None
    o_ref = refs[-1]
    acc = _dot(a_refs[0][...], b_refs[0][...])
    for a_ref, b_ref in zip(a_refs[1:], b_refs[1:]):
        acc = acc + _dot(a_ref[...], b_ref[...])
    if has_resid:
        acc = acc + r_ref[...]
    if tile_major:
        for jj in range(o_ref.shape[0]):
            o_ref[jj] = acc[:, jj * LANES:(jj + 1) * LANES].astype(o_ref.dtype)
    else:
        o_ref[...] = acc.astype(o_ref.dtype)


def _matmul(lhs, rhs, *, resid=None, out_dtype=BF16, bm=512, bn=1024, tile_major=False, name="mm"):
    m, n = lhs[0].shape[0], rhs[0].shape[1]
    bm, bn = min(bm, m), min(bn, n)
    in_specs = [pl.BlockSpec((bm, a.shape[1]), lambda i, j: (i, 0)) for a in lhs]
    in_specs += [pl.BlockSpec((b.shape[0], bn), lambda i, j: (0, j)) for b in rhs]
    args = list(lhs) + list(rhs)
    if resid is not None:
        in_specs.append(pl.BlockSpec((bm, bn), lambda i, j: (i, j)))
        args.append(resid)
    if tile_major:
        out_spec = pl.BlockSpec((bn // LANES, bm, LANES), lambda i, j: (j, i, 0))
        out_shape = jax.ShapeDtypeStruct((n // LANES, m, LANES), out_dtype)
    else:
        out_spec = pl.BlockSpec((bm, bn), lambda i, j: (i, j))
        out_shape = jax.ShapeDtypeStruct((m, n), out_dtype)
    kern = functools.partial(_mm_kernel, n_lhs=len(lhs), has_resid=resid is not None, tile_major=tile_major)
    return pl.pallas_call(
        kern, grid=(m // bm, n // bn), in_specs=in_specs, out_specs=out_spec, out_shape=out_shape,
        compiler_params=_cparams("parallel", "parallel"), name=name,
    )(*args)


def _mm_nt_kernel(w_ref, h_ref, o_ref):
    o_ref[...] = _dot_nt(w_ref[...], h_ref[...]).astype(o_ref.dtype)


def _matmul_nt(w_t, h, *, bm=512, bn=1024, out_dtype=BF16, name="mm_nt"):
    n, k = w_t.shape
    m = h.shape[0]
    bm, bn = min(bm, m), min(bn, n)
    return pl.pallas_call(
        _mm_nt_kernel,
        grid=(m // bm, n // bn),
        in_specs=[pl.BlockSpec((bn, k), lambda i, j: (j, 0)), pl.BlockSpec((bm, k), lambda i, j: (i, 0))],
        out_specs=pl.BlockSpec((bn, bm), lambda i, j: (j, i)),
        out_shape=jax.ShapeDtypeStruct((n, m), out_dtype),
        compiler_params=_cparams("parallel", "parallel"),
        name=name,
    )(w_t, h)


def _s5_tables(log_step, a_re, a_im, b_re, b_im, c_re, c_im, d_skip):
    L = S5_CHUNK
    g, p = a_re.shape
    gt = LANES // S5_GROUP
    nt = g // gt
    hp = lax.Precision.HIGHEST
    step = jnp.exp(log_step)[:, None]
    lr, li = a_re, a_im
    mag = jnp.exp(lr * step)
    abar_re = mag * jnp.cos(li * step)
    abar_im = mag * jnp.sin(li * step)
    den = lr * lr + li * li
    num_re = abar_re - 1.0
    f_re = (num_re * lr + abar_im * li) / den
    f_im = (abar_im * lr - num_re * li) / den
    bb_re = f_re[..., None] * b_re - f_im[..., None] * b_im
    bb_im = f_re[..., None] * b_im + f_im[..., None] * b_re
    k = jnp.arange(L + 1, dtype=F32)[:, None, None]
    pmag = jnp.exp(k * (lr * step)[None])
    pang = k * (li * step)[None]
    pw_re = pmag * jnp.cos(pang)
    pw_im = pmag * jnp.sin(pang)
    ns = 2 * gt * p

    def same_group(row_g, col_g):
        return (row_g[:, None] == col_g[None, :]).astype(F32)

    cp_re = c_re[None] * pw_re[:L, :, None, :] - c_im[None] * pw_im[:L, :, None, :]
    cp_im = c_re[None] * pw_im[:L, :, None, :] + c_im[None] * pw_re[:L, :, None, :]
    kk = (jnp.einsum("dgop,gpi->dgoi", cp_re, bb_re, precision=hp)
          - jnp.einsum("dgop,gpi->dgoi", cp_im, bb_im, precision=hp))
    kki = kk.transpose(0, 1, 3, 2).reshape(L, nt, LANES, S5_GROUP)
    lane_g = jnp.arange(LANES) // S5_GROUP
    bd = jnp.tile(kki, (1, 1, 1, gt)) * same_group(lane_g, lane_g)
    bd = bd.transpose(1, 0, 2, 3).astype(BF16)

    k_rev = (L - 1) - k[:L]
    pmag_rev = jnp.exp(k_rev * (lr * step)[None])
    pang_rev = k_rev * (li * step)[None]
    pwr, pwi = pmag_rev * jnp.cos(pang_rev), pmag_rev * jnp.sin(pang_rev)
    pb_re = pwr[..., None] * bb_re[None] - pwi[..., None] * bb_im[None]
    pb_im = pwr[..., None] * bb_im[None] + pwi[..., None] * bb_re[None]
    pb = jnp.stack([pb_re, pb_im], 0).reshape(2, L, nt, gt, p, S5_GROUP)
    ps = pb.transpose(2, 1, 3, 5, 0, 4).reshape(nt, L * LANES, 2 * p).astype(BF16)
    rr, cc = jnp.arange(2 * p), jnp.arange(ns)
    ex_p = ((rr[:, None] // p == cc[None, :] // (gt * p)) & (rr[:, None] % p == cc[None, :] % p))
    row_g = (jnp.arange(L * LANES) % LANES) // S5_GROUP
    col_g = (cc % (gt * p)) // p
    p_op = (jnp.einsum("jrk,kc->jrc", ps, ex_p.astype(BF16), preferred_element_type=F32)
            * same_group(row_g, col_g)).astype(BF16)

    qr, qi = pw_re[1:L + 1], pw_im[1:L + 1]
    cn_re = c_re[None] * qr[:, :, None, :] - c_im[None] * qi[:, :, None, :]
    cn_im = c_re[None] * qi[:, :, None, :] + c_im[None] * qr[:, :, None, :]
    nb = jnp.stack([cn_re, -cn_im], 0).reshape(2, L, nt, gt, S5_GROUP, p)
    nsm = nb.transpose(2, 0, 3, 5, 1, 4).reshape(nt, ns, L * S5_GROUP).astype(BF16)
    rr, cc = jnp.arange(L * S5_GROUP), jnp.arange(L * LANES)
    ex_n = ((rr[:, None] // S5_GROUP == cc[None, :] // LANES)
            & (rr[:, None] % S5_GROUP == cc[None, :] % S5_GROUP))
    row_g = (jnp.arange(ns) % (gt * p)) // p
    col_g = (cc % LANES) // S5_GROUP
    n_op = (jnp.einsum("jrk,kc->jrc", nsm, ex_n.astype(BF16), preferred_element_type=F32)
            * same_group(row_g, col_g)).astype(BF16)

    a_l = jnp.stack([pw_re[L], pw_im[L]], 0).reshape(2, nt, gt * p).transpose(1, 0, 2)
    d_t = jnp.tile(d_skip.reshape(nt, 1, LANES), (1, 1, L))
    return bd, p_op, n_op, a_l, d_t


def _s5_kernel(x_ref, bd_ref, p_ref, n_ref, al_ref, d_ref, o_ref, m_ref, sloc_ref, sprev_ref):
    x = x_ref[0]
    nc, ns = sloc_ref.shape
    half = ns // 2
    L = bd_ref.shape[1]
    for tau in range(L):
        for t in range(L):
            blk = bd_ref[0, t - tau] if t >= tau else jnp.zeros((LANES, LANES), m_ref.dtype)
            m_ref[tau * LANES:(tau + 1) * LANES, t * LANES:(t + 1) * LANES] = blk
    sloc_ref[...] = _dot(x, p_ref[0])
    a_re = al_ref[0, 0:1, :]
    a_im = al_ref[0, 1:2, :]

    def body(c, carry):
        s_re, s_im = carry
        row = pl.ds(c, 1)
        sprev_ref[row, 0:half] = s_re
        sprev_ref[row, half:ns] = s_im
        l_re = sloc_ref[row, 0:half]
        l_im = sloc_ref[row, half:ns]
        return (a_re * s_re - a_im * s_im + l_re, a_re * s_im + a_im * s_re + l_im)

    zero = jnp.zeros((1, half), F32)
    lax.fori_loop(0, nc, body, (zero, zero))
    y = _dot(x, m_ref[...]) + _dot(sprev_ref[...].astype(BF16), n_ref[0])
    y = y + d_ref[0] * x.astype(F32)
    o_ref[0] = _gelu(y).astype(o_ref.dtype)


def _s5_core(u_t, tables):
    bd, p_op, n_op, a_l, d_t = tables
    nt, t, _ = u_t.shape
    L = S5_CHUNK
    nc, w, ns = t // L, L * LANES, p_op.shape[2]
    x = u_t.reshape(nt, nc, w)
    out = pl.pallas_call(
        _s5_kernel,
        grid=(nt,),
        in_specs=[
            pl.BlockSpec((1, nc, w), lambda j: (j, 0, 0)),
            pl.BlockSpec((1, L, LANES, LANES), lambda j: (j, 0, 0, 0)),
            pl.BlockSpec((1, w, ns), lambda j: (j, 0, 0)),
            pl.BlockSpec((1, ns, w), lambda j: (j, 0, 0)),
            pl.BlockSpec((1, 2, ns // 2), lambda j: (j, 0, 0)),
            pl.BlockSpec((1, 1, w), lambda j: (j, 0, 0)),
        ],
        out_specs=pl.BlockSpec((1, nc, w), lambda j: (j, 0, 0)),
        out_shape=jax.ShapeDtypeStruct((nt, nc, w), BF16),
        scratch_shapes=[pltpu.VMEM((w, w), BF16), pltpu.VMEM((nc, ns), F32), pltpu.VMEM((nc, ns), F32)],
        compiler_params=_cparams("parallel"),
        name="s5_core",
    )(x, bd, p_op, n_op, a_l, d_t)
    return out.reshape(nt, t, LANES)


def _glu_kernel(y_ref, w_ref, b_ref, o_ref):
    y = jnp.concatenate([y_ref[j] for j in range(y_ref.shape[0])], axis=1)
    acc = _dot(y, w_ref[...]) + b_ref[...]
    o_ref[...] = (y.astype(F32) * jax.nn.sigmoid(acc)).astype(o_ref.dtype)


def _glu(y_t, w, b, bm=512):
    nt, t, _ = y_t.shape
    width = nt * LANES
    return pl.pallas_call(
        _glu_kernel,
        grid=(t // bm,),
        in_specs=[
            pl.BlockSpec((nt, bm, LANES), lambda i: (0, i, 0)),
            pl.BlockSpec((width, width), lambda i: (0, 0)),
            pl.BlockSpec((1, width), lambda i: (0, 0)),
        ],
        out_specs=pl.BlockSpec((bm, width), lambda i: (i, 0)),
        out_shape=jax.ShapeDtypeStruct((t, width), BF16),
        compiler_params=_cparams("parallel"),
        name="s5_glu",
    )(y_t, w, b.reshape(1, width).astype(F32))


def _ret_kernel(q_ref, k_ref, v_ref, g_ref, cos_ref, sin_ref, intra_ref, qd_ref, kd_ref, cd_ref,
                gg_ref, gb_ref, o_ref, state_ref, *, chunk, scale):
    @pl.when(pl.program_id(1) == 0)
    def _():
        state_ref[...] = jnp.zeros_like(state_ref)

    intra, qd, kd, cd = intra_ref[0], qd_ref[0], kd_ref[0], cd_ref[0]
    gg, gb = gg_ref[...], gb_ref[...]
    half = cos_ref.shape[1]

    def body(c, carry):
        rows = pl.ds(pl.multiple_of(c * chunk, chunk), chunk)
        cos, sin = cos_ref[rows, :], sin_ref[rows, :]

        def rot(x):
            x1, x2 = x[:, :half], x[:, half:]
            return jnp.concatenate([x1 * cos - x2 * sin, x1 * sin + x2 * cos], axis=1)

        q = rot(q_ref[rows, :].astype(F32))
        k = rot(k_ref[rows, :].astype(F32)) * scale
        v = v_ref[rows, :]
        st = state_ref[...]
        scores = _dot_nt(q.astype(BF16), k.astype(BF16)) * intra
        out = _dot(scores.astype(BF16), v) + _dot((q * qd).astype(BF16), st.astype(BF16))
        state_ref[...] = cd * st + _dot_tn((k * kd).astype(BF16), v)
        mu = jnp.mean(out, axis=-1, keepdims=True)
        cen = out - mu
        var = jnp.mean(cen * cen, axis=-1, keepdims=True)
        o = cen * lax.rsqrt(var + EPS) * gg + gb
        gt = g_ref[rows, :].astype(F32)
        o_ref[rows, :] = (gt * jax.nn.sigmoid(gt) * o).astype(o_ref.dtype)
        return carry

    lax.fori_loop(0, q_ref.shape[0] // chunk, body, 0)


def _retention(z4, gn_g, gn_b, tb=2048):
    t = z4.shape[0]
    w = z4.shape[1] // 4
    hd = w // RET_HEADS
    half = hd // 2
    c = RET_CHUNK
    tb = min(tb, t)
    pos = jnp.arange(t, dtype=F32)
    freqs = ROPE_BASE ** (-jnp.arange(half, dtype=F32) / half)
    ang = pos[:, None] * freqs[None, :]
    cos, sin = jnp.cos(ang), jnp.sin(ang)
    gamma = 1.0 - 2.0 ** (-5.0 - jnp.arange(RET_HEADS, dtype=F32))
    log_g = jnp.log(gamma)
    idx = jnp.arange(c, dtype=F32)
    rel = idx[:, None] - idx[None, :]
    intra = jnp.where(rel >= 0, jnp.exp(log_g[:, None, None] * jnp.maximum(rel, 0.0)), 0.0)
    q_decay = jnp.exp(log_g[:, None] * (idx + 1.0))[..., None]
    k_decay = jnp.exp(log_g[:, None] * (c - 1.0 - idx))[..., None]
    chunk_decay = jnp.exp(log_g * c)[:, None, None]
    nh = RET_HEADS
    blk = lambda off: pl.BlockSpec((tb, hd), lambda h, s: (s, off + h))
    per_head = lambda shape: pl.BlockSpec((1,) + shape, lambda h, s: (h, 0, 0))
    return pl.pallas_call(
        functools.partial(_ret_kernel, chunk=c, scale=hd ** -0.5),
        grid=(nh, t // tb),
        in_specs=[
            blk(0), blk(nh), blk(2 * nh), blk(3 * nh),
            pl.BlockSpec((tb, half), lambda h, s: (s, 0)),
            pl.BlockSpec((tb, half), lambda h, s: (s, 0)),
            per_head((c, c)), per_head((c, 1)), per_head((c, 1)), per_head((1, 1)),
            pl.BlockSpec((1, hd), lambda h, s: (0, h)),
            pl.BlockSpec((1, hd), lambda h, s: (0, h)),
        ],
        out_specs=pl.BlockSpec((tb, hd), lambda h, s: (s, h)),
        out_shape=jax.ShapeDtypeStruct((t, w), BF16),
        scratch_shapes=[pltpu.VMEM((hd, hd), F32)],
        compiler_params=_cparams("parallel", "arbitrary"),
        name="retention",
    )(z4, z4, z4, z4, cos, sin, intra, q_decay, k_decay, chunk_decay,
      gn_g.reshape(1, w).astype(F32), gn_b.reshape(1, w).astype(F32))


def _qknorm_kernel(x_ref, gain_ref, ones_ref, o_ref, *, inv_dim, transpose_out):
    x = x_ref[...].astype(F32)
    xx = x * x
    hi = xx.astype(BF16)
    lo = (xx - hi.astype(F32)).astype(BF16)
    ss = _dot(hi, ones_ref[...]) + _dot(lo, ones_ref[...])
    y = x * lax.rsqrt(ss * inv_dim + EPS) * gain_ref[...]
    o_ref[...] = (y.T if transpose_out else y).astype(o_ref.dtype)


def _qknorm(z, gain, col_off, transpose_out, bt=1024, bw=256):
    t = z.shape[0]
    w = gain.shape[0]
    bt = min(bt, t)
    off = col_off // bw
    grp = jnp.arange(bw) // DIFF_HEAD_DIM
    ones = (grp[:, None] == grp[None, :]).astype(BF16)
    if transpose_out:
        out_spec = pl.BlockSpec((bw, bt), lambda i, j: (j, i))
        out_shape = jax.ShapeDtypeStruct((w, t), BF16)
    else:
        out_spec = pl.BlockSpec((bt, bw), lambda i, j: (i, j))
        out_shape = jax.ShapeDtypeStruct((t, w), BF16)
    return pl.pallas_call(
        functools.partial(_qknorm_kernel, inv_dim=1.0 / DIFF_HEAD_DIM, transpose_out=transpose_out),
        grid=(t // bt, w // bw),
        in_specs=[
            pl.BlockSpec((bt, bw), lambda i, j: (i, off + j)),
            pl.BlockSpec((1, bw), lambda i, j: (0, j)),
            pl.BlockSpec((bw, bw), lambda i, j: (0, 0)),
        ],
        out_specs=out_spec,
        out_shape=out_shape,
        compiler_params=_cparams("parallel", "parallel"),
        name="q_norm_t" if transpose_out else "k_norm",
    )(z, gain.reshape(1, w).astype(F32), ones)


def _attn_kernel(qt_ref, k_ref, vt_ref, d0_ref, d1_ref, lam_ref, sg_ref, o_ref, m_ref, acc_ref, va_ref,
                 *, blk, out_scale):
    i = pl.program_id(1)
    hd = DIFF_HEAD_DIM
    dv = 2 * hd

    @pl.when(i == 0)
    def _():
        for jb in range(va_ref.shape[0]):
            va_ref[jb, 0:dv, :] = vt_ref[:, jb * blk:(jb + 1) * blk]
            va_ref[jb, dv:, :] = jnp.ones((va_ref.shape[1] - dv, blk), va_ref.dtype)

    m_ref[...] = jnp.full(m_ref.shape, -1e30, F32)
    acc_ref[...] = jnp.zeros_like(acc_ref)

    def step(j, bias):
        rows = pl.ds(pl.multiple_of(j * blk, blk), blk)
        va = va_ref[j]
        for mi in range(2):
            s = _dot(k_ref[rows, mi * hd:(mi + 1) * hd], qt_ref[mi * hd:(mi + 1) * hd, :])
            if bias is not None:
                s = s + bias
            m_old = m_ref[mi]
            m_new = jnp.maximum(m_old, jnp.max(s, axis=0, keepdims=True))
            p = jnp.exp2(s - m_new)
            alpha = jnp.exp2(m_old - m_new)
            acc_ref[mi] = alpha * acc_ref[mi] + _dot(va, p.astype(BF16))
            m_ref[mi] = m_new

    def far(j, carry):
        step(j, None)
        return carry

    lax.fori_loop(0, jnp.maximum(i - 1, 0), far, 0)

    @pl.when(i >= 1)
    def _():
        step(i - 1, d1_ref[0])

    step(i, d0_ref[0])
    a0, a1 = acc_ref[0], acc_ref[1]
    o_t = a0[:dv] / a0[dv:dv + 1] - lam_ref[...] * (a1[:dv] / a1[dv:dv + 1])
    o = o_t.T
    ms = jnp.mean(o * o, axis=-1, keepdims=True)
    o_ref[...] = (o * lax.rsqrt(ms + EPS) * sg_ref[...] * out_scale).astype(o_ref.dtype)


def _t5_bucket(n):
    max_exact = REL_BUCKETS // 2
    nf = jnp.maximum(n, 1).astype(F32)
    large = max_exact + (jnp.log(nf / max_exact) / math.log(REL_MAX_DIST / max_exact)
                         * (REL_BUCKETS - max_exact)).astype(jnp.int32)
    large = jnp.minimum(large, REL_BUCKETS - 1)
    return jnp.where(n < max_exact, n, large)


def _diff_attention(q_t, kn, v_t, rel_bias, lam, sub_g, lambda_init):
    t = kn.shape[0]
    nh = DIFF_HEADS
    dv = 2 * DIFF_HEAD_DIM
    ones_rows = 16
    blk = min(ATTN_BLOCK, t)
    assert blk >= REL_MAX_DIST
    key = jnp.arange(blk)[:, None]
    qry = jnp.arange(blk)[None, :]
    far_bias = rel_bias[REL_BUCKETS - 1]
    rel0 = qry - key
    d0 = rel_bias[_t5_bucket(jnp.maximum(rel0, 0))] - far_bias
    d0 = jnp.where((rel0 >= 0)[..., None], d0 * LOG2E, -jnp.inf).transpose(2, 0, 1)
    d1 = ((rel_bias[_t5_bucket(blk + qry - key)] - far_bias) * LOG2E).transpose(2, 0, 1)
    return pl.pallas_call(
        functools.partial(_attn_kernel, blk=blk, out_scale=1.0 - lambda_init),
        grid=(nh, t // blk),
        in_specs=[
            pl.BlockSpec((dv, blk), lambda h, i: (h, i)),
            pl.BlockSpec((t, dv), lambda h, i: (0, h)),
            pl.BlockSpec((dv, t), lambda h, i: (h, 0)),
            pl.BlockSpec((1, blk, blk), lambda h, i: (h, 0, 0)),
            pl.BlockSpec((1, blk, blk), lambda h, i: (h, 0, 0)),
            pl.BlockSpec((1, 1), lambda h, i: (0, 0)),
            pl.BlockSpec((1, dv), lambda h, i: (0, 0)),
        ],
        out_specs=pl.BlockSpec((blk, dv), lambda h, i: (i, h)),
        out_shape=jax.ShapeDtypeStruct((t, nh * dv), BF16),
        scratch_shapes=[pltpu.VMEM((2, 1, blk), F32), pltpu.VMEM((2, dv + ones_rows, blk), F32),
                        pltpu.VMEM((t // blk, dv + ones_rows, blk), BF16)],
        compiler_params=_cparams("parallel", "arbitrary"),
        name="diff_attn",
    )(q_t, kn, v_t, d0.astype(F32), d1.astype(F32), jnp.reshape(lam, (1, 1)).astype(F32),
      sub_g.reshape(1, dv).astype(F32))


def _pscore_kernel(w_ref, h_ref, sk_ref, o_ref):
    q_t = _dot_nt(w_ref[...], h_ref[...])
    for b in range(sk_ref.shape[0]):
        rows = slice(b * LANES, (b + 1) * LANES)
        o_ref[rows, :] = _dot(sk_ref[b], q_t[rows, :].astype(BF16))


def _peer_scores(hq, wq_t, sub_keys, tm=512, bn=512):
    t, d = hq.shape
    n = wq_t.shape[0]
    tm = min(tm, t)
    nb = bn // LANES
    return pl.pallas_call(
        _pscore_kernel,
        grid=(t // tm, n // bn),
        in_specs=[
            pl.BlockSpec((bn, d), lambda i, j: (j, 0)),
            pl.BlockSpec((tm, d), lambda i, j: (i, 0)),
            pl.BlockSpec((nb, LANES, LANES), lambda i, j: (j, 0, 0)),
        ],
        out_specs=pl.BlockSpec((bn, tm), lambda i, j: (j, i)),
        out_shape=jax.ShapeDtypeStruct((n, t), F32),
        compiler_params=_cparams("parallel", "parallel"),
        name="peer_scores",
    )(wq_t, hq, sub_keys)


def _top_values(s, n):
    vals = []
    for r in range(n):
        m = jnp.max(s, axis=0, keepdims=True)
        vals.append(m)
        if r < n - 1:
            s = jnp.where(s == m, -jnp.inf, s)
    return vals


def _route_kernel(s_ref, e1_ref, e2_ref, c_ref):
    nk = PEER_N_KEYS
    k = PEER_TOPK
    tl = s_ref.shape[1]
    pad = 24
    row = lax.broadcasted_iota(jnp.int32, (pad, tl), 0)
    hrow = lax.broadcasted_iota(jnp.int32, (PEER_HEADS, tl), 0)
    c_all = jnp.zeros((PEER_HEADS, tl), F32)
    for h in range(PEER_HEADS):
        s1 = s_ref[2 * h * nk:(2 * h + 1) * nk, :]
        s2 = s_ref[(2 * h + 1) * nk:(2 * h + 2) * nk, :]
        a = _top_values(s1, k + 1)
        b = _top_values(s2, k + 1)
        ea = [jnp.exp(x - a[0]) for x in a]
        eb = [jnp.exp(x - b[0]) for x in b]
        ea_m = jnp.full((pad, tl), -1.0, F32)
        eb_m = jnp.full((pad, tl), -1.0, F32)
        for r in range(k + 1):
            ea_m = jnp.where(row == r, ea[r], ea_m)
            eb_m = jnp.where(row == r, eb[r], eb_m)
        cand = jnp.concatenate(
            [ea[0] * eb_m, ea[1] * eb_m[:8], ea[2] * eb_m[:8], ea[3] * eb_m[:8],
             eb[0] * ea_m, eb[1] * ea_m[:8], eb[2] * ea_m[:8]], axis=0)
        v = _top_values(cand, k + 1)
        z = v[0]
        for r in range(1, k):
            z = z + v[r]
        rz = 1.0 / z
        e1_ref[h] = jnp.exp(s1 - a[0])
        e2_ref[h] = jnp.exp(s2 - b[0]) * rz
        c_all = jnp.where(hrow == h, 0.5 * (v[k - 1] + v[k]) * rz, c_all)
    c_ref[...] = c_all


def _peer_route(s_t, tl=256):
    n, t = s_t.shape
    tl = min(tl, t)
    nh, nk = PEER_HEADS, PEER_N_KEYS
    return pl.pallas_call(
        _route_kernel,
        grid=(t // tl,),
        in_specs=[pl.BlockSpec((n, tl), lambda i: (0, i))],
        out_specs=[
            pl.BlockSpec((nh, nk, tl), lambda i: (0, 0, i)),
            pl.BlockSpec((nh, nk, tl), lambda i: (0, 0, i)),
            pl.BlockSpec((nh, tl), lambda i: (0, i)),
        ],
        out_shape=[
            jax.ShapeDtypeStruct((nh, nk, t), F32),
            jax.ShapeDtypeStruct((nh, nk, t), F32),
            jax.ShapeDtypeStruct((nh, t), F32),
        ],
        compiler_params=_cparams("parallel"),
        name="peer_route",
    )(s_t)


def _peer_dense_kernel(u_ref, v_ref, h_ref, x_ref, e1_ref, e2_ref, c_ref, o_ref, w_ref):
    @pl.when(pl.program_id(1) == 0)
    def _():
        o_ref[...] = x_ref[...]

    a_t = _dot_nt(u_ref[...], h_ref[...])
    tm = a_t.shape[1]
    for ii in range(e1_ref.shape[0]):
        rows = slice(ii * LANES, (ii + 1) * LANES)
        g = jnp.zeros((LANES, tm), F32)
        for h in range(PEER_HEADS):
            vv = e2_ref[h] * e1_ref[ii, h:h + 1, :]
            g = g + jnp.where(vv >= c_ref[h:h + 1, :], vv, 0.0)
        w_ref[rows, :] = (g * _gelu(a_t[rows, :])).astype(w_ref.dtype)
    o_ref[...] += _dot_tn(w_ref[...], v_ref[...])


def _peer_dense(x, hq, u_tab, v_tab, e1_t, e2, c, tm=512, te=512):
    t, d = x.shape
    ne = u_tab.shape[0]
    tm = min(tm, t)
    nh, nk = PEER_HEADS, PEER_N_KEYS
    once = pl.Buffered(1)
    return pl.pallas_call(
        _peer_dense_kernel,
        grid=(t // tm, ne // te),
        in_specs=[
            pl.BlockSpec((te, d), lambda i, e: (e, 0)),
            pl.BlockSpec((te, d), lambda i, e: (e, 0)),
            pl.BlockSpec((tm, d), lambda i, e: (i, 0), pipeline_mode=once),
            pl.BlockSpec((tm, d), lambda i, e: (i, 0), pipeline_mode=once),
            pl.BlockSpec((te // nk, nh, tm), lambda i, e: (e, 0, i)),
            pl.BlockSpec((nh, nk, tm), lambda i, e: (0, 0, i), pipeline_mode=once),
            pl.BlockSpec((nh, tm), lambda i, e: (0, i)),
        ],
        out_specs=pl.BlockSpec((tm, d), lambda i, e: (i, 0)),
        out_shape=jax.ShapeDtypeStruct((t, d), F32),
        scratch_shapes=[pltpu.VMEM((te, tm), BF16)],
        compiler_params=_cparams("parallel", "arbitrary"),
        name="peer_dense",
    )(u_tab, v_tab, hq, x, e1_t, e2, c)


def _peer(x, norm_g, w_q, sub_keys, u_tab, v_tab):
    hq = _rmsnorm(x, norm_g)
    nh, nk = PEER_HEADS, PEER_N_KEYS
    s_t = _peer_scores(hq, w_q.T.astype(BF16), sub_keys.reshape(2 * nh, nk, -1).astype(BF16))
    e1, e2, c = _peer_route(s_t)
    return _peer_dense(x, hq, u_tab.astype(BF16), v_tab.astype(BF16), e1.transpose(1, 0, 2), e2, c)


def _even_layer(x, norm_g, w_in, log_step, a_re, a_im, b_re, b_im, c_re, c_im, d_skip,
                glu_w, glu_b, gn_g, gn_b, w_out):
    s5w = d_skip.shape[0]
    h = _rmsnorm(x, norm_g)
    u_t = _matmul([h], [w_in[:, :s5w].astype(BF16)], tile_major=True, name="ev_in_s5")
    z4 = _matmul([h], [w_in[:, s5w:].astype(BF16)], name="ev_in_ret")
    tables = _s5_tables(log_step, a_re, a_im, b_re, b_im, c_re, c_im, d_skip)
    ya = _glu(_s5_core(u_t, tables), glu_w.astype(BF16), glu_b)
    yb = _retention(z4, gn_g, gn_b)
    w_out = w_out.astype(BF16)
    return _matmul([ya, yb], [w_out[:s5w], w_out[s5w:]], resid=x, out_dtype=F32, name="ev_out")


def _odd_layer(x, norm_g, w_in, q_norm_g, k_norm_g, lq1, lk1, lq2, lk2, sub_norm_g, w_out,
               rel_bias, lambda_init):
    d = x.shape[1]
    h = _rmsnorm(x, norm_g)
    z = _matmul([h], [w_in[:, :2 * d].astype(BF16)], name="od_in_qk")
    v_t = _matmul_nt(w_in[:, 2 * d:].T.astype(BF16), h, name="od_in_vt")
    reps = d // DIFF_HEAD_DIM
    q_t = _qknorm(z, jnp.tile(q_norm_g, reps) * (DIFF_HEAD_DIM ** -0.5 * LOG2E), 0, True)
    kn = _qknorm(z, jnp.tile(k_norm_g, reps), d, False)
    lam = jnp.exp(jnp.sum(lq1 * lk1)) - jnp.exp(jnp.sum(lq2 * lk2)) + lambda_init
    o = _diff_attention(q_t, kn, v_t, rel_bias, lam, sub_norm_g, lambda_init)
    return _matmul([o], [w_out.astype(BF16)], resid=x, out_dtype=F32, name="od_out")


def kernel(x, ev_norm_g, ev_w_in, s5_log_step, s5_a_re, s5_a_im, s5_b_re, s5_b_im, s5_c_re, s5_c_im,
           s5_d, s5_glu_w, s5_glu_b, ret_gn_g, ret_gn_b, ev_w_out, od_norm_g, od_w_in, diff_q_norm_g,
           diff_k_norm_g, diff_lq1, diff_lk1, diff_lq2, diff_lk2, diff_sub_norm_g, od_w_out, rel_bias,
           ffn_norm_g, peer_w_q, peer_sub_keys, peer_u, peer_v):
    bsz, t, d = x.shape
    depth = ffn_norm_g.shape[0]
    outs = []
    for b in range(bsz):
        xb = x[b]
        for layer in range(depth):
            i = layer // 2
            if layer % 2 == 0:
                xb = _even_layer(xb, ev_norm_g[i], ev_w_in[i], s5_log_step[i], s5_a_re[i], s5_a_im[i],
                                 s5_b_re[i], s5_b_im[i], s5_c_re[i], s5_c_im[i], s5_d[i], s5_glu_w[i],
                                 s5_glu_b[i], ret_gn_g[i], ret_gn_b[i], ev_w_out[i])
            else:
                lambda_init = 0.8 - 0.6 * math.exp(-0.3 * layer)
                xb = _odd_layer(xb, od_norm_g[i], od_w_in[i], diff_q_norm_g[i], diff_k_norm_g[i],
                                diff_lq1[i], diff_lk1[i], diff_lq2[i], diff_lk2[i], diff_sub_norm_g[i],
                                od_w_out[i], rel_bias, lambda_init)
            xb = _peer(xb, ffn_norm_g[layer], peer_w_q[layer], peer_sub_keys[layer], peer_u[layer],
                       peer_v[layer])
        outs.append(xb)
    return jnp.stack(outs, 0)
```

```python
import functools
import math

import jax
import jax.numpy as jnp
from jax import lax
from jax.experimental import pallas as pl
from jax.experimental.pallas import tpu as pltpu

F32 = jnp.float32
BF16 = jnp.bfloat16

EPS = 1e-6
LOG2E = 1.4426950408889634
LANES = 128
VMEM_LIMIT_BYTES = 56 * 1024 * 1024

S5_GROUP = 16
S5_STATE = 64
S5_CHUNK = 16
RET_HEADS = 8
RET_CHUNK = 128
ROPE_BASE = 10000.0
DIFF_HEADS = 16
DIFF_HEAD_DIM = 128
ATTN_BLOCK = 512
ATTN_HEADS_PER_STEP = 2
REL_BUCKETS = 32
REL_MAX_DIST = 128
PEER_HEADS = 8
PEER_N_KEYS = 128
PEER_TOPK = 16


def _cparams(*sem):
    return pltpu.CompilerParams(dimension_semantics=sem, vmem_limit_bytes=VMEM_LIMIT_BYTES)


def _dot(a, b):
    return jnp.dot(a, b, preferred_element_type=F32)


def _dot_nt(a, b):
    return lax.dot_general(a, b, (((1,), (1,)), ((), ())), preferred_element_type=F32)


def _dot_tn(a, b):
    return lax.dot_general(a, b, (((0,), (0,)), ((), ())), preferred_element_type=F32)


def _gelu(x):
    return 0.5 * x * (1.0 + jnp.tanh(0.7978845608028654 * (x + 0.044715 * (x * x * x))))


def _norm_kernel(x_ref, g_ref, o_ref):
    x = x_ref[...]
    r = lax.rsqrt(jnp.mean(x * x, axis=-1, keepdims=True) + EPS)
    o_ref[...] = (x * r * g_ref[...]).astype(o_ref.dtype)


def _rmsnorm(x, g, bt=256):
    t, d = x.shape
    return pl.pallas_call(
        _norm_kernel,
        grid=(t // bt,),
        in_specs=[pl.BlockSpec((bt, d), lambda i: (i, 0)), pl.BlockSpec((1, d), lambda i: (0, 0))],
        out_specs=pl.BlockSpec((bt, d), lambda i: (i, 0)),
        out_shape=jax.ShapeDtypeStruct((t, d), BF16),
        compiler_params=_cparams("parallel"),
        name="rmsnorm",
    )(x, g.reshape(1, d).astype(F32))


def _cast_kernel(w_ref, o_ref):
    o_ref[...] = w_ref[0].astype(o_ref.dtype)


def _cast_bf16(w, layer, col0=0, ncols=None, br=512, bc=4096):
    _, r, c = w.shape
    ncols = c - col0 if ncols is None else ncols
    bc = math.gcd(math.gcd(bc, ncols), col0) if col0 else math.gcd(bc, ncols)
    br = min(br, r)
    off = col0 // bc
    return pl.pallas_call(
        _cast_kernel,
        grid=(r // br, ncols // bc),
        in_specs=[pl.BlockSpec((1, br, bc), lambda i, j: (layer, i, off + j))],
        out_specs=pl.BlockSpec((br, bc), lambda i, j: (i, j)),
        out_shape=jax.ShapeDtypeStruct((r, ncols), BF16),
        compiler_params=_cparams("parallel", "parallel"),
        name="cast_bf16",
    )(w)


def _mm_kernel(*refs, n_lhs, has_resid, tile_major):
    a_refs, b_refs = refs[:n_lhs], refs[n_lhs:2 * n_lhs]
    r_ref = refs[2 * n_lhs] if has_resid else None
    o_ref = refs[-1]
    acc = _dot(a_refs[0][...], b_refs[0][...])
    for a_ref, b_ref in zip(a_refs[1:], b_refs[1:]):
        acc = acc + _dot(a_ref[...], b_ref[...])
    if has_resid:
        acc = acc + r_ref[...]
    if tile_major:
        for jj in range(o_ref.shape[0]):
            o_ref[jj] = acc[:, jj * LANES:(jj + 1) * LANES].astype(o_ref.dtype)
    else:
        o_ref[...] = acc.astype(o_ref.dtype)


def _matmul(lhs, rhs, *, resid=None, out_dtype=BF16, bm=512, bn=1024, tile_major=False, name="mm"):
    m, n = lhs[0].shape[0], rhs[0].shape[1]
    bm, bn = min(bm, m), min(bn, n)
    in_specs = [pl.BlockSpec((bm, a.shape[1]), lambda i, j: (i, 0)) for a in lhs]
    in_specs += [pl.BlockSpec((b.shape[0], bn), lambda i, j: (0, j)) for b in rhs]
    args = list(lhs) + list(rhs)
    if resid is not None:
        in_specs.append(pl.BlockSpec((bm, bn), lambda i, j: (i, j)))
        args.append(resid)
    if tile_major:
        out_spec = pl.BlockSpec((bn // LANES, bm, LANES), lambda i, j: (j, i, 0))
        out_shape = jax.ShapeDtypeStruct((n // LANES, m, LANES), out_dtype)
    else:
        out_spec = pl.BlockSpec((bm, bn), lambda i, j: (i, j))
        out_shape = jax.ShapeDtypeStruct((m, n), out_dtype)
    kern = functools.partial(_mm_kernel, n_lhs=len(lhs), has_resid=resid is not None, tile_major=tile_major)
    return pl.pallas_call(
        kern, grid=(m // bm, n // bn), in_specs=in_specs, out_specs=out_spec, out_shape=out_shape,
        compiler_params=_cparams("parallel", "parallel"), name=name,
    )(*args)


def _mm_nt_kernel(w_ref, h_ref, o_ref):
    o_ref[...] = _dot_nt(w_ref[...], h_ref[...]).astype(o_ref.dtype)


def _matmul_nt(w_t, h, *, bm=512, bn=1024, out_dtype=BF16, name="mm_nt"):
    n, k = w_t.shape
    m = h.shape[0]
    bm, bn = min(bm, m), min(bn, n)
    return pl.pallas_call(
        _mm_nt_kernel,
        grid=(m // bm, n // bn),
        in_specs=[pl.BlockSpec((bn, k), lambda i, j: (j, 0)), pl.BlockSpec((bm, k), lambda i, j: (i, 0))],
        out_specs=pl.BlockSpec((bn, bm), lambda i, j: (j, i)),
        out_shape=jax.ShapeDtypeStruct((n, m), out_dtype),
        compiler_params=_cparams("parallel", "parallel"),
        name=name,
    )(w_t, h)


def _s5_tables(log_step, a_re, a_im, b_re, b_im, c_re, c_im, d_skip):
    L = S5_CHUNK
    g, p = a_re.shape
    gt = LANES // S5_GROUP
    nt = g // gt
    hp = lax.Precision.HIGHEST
    step = jnp.exp(log_step)[:, None]
    lr, li = a_re, a_im
    mag = jnp.exp(lr * step)
    abar_re = mag * jnp.cos(li * step)
    abar_im = mag * jnp.sin(li * step)
    den = lr * lr + li * li
    num_re = abar_re - 1.0
    f_re = (num_re * lr + abar_im * li) / den
    f_im = (abar_im * lr - num_re * li) / den
    bb_re = f_re[..., None] * b_re - f_im[..., None] * b_im
    bb_im = f_re[..., None] * b_im + f_im[..., None] * b_re
    k = jnp.arange(L + 1, dtype=F32)[:, None, None]
    pmag = jnp.exp(k * (lr * step)[None])
    pang = k * (li * step)[None]
    pw_re = pmag * jnp.cos(pang)
    pw_im = pmag * jnp.sin(pang)
    ns = 2 * gt * p

    def same_group(row_g, col_g):
        return (row_g[:, None] == col_g[None, :]).astype(F32)

    cp_re = c_re[None] * pw_re[:L, :, None, :] - c_im[None] * pw_im[:L, :, None, :]
    cp_im = c_re[None] * pw_im[:L, :, None, :] + c_im[None] * pw_re[:L, :, None, :]
    kk = (jnp.einsum("dgop,gpi->dgoi", cp_re, bb_re, precision=hp)
          - jnp.einsum("dgop,gpi->dgoi", cp_im, bb_im, precision=hp))
    kki = kk.transpose(0, 1, 3, 2).reshape(L, nt, LANES, S5_GROUP)
    lane_g = jnp.arange(LANES) // S5_GROUP
    bd = jnp.tile(kki, (1, 1, 1, gt)) * same_group(lane_g, lane_g)
    bd = bd.transpose(1, 0, 2, 3).astype(BF16)

    k_rev = (L - 1) - k[:L]
    pmag_rev = jnp.exp(k_rev * (lr * step)[None])
    pang_rev = k_rev * (li * step)[None]
    pwr, pwi = pmag_rev * jnp.cos(pang_rev), pmag_rev * jnp.sin(pang_rev)
    pb_re = pwr[..., None] * bb_re[None] - pwi[..., None] * bb_im[None]
    pb_im = pwr[..., None] * bb_im[None] + pwi[..., None] * bb_re[None]
    pb = jnp.stack([pb_re, pb_im], 0).reshape(2, L, nt, gt, p, S5_GROUP)
    ps = pb.transpose(2, 1, 3, 5, 0, 4).reshape(nt, L * LANES, 2 * p).astype(BF16)
    rr, cc = jnp.arange(2 * p), jnp.arange(ns)
    ex_p = ((rr[:, None] // p == cc[None, :] // (gt * p)) & (rr[:, None] % p == cc[None, :] % p))
    row_g = (jnp.arange(L * LANES) % LANES) // S5_GROUP
    col_g = (cc % (gt * p)) // p
    p_op = (jnp.einsum("jrk,kc->jrc", ps, ex_p.astype(BF16), preferred_element_type=F32)
            * same_group(row_g, col_g)).astype(BF16)

    qr, qi = pw_re[1:L + 1], pw_im[1:L + 1]
    cn_re = c_re[None] * qr[:, :, None, :] - c_im[None] * qi[:, :, None, :]
    cn_im = c_re[None] * qi[:, :, None, :] + c_im[None] * qr[:, :, None, :]
    nb = jnp.stack([cn_re, -cn_im], 0).reshape(2, L, nt, gt, S5_GROUP, p)
    nsm = nb.transpose(2, 0, 3, 5, 1, 4).reshape(nt, ns, L * S5_GROUP).astype(BF16)
    rr, cc = jnp.arange(L * S5_GROUP), jnp.arange(L * LANES)
    ex_n = ((rr[:, None] // S5_GROUP == cc[None, :] // LANES)
            & (rr[:, None] % S5_GROUP == cc[None, :] % S5_GROUP))
    row_g = (jnp.arange(ns) % (gt * p)) // p
    col_g = (cc % LANES) // S5_GROUP
    n_op = (jnp.einsum("jrk,kc->jrc", nsm, ex_n.astype(BF16), preferred_element_type=F32)
            * same_group(row_g, col_g)).astype(BF16)

    a_l = jnp.stack([pw_re[L], pw_im[L]], 0).reshape(2, nt, gt * p).transpose(1, 0, 2)
    d_t = jnp.tile(d_skip.reshape(nt, 1, LANES), (1, 1, L))
    return bd, p_op, n_op, a_l, d_t


def _s5_kernel(x_ref, bd_ref, p_ref, n_ref, al_ref, d_ref, o_ref, m_ref, sloc_ref, sprev_ref):
    x = x_ref[0]
    nc, ns = sloc_ref.shape
    half = ns // 2
    L = bd_ref.shape[1]
    for tau in range(L):
        for t in range(L):
            blk = bd_ref[0, t - tau] if t >= tau else jnp.zeros((LANES, LANES), m_ref.dtype)
            m_ref[tau * LANES:(tau + 1) * LANES, t * LANES:(t + 1) * LANES] = blk
    sloc_ref[...] = _dot(x, p_ref[0])
    a_re = al_ref[0, 0:1, :]
    a_im = al_ref[0, 1:2, :]

    def body(c, carry):
        s_re, s_im = carry
        row = pl.ds(c, 1)
        sprev_ref[row, 0:half] = s_re
        sprev_ref[row, half:ns] = s_im
        l_re = sloc_ref[row, 0:half]
        l_im = sloc_ref[row, half:ns]
        return (a_re * s_re - a_im * s_im + l_re, a_re * s_im + a_im * s_re + l_im)

    zero = jnp.zeros((1, half), F32)
    lax.fori_loop(0, nc, body, (zero, zero))
    y = _dot(x, m_ref[...]) + _dot(sprev_ref[...].astype(BF16), n_ref[0])
    y = y + d_ref[0] * x.astype(F32)
    o_ref[0] = _gelu(y).astype(o_ref.dtype)


def _s5_core(u_t, tables):
    bd, p_op, n_op, a_l, d_t = tables
    nt, t, _ = u_t.shape
    L = S5_CHUNK
    nc, w, ns = t // L, L * LANES, p_op.shape[2]
    x = u_t.reshape(nt, nc, w)
    out = pl.pallas_call(
        _s5_kernel,
        grid=(nt,),
        in_specs=[
            pl.BlockSpec((1, nc, w), lambda j: (j, 0, 0)),
            pl.BlockSpec((1, L, LANES, LANES), lambda j: (j, 0, 0, 0)),
            pl.BlockSpec((1, w, ns), lambda j: (j, 0, 0)),
            pl.BlockSpec((1, ns, w), lambda j: (j, 0, 0)),
            pl.BlockSpec((1, 2, ns // 2), lambda j: (j, 0, 0)),
            pl.BlockSpec((1, 1, w), lambda j: (j, 0, 0)),
        ],
        out_specs=pl.BlockSpec((1, nc, w), lambda j: (j, 0, 0)),
        out_shape=jax.ShapeDtypeStruct((nt, nc, w), BF16),
        scratch_shapes=[pltpu.VMEM((w, w), BF16), pltpu.VMEM((nc, ns), F32), pltpu.VMEM((nc, ns), F32)],
        compiler_params=_cparams("parallel"),
        name="s5_core",
    )(x, bd, p_op, n_op, a_l, d_t)
    return out.reshape(nt, t, LANES)


def _glu_kernel(y_ref, w_ref, b_ref, o_ref):
    y = jnp.concatenate([y_ref[j] for j in range(y_ref.shape[0])], axis=1)
    acc = _dot(y, w_ref[...]) + b_ref[...]
    o_ref[...] = (y.astype(F32) * jax.nn.sigmoid(acc)).astype(o_ref.dtype)


def _glu(y_t, w, b, bm=512):
    nt, t, _ = y_t.shape
    width = nt * LANES
    return pl.pallas_call(
        _glu_kernel,
        grid=(t // bm,),
        in_specs=[
            pl.BlockSpec((nt, bm, LANES), lambda i: (0, i, 0)),
            pl.BlockSpec((width, width), lambda i: (0, 0)),
            pl.BlockSpec((1, width), lambda i: (0, 0)),
        ],
        out_specs=pl.BlockSpec((bm, width), lambda i: (i, 0)),
        out_shape=jax.ShapeDtypeStruct((t, width), BF16),
        compiler_params=_cparams("parallel"),
        name="s5_glu",
    )(y_t, w, b.reshape(1, width).astype(F32))


def _ret_kernel(q_ref, k_ref, v_ref, g_ref, cos_ref, sin_ref, intra_ref, qd_ref, kd_ref, cd_ref,
                gg_ref, gb_ref, o_ref, state_ref, *, chunk, scale):
    @pl.when(pl.program_id(1) == 0)
    def _():
        state_ref[...] = jnp.zeros_like(state_ref)

    intra, qd, kd, cd = intra_ref[0], qd_ref[0], kd_ref[0], cd_ref[0]
    gg, gb = gg_ref[...], gb_ref[...]
    half = cos_ref.shape[1]

    def body(c, carry):
        rows = pl.ds(pl.multiple_of(c * chunk, chunk), chunk)
        cos, sin = cos_ref[rows, :], sin_ref[rows, :]

        def rot(x):
            x1, x2 = x[:, :half], x[:, half:]
            return jnp.concatenate([x1 * cos - x2 * sin, x1 * sin + x2 * cos], axis=1)

        q = rot(q_ref[rows, :].astype(F32))
        k = rot(k_ref[rows, :].astype(F32)) * scale
        v = v_ref[rows, :]
        st = state_ref[...]
        scores = _dot_nt(q.astype(BF16), k.astype(BF16)) * intra
        out = _dot(scores.astype(BF16), v) + _dot((q * qd).astype(BF16), st.astype(BF16))
        state_ref[...] = cd * st + _dot_tn((k * kd).astype(BF16), v)
        mu = jnp.mean(out, axis=-1, keepdims=True)
        cen = out - mu
        var = jnp.mean(cen * cen, axis=-1, keepdims=True)
        o = cen * lax.rsqrt(var + EPS) * gg + gb
        gt = g_ref[rows, :].astype(F32)
        o_ref[rows, :] = (gt * jax.nn.sigmoid(gt) * o).astype(o_ref.dtype)
        return carry

    lax.fori_loop(0, q_ref.shape[0] // chunk, body, 0)


def _retention(z4, gn_g, gn_b, tb=2048):
    t = z4.shape[0]
    w = z4.shape[1] // 4
    hd = w // RET_HEADS
    half = hd // 2
    c = RET_CHUNK
    tb = min(tb, t)
    pos = jnp.arange(t, dtype=F32)
    freqs = ROPE_BASE ** (-jnp.arange(half, dtype=F32) / half)
    ang = pos[:, None] * freqs[None, :]
    cos, sin = jnp.cos(ang), jnp.sin(ang)
    gamma = 1.0 - 2.0 ** (-5.0 - jnp.arange(RET_HEADS, dtype=F32))
    log_g = jnp.log(gamma)
    idx = jnp.arange(c, dtype=F32)
    rel = idx[:, None] - idx[None, :]
    intra = jnp.where(rel >= 0, jnp.exp(log_g[:, None, None] * jnp.maximum(rel, 0.0)), 0.0)
    q_decay = jnp.exp(log_g[:, None] * (idx + 1.0))[..., None]
    k_decay = jnp.exp(log_g[:, None] * (c - 1.0 - idx))[..., None]
    chunk_decay = jnp.exp(log_g * c)[:, None, None]
    nh = RET_HEADS
    blk = lambda off: pl.BlockSpec((tb, hd), lambda h, s: (s, off + h))
    per_head = lambda shape: pl.BlockSpec((1,) + shape, lambda h, s: (h, 0, 0))
    return pl.pallas_call(
        functools.partial(_ret_kernel, chunk=c, scale=hd ** -0.5),
        grid=(nh, t // tb),
        in_specs=[
            blk(0), blk(nh), blk(2 * nh), blk(3 * nh),
            pl.BlockSpec((tb, half), lambda h, s: (s, 0)),
            pl.BlockSpec((tb, half), lambda h, s: (s, 0)),
            per_head((c, c)), per_head((c, 1)), per_head((c, 1)), per_head((1, 1)),
            pl.BlockSpec((1, hd), lambda h, s: (0, h)),
            pl.BlockSpec((1, hd), lambda h, s: (0, h)),
        ],
        out_specs=pl.BlockSpec((tb, hd), lambda h, s: (s, h)),
        out_shape=jax.ShapeDtypeStruct((t, w), BF16),
        scratch_shapes=[pltpu.VMEM((hd, hd), F32)],
        compiler_params=_cparams("parallel", "arbitrary"),
        name="retention",
    )(z4, z4, z4, z4, cos, sin, intra, q_decay, k_decay, chunk_decay,
      gn_g.reshape(1, w).astype(F32), gn_b.reshape(1, w).astype(F32))


def _qknorm_kernel(x_ref, gain_ref, ones_ref, o_ref, *, inv_dim, transpose_out):
    x = x_ref[...].astype(F32)
    xx = x * x
    hi = xx.astype(BF16)
    lo = (xx - hi.astype(F32)).astype(BF16)
    ss = _dot(hi, ones_ref[...]) + _dot(lo, ones_ref[...])
    y = x * lax.rsqrt(ss * inv_dim + EPS) * gain_ref[...]
    o_ref[...] = (y.T if transpose_out else y).astype(o_ref.dtype)


def _qknorm(z, gain, col_off, transpose_out, bt=1024, bw=256):
    t = z.shape[0]
    w = gain.shape[0]
    bt = min(bt, t)
    off = col_off // bw
    grp = jnp.arange(bw) // DIFF_HEAD_DIM
    ones = (grp[:, None] == grp[None, :]).astype(BF16)
    if transpose_out:
        out_spec = pl.BlockSpec((bw, bt), lambda i, j: (j, i))
        out_shape = jax.ShapeDtypeStruct((w, t), BF16)
    else:
        out_spec = pl.BlockSpec((bt, bw), lambda i, j: (i, j))
        out_shape = jax.ShapeDtypeStruct((t, w), BF16)
    return pl.pallas_call(
        functools.partial(_qknorm_kernel, inv_dim=1.0 / DIFF_HEAD_DIM, transpose_out=transpose_out),
        grid=(t // bt, w // bw),
        in_specs=[
            pl.BlockSpec((bt, bw), lambda i, j: (i, off + j)),
            pl.BlockSpec((1, bw), lambda i, j: (0, j)),
            pl.BlockSpec((bw, bw), lambda i, j: (0, 0)),
        ],
        out_specs=out_spec,
        out_shape=out_shape,
        compiler_params=_cparams("parallel", "parallel"),
        name="q_norm_t" if transpose_out else "k_norm",
    )(z, gain.reshape(1, w).astype(F32), ones)


def _attn_kernel(qt_ref, k_ref, vt_ref, d0_ref, d1_ref, lam_ref, sg_ref, o_ref, m_ref, acc_ref, va_ref,
                 *, blk, out_scale):
    i = pl.program_id(1)
    hd = DIFF_HEAD_DIM
    dv = 2 * hd
    nhs = va_ref.shape[0]

    @pl.when(i == 0)
    def _():
        for hh in range(nhs):
            for jb in range(va_ref.shape[1]):
                va_ref[hh, jb, 0:dv, :] = vt_ref[hh * dv:(hh + 1) * dv, jb * blk:(jb + 1) * blk]
                va_ref[hh, jb, dv:, :] = jnp.ones((va_ref.shape[2] - dv, blk), va_ref.dtype)

    m_ref[...] = jnp.full(m_ref.shape, -1e30, F32)
    acc_ref[...] = jnp.zeros_like(acc_ref)

    def step(j, bias_ref):
        rows = pl.ds(pl.multiple_of(j * blk, blk), blk)
        for hh in range(nhs):
            va = va_ref[hh, j]
            for mi in range(2):
                c = 2 * hh + mi
                s = _dot(k_ref[rows, c * hd:(c + 1) * hd], qt_ref[c * hd:(c + 1) * hd, :])
                if bias_ref is not None:
                    s = s + bias_ref[hh]
                m_old = m_ref[c]
                m_new = jnp.maximum(m_old, jnp.max(s, axis=0, keepdims=True))
                p = jnp.exp2(s - m_new)
                alpha = jnp.exp2(m_old - m_new)
                acc_ref[c] = alpha * acc_ref[c] + _dot(va, p.astype(BF16))
                m_ref[c] = m_new

    def far(j, carry):
        step(j, None)
        return carry

    lax.fori_loop(0, jnp.maximum(i - 1, 0), far, 0)

    @pl.when(i >= 1)
    def _():
        step(i - 1, d1_ref)

    step(i, d0_ref)
    for hh in range(nhs):
        a0, a1 = acc_ref[2 * hh], acc_ref[2 * hh + 1]
        o_t = a0[:dv] / a0[dv:dv + 1] - lam_ref[...] * (a1[:dv] / a1[dv:dv + 1])
        o = o_t.T
        ms = jnp.mean(o * o, axis=-1, keepdims=True)
        o_ref[:, hh * dv:(hh + 1) * dv] = (o * lax.rsqrt(ms + EPS) * sg_ref[...] * out_scale).astype(o_ref.dtype)


def _t5_bucket(n):
    max_exact = REL_BUCKETS // 2
    nf = jnp.maximum(n, 1).astype(F32)
    large = max_exact + (jnp.log(nf / max_exact) / math.log(REL_MAX_DIST / max_exact)
                         * (REL_BUCKETS - max_exact)).astype(jnp.int32)
    large = jnp.minimum(large, REL_BUCKETS - 1)
    return jnp.where(n < max_exact, n, large)


def _diff_attention(q_t, kn, v_t, rel_bias, lam, sub_g, lambda_init):
    t = kn.shape[0]
    nh = DIFF_HEADS
    dv = 2 * DIFF_HEAD_DIM
    ones_rows = 16
    blk = min(ATTN_BLOCK, t)
    nhs = ATTN_HEADS_PER_STEP
    assert blk >= REL_MAX_DIST
    key = jnp.arange(blk)[:, None]
    qry = jnp.arange(blk)[None, :]
    far_bias = rel_bias[REL_BUCKETS - 1]

    def bias_tile(rel):
        onehot = (_t5_bucket(rel).reshape(-1)[None, :] == jnp.arange(REL_BUCKETS)[:, None]).astype(F32)
        tile = jnp.dot((rel_bias - far_bias).T * LOG2E, onehot, precision=lax.Precision.HIGHEST)
        return tile.reshape(nh, blk, blk)

    rel0 = qry - key
    d0 = jnp.where((rel0 >= 0)[None], bias_tile(jnp.maximum(rel0, 0)), -jnp.inf)
    d1 = bias_tile(blk + qry - key)
    once = pl.Buffered(1)
    return pl.pallas_call(
        functools.partial(_attn_kernel, blk=blk, out_scale=1.0 - lambda_init),
        grid=(nh // nhs, t // blk),
        in_specs=[
            pl.BlockSpec((nhs * dv, blk), lambda h, i: (h, i)),
            pl.BlockSpec((t, nhs * dv), lambda h, i: (0, h), pipeline_mode=once),
            pl.BlockSpec((nhs * dv, t), lambda h, i: (h, 0), pipeline_mode=once),
            pl.BlockSpec((nhs, blk, blk), lambda h, i: (h, 0, 0), pipeline_mode=once),
            pl.BlockSpec((nhs, blk, blk), lambda h, i: (h, 0, 0), pipeline_mode=once),
            pl.BlockSpec((1, 1), lambda h, i: (0, 0)),
            pl.BlockSpec((1, dv), lambda h, i: (0, 0)),
        ],
        out_specs=pl.BlockSpec((blk, nhs * dv), lambda h, i: (i, h)),
        out_shape=jax.ShapeDtypeStruct((t, nh * dv), BF16),
        scratch_shapes=[pltpu.VMEM((2 * nhs, 1, blk), F32), pltpu.VMEM((2 * nhs, dv + ones_rows, blk), F32),
                        pltpu.VMEM((nhs, t // blk, dv + ones_rows, blk), BF16)],
        compiler_params=_cparams("parallel", "arbitrary"),
        name="diff_attn",
    )(q_t, kn, v_t, d0, d1, jnp.reshape(lam, (1, 1)).astype(F32), sub_g.reshape(1, dv).astype(F32))


def _pscore_kernel(w_ref, h_ref, sk_ref, o_ref):
    q_t = _dot_nt(w_ref[...], h_ref[...])
    for b in range(sk_ref.shape[0]):
        rows = slice(b * LANES, (b + 1) * LANES)
        o_ref[rows, :] = _dot(sk_ref[b], q_t[rows, :].astype(BF16))


def _peer_scores(hq, wq_t, sub_keys, tm=512, bn=512):
    t, d = hq.shape
    n = wq_t.shape[0]
    tm = min(tm, t)
    nb = bn // LANES
    return pl.pallas_call(
        _pscore_kernel,
        grid=(t // tm, n // bn),
        in_specs=[
            pl.BlockSpec((bn, d), lambda i, j: (j, 0)),
            pl.BlockSpec((tm, d), lambda i, j: (i, 0)),
            pl.BlockSpec((nb, LANES, LANES), lambda i, j: (j, 0, 0)),
        ],
        out_specs=pl.BlockSpec((bn, tm), lambda i, j: (j, i)),
        out_shape=jax.ShapeDtypeStruct((n, t), F32),
        compiler_params=_cparams("parallel", "parallel"),
        name="peer_scores",
    )(wq_t, hq, sub_keys)


def _top_values(s, n):
    vals = []
    for r in range(n):
        m = jnp.max(s, axis=0, keepdims=True)
        vals.append(m)
        if r < n - 1:
            s = jnp.where(s == m, -jnp.inf, s)
    return vals


def _route_kernel(s_ref, e1_ref, e2_ref, c_ref):
    nk = PEER_N_KEYS
    k = PEER_TOPK
    tl = s_ref.shape[1]
    pad = 24
    row = lax.broadcasted_iota(jnp.int32, (pad, tl), 0)
    hrow = lax.broadcasted_iota(jnp.int32, (PEER_HEADS, tl), 0)
    c_all = jnp.zeros((PEER_HEADS, tl), F32)
    for h in range(PEER_HEADS):
        s1 = s_ref[2 * h * nk:(2 * h + 1) * nk, :]
        s2 = s_ref[(2 * h + 1) * nk:(2 * h + 2) * nk, :]
        a = _top_values(s1, k + 1)
        b = _top_values(s2, k + 1)
        ea = [jnp.exp(x - a[0]) for x in a]
        eb = [jnp.exp(x - b[0]) for x in b]
        ea_m = jnp.full((pad, tl), -1.0, F32)
        eb_m = jnp.full((pad, tl), -1.0, F32)
        for r in range(k + 1):
            ea_m = jnp.where(row == r, ea[r], ea_m)
            eb_m = jnp.where(row == r, eb[r], eb_m)
        cand = jnp.concatenate(
            [ea[0] * eb_m, ea[1] * eb_m[:8], ea[2] * eb_m[:8], ea[3] * eb_m[:8],
             eb[0] * ea_m, eb[1] * ea_m[:8], eb[2] * ea_m[:8]], axis=0)
        v = _top_values(cand, k + 1)
        z = v[0]
        for r in range(1, k):
            z = z + v[r]
        rz = 1.0 / z
        e1_ref[h] = jnp.exp(s1 - a[0])
        e2_ref[h] = jnp.exp(s2 - b[0]) * rz
        c_all = jnp.where(hrow == h, 0.5 * (v[k - 1] + v[k]) * rz, c_all)
    c_ref[...] = c_all


def _peer_route(s_t, tl=256):
    n, t = s_t.shape
    tl = min(tl, t)
    nh, nk = PEER_HEADS, PEER_N_KEYS
    return pl.pallas_call(
        _route_kernel,
        grid=(t // tl,),
        in_specs=[pl.BlockSpec((n, tl), lambda i: (0, i))],
        out_specs=[
            pl.BlockSpec((nh, nk, tl), lambda i: (0, 0, i)),
            pl.BlockSpec((nh, nk, tl), lambda i: (0, 0, i)),
            pl.BlockSpec((nh, tl), lambda i: (0, i)),
        ],
        out_shape=[
            jax.ShapeDtypeStruct((nh, nk, t), F32),
            jax.ShapeDtypeStruct((nh, nk, t), F32),
            jax.ShapeDtypeStruct((nh, t), F32),
        ],
        compiler_params=_cparams("parallel"),
        name="peer_route",
    )(s_t)


def _peer_dense_kernel(u_ref, v_ref, h_ref, x_ref, e1_ref, e2_ref, c_ref, o_ref, w_ref):
    e = pl.program_id(1)

    @pl.when(e == 0)
    def _():
        o_ref[...] = x_ref[...]
        w_ref[1] = jnp.zeros(w_ref.shape[1:], w_ref.dtype)

    cur = e % 2
    a_t = _dot_nt(u_ref[...], h_ref[...])
    o_ref[...] += _dot_tn(w_ref[1 - cur], v_ref[...])
    tm = a_t.shape[1]
    for ii in range(e1_ref.shape[0]):
        rows = slice(ii * LANES, (ii + 1) * LANES)
        g = jnp.zeros((LANES, tm), F32)
        for h in range(PEER_HEADS):
            vv = e2_ref[h] * e1_ref[ii, h:h + 1, :]
            g = g + jnp.where(vv >= c_ref[h:h + 1, :], vv, 0.0)
        w_ref[cur, rows, :] = (g * _gelu(a_t[rows, :])).astype(w_ref.dtype)


def _peer_dense(x, hq, u_tab, v_tab, e1_t, e2, c, tm=512, te=512):
    t, d = x.shape
    ne = u_tab.shape[0]
    tm = min(tm, t)
    nh, nk = PEER_HEADS, PEER_N_KEYS
    once = pl.Buffered(1)
    nb = ne // te
    last = nb - 1
    return pl.pallas_call(
        _peer_dense_kernel,
        grid=(t // tm, nb + 1),
        in_specs=[
            pl.BlockSpec((te, d), lambda i, e: (jnp.minimum(e, last), 0)),
            pl.BlockSpec((te, d), lambda i, e: (jnp.maximum(e - 1, 0), 0)),
            pl.BlockSpec((tm, d), lambda i, e: (i, 0), pipeline_mode=once),
            pl.BlockSpec((tm, d), lambda i, e: (i, 0), pipeline_mode=once),
            pl.BlockSpec((te // nk, nh, tm), lambda i, e: (jnp.minimum(e, last), 0, i)),
            pl.BlockSpec((nh, nk, tm), lambda i, e: (0, 0, i), pipeline_mode=once),
            pl.BlockSpec((nh, tm), lambda i, e: (0, i)),
        ],
        out_specs=pl.BlockSpec((tm, d), lambda i, e: (i, 0)),
        out_shape=jax.ShapeDtypeStruct((t, d), F32),
        scratch_shapes=[pltpu.VMEM((2, te, tm), BF16)],
        compiler_params=_cparams("parallel", "arbitrary"),
        name="peer_dense",
    )(u_tab, v_tab, hq, x, e1_t, e2, c)


def _peer(x, norm_g, w_q, sub_keys, u_tabs, v_tabs, layer):
    hq = _rmsnorm(x, norm_g)
    nh, nk = PEER_HEADS, PEER_N_KEYS
    s_t = _peer_scores(hq, w_q.T.astype(BF16), sub_keys.reshape(2 * nh, nk, -1).astype(BF16))
    e1, e2, c = _peer_route(s_t)
    return _peer_dense(x, hq, _cast_bf16(u_tabs, layer), _cast_bf16(v_tabs, layer), e1.transpose(1, 0, 2), e2, c)


def _even_layer(x, i, norm_g, w_ins, log_step, a_re, a_im, b_re, b_im, c_re, c_im, d_skip,
                glu_ws, glu_b, gn_g, gn_b, w_outs):
    s5w = d_skip.shape[0]
    h = _rmsnorm(x, norm_g)
    u_t = _matmul([h], [_cast_bf16(w_ins, i, 0, s5w)], tile_major=True, name="ev_in_s5")
    z4 = _matmul([h], [_cast_bf16(w_ins, i, s5w)], name="ev_in_ret")
    tables = _s5_tables(log_step, a_re, a_im, b_re, b_im, c_re, c_im, d_skip)
    ya = _glu(_s5_core(u_t, tables), _cast_bf16(glu_ws, i), glu_b)
    yb = _retention(z4, gn_g, gn_b)
    w_out = _cast_bf16(w_outs, i)
    return _matmul([ya, yb], [w_out[:s5w], w_out[s5w:]], resid=x, out_dtype=F32, name="ev_out")


def _odd_layer(x, i, norm_g, w_ins, q_norm_g, k_norm_g, lq1, lk1, lq2, lk2, sub_norm_g, w_outs,
               rel_bias, lambda_init):
    d = x.shape[1]
    h = _rmsnorm(x, norm_g)
    z = _matmul([h], [_cast_bf16(w_ins, i, 0, 2 * d)], name="od_in_qk")
    v_t = _matmul_nt(w_ins[i, :, 2 * d:].T.astype(BF16), h, name="od_in_vt")
    reps = d // DIFF_HEAD_DIM
    q_t = _qknorm(z, jnp.tile(q_norm_g, reps) * (DIFF_HEAD_DIM ** -0.5 * LOG2E), 0, True)
    kn = _qknorm(z, jnp.tile(k_norm_g, reps), d, False)
    lam = jnp.exp(jnp.sum(lq1 * lk1)) - jnp.exp(jnp.sum(lq2 * lk2)) + lambda_init
    o = _diff_attention(q_t, kn, v_t, rel_bias, lam, sub_norm_g, lambda_init)
    return _matmul([o], [_cast_bf16(w_outs, i)], resid=x, out_dtype=F32, name="od_out")


def kernel(x, ev_norm_g, ev_w_in, s5_log_step, s5_a_re, s5_a_im, s5_b_re, s5_b_im, s5_c_re, s5_c_im,
           s5_d, s5_glu_w, s5_glu_b, ret_gn_g, ret_gn_b, ev_w_out, od_norm_g, od_w_in, diff_q_norm_g,
           diff_k_norm_g, diff_lq1, diff_lk1, diff_lq2, diff_lk2, diff_sub_norm_g, od_w_out, rel_bias,
           ffn_norm_g, peer_w_q, peer_sub_keys, peer_u, peer_v):
    bsz, t, d = x.shape
    depth = ffn_norm_g.shape[0]
    outs = []
    for b in range(bsz):
        xb = x[b]
        for layer in range(depth):
            i = layer // 2
            if layer % 2 == 0:
                xb = _even_layer(xb, i, ev_norm_g[i], ev_w_in, s5_log_step[i], s5_a_re[i], s5_a_im[i],
                                 s5_b_re[i], s5_b_im[i], s5_c_re[i], s5_c_im[i], s5_d[i], s5_glu_w,
                                 s5_glu_b[i], ret_gn_g[i], ret_gn_b[i], ev_w_out)
            else:
                lambda_init = 0.8 - 0.6 * math.exp(-0.3 * layer)
                xb = _odd_layer(xb, i, od_norm_g[i], od_w_in, diff_q_norm_g[i], diff_k_norm_g[i],
                                diff_lq1[i], diff_lk1[i], diff_lq2[i], diff_lk2[i], diff_sub_norm_g[i],
                                od_w_out, rel_bias, lambda_init)
            xb = _peer(xb, ffn_norm_g[layer], peer_w_q[layer], peer_sub_keys[layer], peer_u, peer_v, layer)
        outs.append(xb)
    return jnp.stack(outs, 0)
```

```python
import functools
import math

import jax
import jax.numpy as jnp
from jax import lax
from jax.experimental import pallas as pl
from jax.experimental.pallas import tpu as pltpu

F32 = jnp.float32
BF16 = jnp.bfloat16

EPS = 1e-6
LOG2E = 1.4426950408889634
LANES = 128
VMEM_LIMIT_BYTES = 56 * 1024 * 1024

S5_GROUP = 16
S5_STATE = 64
S5_CHUNK = 16
RET_HEADS = 8
RET_CHUNK = 128
ROPE_BASE = 10000.0
DIFF_HEADS = 16
DIFF_HEAD_DIM = 128
ATTN_BLOCK = 512
ATTN_HEADS_PER_STEP = 2
REL_BUCKETS = 32
REL_MAX_DIST = 128
PEER_HEADS = 8
PEER_N_KEYS = 128
PEER_TOPK = 16
GATE_HEAD_ROWS = 16


def _cparams(*sem, flags=None):
    return pltpu.CompilerParams(dimension_semantics=sem, vmem_limit_bytes=VMEM_LIMIT_BYTES, flags=flags)


def _dot(a, b):
    return jnp.dot(a, b, preferred_element_type=F32)


def _dot_nt(a, b):
    return lax.dot_general(a, b, (((1,), (1,)), ((), ())), preferred_element_type=F32)


def _dot_tn(a, b):
    return lax.dot_general(a, b, (((0,), (0,)), ((), ())), preferred_element_type=F32)


def _gelu(x):
    return 0.5 * x * (1.0 + jnp.tanh(0.7978845608028654 * (x + 0.044715 * (x * x * x))))


def _norm_kernel(x_ref, g_ref, o_ref):
    x = x_ref[...]
    r = lax.rsqrt(jnp.mean(x * x, axis=-1, keepdims=True) + EPS)
    o_ref[...] = (x * r * g_ref[...]).astype(o_ref.dtype)


def _rmsnorm(x, g, bt=256):
    t, d = x.shape
    return pl.pallas_call(
        _norm_kernel,
        grid=(t // bt,),
        in_specs=[pl.BlockSpec((bt, d), lambda i: (i, 0)), pl.BlockSpec((1, d), lambda i: (0, 0))],
        out_specs=pl.BlockSpec((bt, d), lambda i: (i, 0)),
        out_shape=jax.ShapeDtypeStruct((t, d), BF16),
        compiler_params=_cparams("parallel"),
        name="rmsnorm",
    )(x, g.reshape(1, d).astype(F32))


def _addnorm_kernel(x_ref, pt_ref, g_ref, xo_ref, h_ref=None):
    x = x_ref[...] + pt_ref[...].T
    xo_ref[...] = x
    if h_ref is not None:
        r = lax.rsqrt(jnp.mean(x * x, axis=-1, keepdims=True) + EPS)
        h_ref[...] = (x * r * g_ref[...]).astype(h_ref.dtype)


def _add_norm(x, p_t, g=None, bt=256):
    t, d = x.shape
    with_norm = g is not None
    row = pl.BlockSpec((bt, d), lambda i: (i, 0))
    g = jnp.ones((d,), F32) if g is None else g
    out_specs = [row, row] if with_norm else [row]
    out_shape = [jax.ShapeDtypeStruct((t, d), F32)] + ([jax.ShapeDtypeStruct((t, d), BF16)] if with_norm else [])
    out = pl.pallas_call(
        _addnorm_kernel,
        grid=(t // bt,),
        in_specs=[row, pl.BlockSpec((d, bt), lambda i: (0, i)), pl.BlockSpec((1, d), lambda i: (0, 0))],
        out_specs=out_specs,
        out_shape=out_shape,
        compiler_params=_cparams("parallel"),
        name="add_norm" if with_norm else "add_update",
    )(x, p_t, g.reshape(1, d).astype(F32))
    return (out[0], out[1]) if with_norm else (out[0], None)


def _cast_kernel(w_ref, o_ref, *, transpose):
    w = w_ref[0]
    o_ref[...] = (w.T if transpose else w).astype(o_ref.dtype)


def _cast_bf16(w, layer, col0=0, ncols=None, transpose=False, br=512, bc=4096):
    _, r, c = w.shape
    ncols = c - col0 if ncols is None else ncols
    bc = math.gcd(math.gcd(bc, ncols), col0) if col0 else math.gcd(bc, ncols)
    br = min(br, r)
    off = col0 // bc
    if transpose:
        out_spec = pl.BlockSpec((bc, br), lambda i, j: (j, i))
        out_shape = jax.ShapeDtypeStruct((ncols, r), BF16)
    else:
        out_spec = pl.BlockSpec((br, bc), lambda i, j: (i, j))
        out_shape = jax.ShapeDtypeStruct((r, ncols), BF16)
    return pl.pallas_call(
        functools.partial(_cast_kernel, transpose=transpose),
        grid=(r // br, ncols // bc),
        in_specs=[pl.BlockSpec((1, br, bc), lambda i, j: (layer, i, off + j))],
        out_specs=out_spec,
        out_shape=out_shape,
        compiler_params=_cparams("parallel", "parallel"),
        name="cast_bf16_t" if transpose else "cast_bf16",
    )(w)


def _mm_kernel(*refs, n_lhs, has_resid, tile_major):
    a_refs, b_refs = refs[:n_lhs], refs[n_lhs:2 * n_lhs]
    r_ref = refs[2 * n_lhs] if has_resid else None
    o_ref = refs[-1]
    acc = _dot(a_refs[0][...], b_refs[0][...])
    for a_ref, b_ref in zip(a_refs[1:], b_refs[1:]):
        acc = acc + _dot(a_ref[...], b_ref[...])
    if has_resid:
        acc = acc + r_ref[...]
    if tile_major:
        for jj in range(o_ref.shape[0]):
            o_ref[jj] = acc[:, jj * LANES:(jj + 1) * LANES].astype(o_ref.dtype)
    else:
        o_ref[...] = acc.astype(o_ref.dtype)


def _matmul(lhs, rhs, *, resid=None, out_dtype=BF16, bm=512, bn=1024, tile_major=False, name="mm"):
    m, n = lhs[0].shape[0], rhs[0].shape[1]
    bm, bn = min(bm, m), min(bn, n)
    in_specs = [pl.BlockSpec((bm, a.shape[1]), lambda i, j: (i, 0)) for a in lhs]
    in_specs += [pl.BlockSpec((b.shape[0], bn), lambda i, j: (0, j)) for b in rhs]
    args = list(lhs) + list(rhs)
    if resid is not None:
        in_specs.append(pl.BlockSpec((bm, bn), lambda i, j: (i, j)))
        args.append(resid)
    if tile_major:
        out_spec = pl.BlockSpec((bn // LANES, bm, LANES), lambda i, j: (j, i, 0))
        out_shape = jax.ShapeDtypeStruct((n // LANES, m, LANES), out_dtype)
    else:
        out_spec = pl.BlockSpec((bm, bn), lambda i, j: (i, j))
        out_shape = jax.ShapeDtypeStruct((m, n), out_dtype)
    kern = functools.partial(_mm_kernel, n_lhs=len(lhs), has_resid=resid is not None, tile_major=tile_major)
    return pl.pallas_call(
        kern, grid=(m // bm, n // bn), in_specs=in_specs, out_specs=out_spec, out_shape=out_shape,
        compiler_params=_cparams("parallel", "parallel"), name=name,
    )(*args)


def _mm_nt_kernel(w_ref, h_ref, o_ref):
    o_ref[...] = _dot_nt(w_ref[...], h_ref[...]).astype(o_ref.dtype)


def _matmul_nt(w_t, h, *, bm=512, bn=1024, out_dtype=BF16, name="mm_nt"):
    n, k = w_t.shape
    m = h.shape[0]
    bm, bn = min(bm, m), min(bn, n)
    return pl.pallas_call(
        _mm_nt_kernel,
        grid=(m // bm, n // bn),
        in_specs=[pl.BlockSpec((bn, k), lambda i, j: (j, 0)), pl.BlockSpec((bm, k), lambda i, j: (i, 0))],
        out_specs=pl.BlockSpec((bn, bm), lambda i, j: (j, i)),
        out_shape=jax.ShapeDtypeStruct((n, m), out_dtype),
        compiler_params=_cparams("parallel", "parallel"),
        name=name,
    )(w_t, h)


def _s5_tables(log_step, a_re, a_im, b_re, b_im, c_re, c_im, d_skip):
    L = S5_CHUNK
    g, p = a_re.shape
    gt = LANES // S5_GROUP
    nt = g // gt
    hp = lax.Precision.HIGHEST
    step = jnp.exp(log_step)[:, None]
    lr, li = a_re, a_im
    mag = jnp.exp(lr * step)
    abar_re = mag * jnp.cos(li * step)
    abar_im = mag * jnp.sin(li * step)
    den = lr * lr + li * li
    num_re = abar_re - 1.0
    f_re = (num_re * lr + abar_im * li) / den
    f_im = (abar_im * lr - num_re * li) / den
    bb_re = f_re[..., None] * b_re - f_im[..., None] * b_im
    bb_im = f_re[..., None] * b_im + f_im[..., None] * b_re
    k = jnp.arange(L + 1, dtype=F32)[:, None, None]
    pmag = jnp.exp(k * (lr * step)[None])
    pang = k * (li * step)[None]
    pw_re = pmag * jnp.cos(pang)
    pw_im = pmag * jnp.sin(pang)
    ns = 2 * gt * p

    def same_group(row_g, col_g):
        return (row_g[:, None] == col_g[None, :]).astype(F32)

    cp_re = c_re[None] * pw_re[:L, :, None, :] - c_im[None] * pw_im[:L, :, None, :]
    cp_im = c_re[None] * pw_im[:L, :, None, :] + c_im[None] * pw_re[:L, :, None, :]
    kk = (jnp.einsum("dgop,gpi->dgoi", cp_re, bb_re, precision=hp)
          - jnp.einsum("dgop,gpi->dgoi", cp_im, bb_im, precision=hp))
    kki = kk.transpose(0, 1, 3, 2).reshape(L, nt, LANES, S5_GROUP)
    lane_g = jnp.arange(LANES) // S5_GROUP
    bd = jnp.tile(kki, (1, 1, 1, gt)) * same_group(lane_g, lane_g)
    bd = bd.transpose(1, 0, 2, 3).astype(BF16)

    k_rev = (L - 1) - k[:L]
    pmag_rev = jnp.exp(k_rev * (lr * step)[None])
    pang_rev = k_rev * (li * step)[None]
    pwr, pwi = pmag_rev * jnp.cos(pang_rev), pmag_rev * jnp.sin(pang_rev)
    pb_re = pwr[..., None] * bb_re[None] - pwi[..., None] * bb_im[None]
    pb_im = pwr[..., None] * bb_im[None] + pwi[..., None] * bb_re[None]
    pb = jnp.stack([pb_re, pb_im], 0).reshape(2, L, nt, gt, p, S5_GROUP)
    ps = pb.transpose(2, 1, 3, 5, 0, 4).reshape(nt, L * LANES, 2 * p).astype(BF16)
    rr, cc = jnp.arange(2 * p), jnp.arange(ns)
    ex_p = ((rr[:, None] // p == cc[None, :] // (gt * p)) & (rr[:, None] % p == cc[None, :] % p))
    row_g = (jnp.arange(L * LANES) % LANES) // S5_GROUP
    col_g = (cc % (gt * p)) // p
    p_op = (jnp.einsum("jrk,kc->jrc", ps, ex_p.astype(BF16), preferred_element_type=F32)
            * same_group(row_g, col_g)).astype(BF16)

    qr, qi = pw_re[1:L + 1], pw_im[1:L + 1]
    cn_re = c_re[None] * qr[:, :, None, :] - c_im[None] * qi[:, :, None, :]
    cn_im = c_re[None] * qi[:, :, None, :] + c_im[None] * qr[:, :, None, :]
    nb = jnp.stack([cn_re, -cn_im], 0).reshape(2, L, nt, gt, S5_GROUP, p)
    nsm = nb.transpose(2, 0, 3, 5, 1, 4).reshape(nt, ns, L * S5_GROUP).astype(BF16)
    rr, cc = jnp.arange(L * S5_GROUP), jnp.arange(L * LANES)
    ex_n = ((rr[:, None] // S5_GROUP == cc[None, :] // LANES)
            & (rr[:, None] % S5_GROUP == cc[None, :] % S5_GROUP))
    row_g = (jnp.arange(ns) % (gt * p)) // p
    col_g = (cc % LANES) // S5_GROUP
    n_op = (jnp.einsum("jrk,kc->jrc", nsm, ex_n.astype(BF16), preferred_element_type=F32)
            * same_group(row_g, col_g)).astype(BF16)

    a_l = jnp.stack([pw_re[L], pw_im[L]], 0).reshape(2, nt, gt * p).transpose(1, 0, 2)
    d_t = jnp.tile(d_skip.reshape(nt, 1, LANES), (1, 1, L))
    return bd, p_op, n_op, a_l, d_t


def _s5_kernel(x_ref, bd_ref, p_ref, n_ref, al_ref, d_ref, o_ref, m_ref, sloc_ref, sprev_ref):
    x = x_ref[0]
    nc, ns = sloc_ref.shape
    half = ns // 2
    L = bd_ref.shape[1]
    for tau in range(L):
        for t in range(L):
            blk = bd_ref[0, t - tau] if t >= tau else jnp.zeros((LANES, LANES), m_ref.dtype)
            m_ref[tau * LANES:(tau + 1) * LANES, t * LANES:(t + 1) * LANES] = blk
    sloc_ref[...] = _dot(x, p_ref[0])
    a_re = al_ref[0, 0:1, :]
    a_im = al_ref[0, 1:2, :]

    def body(c, carry):
        s_re, s_im = carry
        row = pl.ds(c, 1)
        sprev_ref[row, 0:half] = s_re
        sprev_ref[row, half:ns] = s_im
        l_re = sloc_ref[row, 0:half]
        l_im = sloc_ref[row, half:ns]
        return (a_re * s_re - a_im * s_im + l_re, a_re * s_im + a_im * s_re + l_im)

    zero = jnp.zeros((1, half), F32)
    lax.fori_loop(0, nc, body, (zero, zero))
    y = _dot(x, m_ref[...]) + _dot(sprev_ref[...].astype(BF16), n_ref[0])
    y = y + d_ref[0] * x.astype(F32)
    o_ref[0] = _gelu(y).astype(o_ref.dtype)


def _s5_core(u_t, tables):
    bd, p_op, n_op, a_l, d_t = tables
    nt, t, _ = u_t.shape
    L = S5_CHUNK
    nc, w, ns = t // L, L * LANES, p_op.shape[2]
    x = u_t.reshape(nt, nc, w)
    out = pl.pallas_call(
        _s5_kernel,
        grid=(nt,),
        in_specs=[
            pl.BlockSpec((1, nc, w), lambda j: (j, 0, 0)),
            pl.BlockSpec((1, L, LANES, LANES), lambda j: (j, 0, 0, 0)),
            pl.BlockSpec((1, w, ns), lambda j: (j, 0, 0)),
            pl.BlockSpec((1, ns, w), lambda j: (j, 0, 0)),
            pl.BlockSpec((1, 2, ns // 2), lambda j: (j, 0, 0)),
            pl.BlockSpec((1, 1, w), lambda j: (j, 0, 0)),
        ],
        out_specs=pl.BlockSpec((1, nc, w), lambda j: (j, 0, 0)),
        out_shape=jax.ShapeDtypeStruct((nt, nc, w), BF16),
        scratch_shapes=[pltpu.VMEM((w, w), BF16), pltpu.VMEM((nc, ns), F32), pltpu.VMEM((nc, ns), F32)],
        compiler_params=_cparams("parallel"),
        name="s5_core",
    )(x, bd, p_op, n_op, a_l, d_t)
    return out.reshape(nt, t, LANES)


def _glu_kernel(y_ref, w_ref, b_ref, o_ref):
    y = jnp.concatenate([y_ref[j] for j in range(y_ref.shape[0])], axis=1)
    acc = _dot(y, w_ref[...]) + b_ref[...]
    o_ref[...] = (y.astype(F32) * jax.nn.sigmoid(acc)).astype(o_ref.dtype)


def _glu(y_t, w, b, bm=512):
    nt, t, _ = y_t.shape
    width = nt * LANES
    return pl.pallas_call(
        _glu_kernel,
        grid=(t // bm,),
        in_specs=[
            pl.BlockSpec((nt, bm, LANES), lambda i: (0, i, 0)),
            pl.BlockSpec((width, width), lambda i: (0, 0)),
            pl.BlockSpec((1, width), lambda i: (0, 0)),
        ],
        out_specs=pl.BlockSpec((bm, width), lambda i: (i, 0)),
        out_shape=jax.ShapeDtypeStruct((t, width), BF16),
        compiler_params=_cparams("parallel"),
        name="s5_glu",
    )(y_t, w, b.reshape(1, width).astype(F32))


def _ret_kernel(q_ref, k_ref, v_ref, g_ref, cos_ref, sin_ref, intra_ref, qd_ref, kd_ref, cd_ref,
                gg_ref, gb_ref, o_ref, state_ref, *, chunk, scale):
    @pl.when(pl.program_id(1) == 0)
    def _():
        state_ref[...] = jnp.zeros_like(state_ref)

    intra, qd, kd, cd = intra_ref[0], qd_ref[0], kd_ref[0], cd_ref[0]
    gg, gb = gg_ref[...], gb_ref[...]
    half = cos_ref.shape[1]

    def body(c, carry):
        rows = pl.ds(pl.multiple_of(c * chunk, chunk), chunk)
        cos, sin = cos_ref[rows, :], sin_ref[rows, :]

        def rot(x):
            x1, x2 = x[:, :half], x[:, half:]
            return jnp.concatenate([x1 * cos - x2 * sin, x1 * sin + x2 * cos], axis=1)

        q = rot(q_ref[rows, :].astype(F32))
        k = rot(k_ref[rows, :].astype(F32)) * scale
        v = v_ref[rows, :]
        st = state_ref[...]
        scores = _dot_nt(q.astype(BF16), k.astype(BF16)) * intra
        out = _dot(scores.astype(BF16), v) + _dot((q * qd).astype(BF16), st.astype(BF16))
        state_ref[...] = cd * st + _dot_tn((k * kd).astype(BF16), v)
        mu = jnp.mean(out, axis=-1, keepdims=True)
        cen = out - mu
        var = jnp.mean(cen * cen, axis=-1, keepdims=True)
        o = cen * lax.rsqrt(var + EPS) * gg + gb
        gt = g_ref[rows, :].astype(F32)
        o_ref[rows, :] = (gt * jax.nn.sigmoid(gt) * o).astype(o_ref.dtype)
        return carry

    lax.fori_loop(0, q_ref.shape[0] // chunk, body, 0)


def _retention(z4, gn_g, gn_b, tb=2048):
    t = z4.shape[0]
    w = z4.shape[1] // 4
    hd = w // RET_HEADS
    half = hd // 2
    c = RET_CHUNK
    tb = min(tb, t)
    pos = jnp.arange(t, dtype=F32)
    freqs = ROPE_BASE ** (-jnp.arange(half, dtype=F32) / half)
    ang = pos[:, None] * freqs[None, :]
    cos, sin = jnp.cos(ang), jnp.sin(ang)
    gamma = 1.0 - 2.0 ** (-5.0 - jnp.arange(RET_HEADS, dtype=F32))
    log_g = jnp.log(gamma)
    idx = jnp.arange(c, dtype=F32)
    rel = idx[:, None] - idx[None, :]
    intra = jnp.where(rel >= 0, jnp.exp(log_g[:, None, None] * jnp.maximum(rel, 0.0)), 0.0)
    q_decay = jnp.exp(log_g[:, None] * (idx + 1.0))[..., None]
    k_decay = jnp.exp(log_g[:, None] * (c - 1.0 - idx))[..., None]
    chunk_decay = jnp.exp(log_g * c)[:, None, None]
    nh = RET_HEADS
    blk = lambda off: pl.BlockSpec((tb, hd), lambda h, s: (s, off + h))
    per_head = lambda shape: pl.BlockSpec((1,) + shape, lambda h, s: (h, 0, 0))
    return pl.pallas_call(
        functools.partial(_ret_kernel, chunk=c, scale=hd ** -0.5),
        grid=(nh, t // tb),
        in_specs=[
            blk(0), blk(nh), blk(2 * nh), blk(3 * nh),
            pl.BlockSpec((tb, half), lambda h, s: (s, 0)),
            pl.BlockSpec((tb, half), lambda h, s: (s, 0)),
            per_head((c, c)), per_head((c, 1)), per_head((c, 1)), per_head((1, 1)),
            pl.BlockSpec((1, hd), lambda h, s: (0, h)),
            pl.BlockSpec((1, hd), lambda h, s: (0, h)),
        ],
        out_specs=pl.BlockSpec((tb, hd), lambda h, s: (s, h)),
        out_shape=jax.ShapeDtypeStruct((t, w), BF16),
        scratch_shapes=[pltpu.VMEM((hd, hd), F32)],
        compiler_params=_cparams("parallel", "arbitrary"),
        name="retention",
    )(z4, z4, z4, z4, cos, sin, intra, q_decay, k_decay, chunk_decay,
      gn_g.reshape(1, w).astype(F32), gn_b.reshape(1, w).astype(F32))


def _qknorm_kernel(x_ref, gain_ref, ones_ref, o_ref, *, inv_dim, transpose_out):
    x = x_ref[...].astype(F32)
    xx = x * x
    hi = xx.astype(BF16)
    lo = (xx - hi.astype(F32)).astype(BF16)
    ss = _dot(hi, ones_ref[...]) + _dot(lo, ones_ref[...])
    y = x * lax.rsqrt(ss * inv_dim + EPS) * gain_ref[...]
    o_ref[...] = (y.T if transpose_out else y).astype(o_ref.dtype)


def _qknorm(z, gain, col_off, transpose_out, bt=1024, bw=256):
    t = z.shape[0]
    w = gain.shape[0]
    bt = min(bt, t)
    off = col_off // bw
    grp = jnp.arange(bw) // DIFF_HEAD_DIM
    ones = (grp[:, None] == grp[None, :]).astype(BF16)
    if transpose_out:
        out_spec = pl.BlockSpec((bw, bt), lambda i, j: (j, i))
        out_shape = jax.ShapeDtypeStruct((w, t), BF16)
    else:
        out_spec = pl.BlockSpec((bt, bw), lambda i, j: (i, j))
        out_shape = jax.ShapeDtypeStruct((t, w), BF16)
    return pl.pallas_call(
        functools.partial(_qknorm_kernel, inv_dim=1.0 / DIFF_HEAD_DIM, transpose_out=transpose_out),
        grid=(t // bt, w // bw),
        in_specs=[
            pl.BlockSpec((bt, bw), lambda i, j: (i, off + j)),
            pl.BlockSpec((1, bw), lambda i, j: (0, j)),
            pl.BlockSpec((bw, bw), lambda i, j: (0, 0)),
        ],
        out_specs=out_spec,
        out_shape=out_shape,
        compiler_params=_cparams("parallel", "parallel"),
        name="q_norm_t" if transpose_out else "k_norm",
    )(z, gain.reshape(1, w).astype(F32), ones)


def _attn_kernel(qt_ref, k_ref, vt_ref, d0_ref, d1_ref, lam_ref, sg_ref, o_ref, m_ref, acc_ref, va_ref,
                 *, blk, out_scale):
    i = pl.program_id(1)
    hd = DIFF_HEAD_DIM
    dv = 2 * hd
    nhs = va_ref.shape[0]

    @pl.when(i == 0)
    def _():
        for hh in range(nhs):
            for jb in range(va_ref.shape[1]):
                va_ref[hh, jb, 0:dv, :] = vt_ref[hh * dv:(hh + 1) * dv, jb * blk:(jb + 1) * blk]
                va_ref[hh, jb, dv:, :] = jnp.ones((va_ref.shape[2] - dv, blk), va_ref.dtype)

    m_ref[...] = jnp.full(m_ref.shape, -1e30, F32)
    acc_ref[...] = jnp.zeros_like(acc_ref)

    def step(j, bias_ref):
        rows = pl.ds(pl.multiple_of(j * blk, blk), blk)
        for hh in range(nhs):
            va = va_ref[hh, j]
            for mi in range(2):
                c = 2 * hh + mi
                s = _dot(k_ref[rows, c * hd:(c + 1) * hd], qt_ref[c * hd:(c + 1) * hd, :])
                if bias_ref is not None:
                    s = s + bias_ref[hh]
                m_old = m_ref[c]
                m_new = jnp.maximum(m_old, jnp.max(s, axis=0, keepdims=True))
                p = jnp.exp2(s - m_new)
                alpha = jnp.exp2(m_old - m_new)
                acc_ref[c] = alpha * acc_ref[c] + _dot(va, p.astype(BF16))
                m_ref[c] = m_new

    def far(j, carry):
        step(j, None)
        return carry

    lax.fori_loop(0, jnp.maximum(i - 1, 0), far, 0)

    @pl.when(i >= 1)
    def _():
        step(i - 1, d1_ref)

    step(i, d0_ref)
    for hh in range(nhs):
        a0, a1 = acc_ref[2 * hh], acc_ref[2 * hh + 1]
        o_t = a0[:dv] / a0[dv:dv + 1] - lam_ref[...] * (a1[:dv] / a1[dv:dv + 1])
        o = o_t.T
        ms = jnp.mean(o * o, axis=-1, keepdims=True)
        o_ref[:, hh * dv:(hh + 1) * dv] = (o * lax.rsqrt(ms + EPS) * sg_ref[...] * out_scale).astype(o_ref.dtype)


def _t5_bucket(n):
    max_exact = REL_BUCKETS // 2
    nf = jnp.maximum(n, 1).astype(F32)
    large = max_exact + (jnp.log(nf / max_exact) / math.log(REL_MAX_DIST / max_exact)
                         * (REL_BUCKETS - max_exact)).astype(jnp.int32)
    large = jnp.minimum(large, REL_BUCKETS - 1)
    return jnp.where(n < max_exact, n, large)


def _diff_attention(q_t, kn, v_t, rel_bias, lam, sub_g, lambda_init):
    t = kn.shape[0]
    nh = DIFF_HEADS
    dv = 2 * DIFF_HEAD_DIM
    ones_rows = 16
    blk = min(ATTN_BLOCK, t)
    nhs = ATTN_HEADS_PER_STEP
    assert blk >= REL_MAX_DIST
    key = jnp.arange(blk)[:, None]
    qry = jnp.arange(blk)[None, :]
    far_bias = rel_bias[REL_BUCKETS - 1]

    def bias_tile(rel):
        onehot = (_t5_bucket(rel).reshape(-1)[None, :] == jnp.arange(REL_BUCKETS)[:, None]).astype(F32)
        tile = jnp.dot((rel_bias - far_bias).T * LOG2E, onehot, precision=lax.Precision.HIGHEST)
        return tile.reshape(nh, blk, blk)

    rel0 = qry - key
    d0 = jnp.where((rel0 >= 0)[None], bias_tile(jnp.maximum(rel0, 0)), -jnp.inf)
    d1 = bias_tile(blk + qry - key)
    once = pl.Buffered(1)
    return pl.pallas_call(
        functools.partial(_attn_kernel, blk=blk, out_scale=1.0 - lambda_init),
        grid=(nh // nhs, t // blk),
        in_specs=[
            pl.BlockSpec((nhs * dv, blk), lambda h, i: (h, i)),
            pl.BlockSpec((t, nhs * dv), lambda h, i: (0, h), pipeline_mode=once),
            pl.BlockSpec((nhs * dv, t), lambda h, i: (h, 0), pipeline_mode=once),
            pl.BlockSpec((nhs, blk, blk), lambda h, i: (h, 0, 0), pipeline_mode=once),
            pl.BlockSpec((nhs, blk, blk), lambda h, i: (h, 0, 0), pipeline_mode=once),
            pl.BlockSpec((1, 1), lambda h, i: (0, 0)),
            pl.BlockSpec((1, dv), lambda h, i: (0, 0)),
        ],
        out_specs=pl.BlockSpec((blk, nhs * dv), lambda h, i: (i, h)),
        out_shape=jax.ShapeDtypeStruct((t, nh * dv), BF16),
        scratch_shapes=[pltpu.VMEM((2 * nhs, 1, blk), F32), pltpu.VMEM((2 * nhs, dv + ones_rows, blk), F32),
                        pltpu.VMEM((nhs, t // blk, dv + ones_rows, blk), BF16)],
        compiler_params=_cparams("parallel", "arbitrary"),
        name="diff_attn",
    )(q_t, kn, v_t, d0, d1, jnp.reshape(lam, (1, 1)).astype(F32), sub_g.reshape(1, dv).astype(F32))


def _pscore_kernel(w_ref, h_ref, sk_ref, o_ref):
    q_t = _dot_nt(w_ref[...], h_ref[...])
    for b in range(sk_ref.shape[0]):
        rows = slice(b * LANES, (b + 1) * LANES)
        o_ref[rows, :] = _dot(sk_ref[b], q_t[rows, :].astype(BF16))


def _peer_scores(hq, wq_t, sub_keys, tm=512, bn=512):
    t, d = hq.shape
    n = wq_t.shape[0]
    tm = min(tm, t)
    nb = bn // LANES
    return pl.pallas_call(
        _pscore_kernel,
        grid=(t // tm, n // bn),
        in_specs=[
            pl.BlockSpec((bn, d), lambda i, j: (j, 0)),
            pl.BlockSpec((tm, d), lambda i, j: (i, 0)),
            pl.BlockSpec((nb, LANES, LANES), lambda i, j: (j, 0, 0)),
        ],
        out_specs=pl.BlockSpec((bn, tm), lambda i, j: (j, i)),
        out_shape=jax.ShapeDtypeStruct((n, t), F32),
        compiler_params=_cparams("parallel", "parallel"),
        name="peer_scores",
    )(wq_t, hq, sub_keys)


def _top_values(s, n):
    vals = []
    for r in range(n):
        m = jnp.max(s, axis=0, keepdims=True)
        vals.append(m)
        if r < n - 1:
            s = jnp.where(s == m, -jnp.inf, s)
    return vals


def _route_kernel(s_ref, e1_ref, e2_ref, c_ref):
    nk = PEER_N_KEYS
    k = PEER_TOPK
    tl = s_ref.shape[1]
    pad = 24
    row = lax.broadcasted_iota(jnp.int32, (pad, tl), 0)
    hrow = lax.broadcasted_iota(jnp.int32, (PEER_HEADS, tl), 0)
    c_all = jnp.zeros((PEER_HEADS, tl), F32)
    for h in range(PEER_HEADS):
        s1 = s_ref[2 * h * nk:(2 * h + 1) * nk, :]
        s2 = s_ref[(2 * h + 1) * nk:(2 * h + 2) * nk, :]
        a = _top_values(s1, k + 1)
        b = _top_values(s2, k + 1)
        ea = [jnp.exp(x - a[0]) for x in a]
        eb = [jnp.exp(x - b[0]) for x in b]
        ea_m = jnp.full((pad, tl), -1.0, F32)
        eb_m = jnp.full((pad, tl), -1.0, F32)
        for r in range(k + 1):
            ea_m = jnp.where(row == r, ea[r], ea_m)
            eb_m = jnp.where(row == r, eb[r], eb_m)
        cand = jnp.concatenate(
            [ea[0] * eb_m, ea[1] * eb_m[:8], ea[2] * eb_m[:8], ea[3] * eb_m[:8],
             eb[0] * ea_m, eb[1] * ea_m[:8], eb[2] * ea_m[:8]], axis=0)
        v = _top_values(cand, k + 1)
        z = v[0]
        for r in range(1, k):
            z = z + v[r]
        rz = 1.0 / z
        e1_ref[h] = jnp.exp(s1 - a[0])
        e2_ref[h] = jnp.exp(s2 - b[0]) * rz
        c_all = jnp.where(hrow == h, 0.5 * (v[k - 1] + v[k]) * rz, c_all)
    c_ref[...] = c_all


def _peer_route(s_t, tl=256):
    n, t = s_t.shape
    tl = min(tl, t)
    nh, nk = PEER_HEADS, PEER_N_KEYS
    return pl.pallas_call(
        _route_kernel,
        grid=(t // tl,),
        in_specs=[pl.BlockSpec((n, tl), lambda i: (0, i))],
        out_specs=[
            pl.BlockSpec((nh, nk, tl), lambda i: (0, 0, i)),
            pl.BlockSpec((nh, nk, tl), lambda i: (0, 0, i)),
            pl.BlockSpec((nh, tl), lambda i: (0, i)),
        ],
        out_shape=[
            jax.ShapeDtypeStruct((nh, nk, t), F32),
            jax.ShapeDtypeStruct((nh, nk, t), F32),
            jax.ShapeDtypeStruct((nh, t), F32),
        ],
        compiler_params=_cparams("parallel"),
        name="peer_route",
    )(s_t)


def _peer_dense_kernel(u_ref, vt_ref, h_ref, e1_ref, e2_ref, c_ref, o_ref, w_ref):
    @pl.when(pl.program_id(1) == 0)
    def _():
        o_ref[...] = jnp.zeros_like(o_ref)

    a_t = _dot_nt(u_ref[...], h_ref[...])
    tm = a_t.shape[1]
    for ii in range(e1_ref.shape[0]):
        rows = slice(ii * LANES, (ii + 1) * LANES)
        g = jnp.zeros((LANES, tm), e2_ref.dtype)
        for h in range(PEER_HEADS):
            vv = e2_ref[h] * e1_ref[ii, h:h + 1, :]
            g = g + jnp.where(vv >= c_ref[h:h + 1, :], vv, jnp.zeros_like(vv))
        w_ref[rows, :] = (g.astype(F32) * _gelu(a_t[rows, :])).astype(w_ref.dtype)
    o_ref[...] += _dot(vt_ref[...], w_ref[...])


def _peer_dense(hq, u_tab, v_tab_t, e1_t, e2, c, tm=512, te=512):
    t, d = hq.shape
    ne = u_tab.shape[0]
    tm = min(tm, t)
    nh, nk = PEER_HEADS, PEER_N_KEYS
    once = pl.Buffered(1)
    return pl.pallas_call(
        _peer_dense_kernel,
        grid=(t // tm, ne // te),
        in_specs=[
            pl.BlockSpec((te, d), lambda i, e: (e, 0)),
            pl.BlockSpec((d, te), lambda i, e: (0, e)),
            pl.BlockSpec((tm, d), lambda i, e: (i, 0), pipeline_mode=once),
            pl.BlockSpec((te // nk, GATE_HEAD_ROWS, tm), lambda i, e: (e, 0, i)),
            pl.BlockSpec((nh, nk, tm), lambda i, e: (0, 0, i), pipeline_mode=once),
            pl.BlockSpec((GATE_HEAD_ROWS, tm), lambda i, e: (0, i)),
        ],
        out_specs=pl.BlockSpec((d, tm), lambda i, e: (0, i)),
        out_shape=jax.ShapeDtypeStruct((d, t), F32),
        scratch_shapes=[pltpu.VMEM((te, tm), BF16)],
        compiler_params=_cparams("parallel", "arbitrary"),
        name="peer_dense",
    )(u_tab, v_tab_t, hq, e1_t, e2, c)


def _peer(x, norm_g, w_qs, sub_keys, u_tabs, v_tabs, layer):
    hq = _rmsnorm(x, norm_g)
    nh, nk = PEER_HEADS, PEER_N_KEYS
    s_t = _peer_scores(hq, _cast_bf16(w_qs, layer, transpose=True), sub_keys.reshape(2 * nh, nk, -1).astype(BF16))
    e1, e2, c = _peer_route(s_t)
    pad = GATE_HEAD_ROWS - nh
    e1_t = jnp.pad(e1.transpose(1, 0, 2), ((0, 0), (0, pad), (0, 0))).astype(BF16)
    c_p = jnp.pad(c, ((0, pad), (0, 0))).astype(BF16)
    return _peer_dense(hq, _cast_bf16(u_tabs, layer), _cast_bf16(v_tabs, layer, transpose=True), e1_t,
                       e2.astype(BF16), c_p)


def _even_layer(x, h, i, w_ins, log_step, a_re, a_im, b_re, b_im, c_re, c_im, d_skip,
                glu_ws, glu_b, gn_g, gn_b, w_outs):
    s5w = d_skip.shape[0]
    u_t = _matmul([h], [_cast_bf16(w_ins, i, 0, s5w)], tile_major=True, name="ev_in_s5")
    z4 = _matmul([h], [_cast_bf16(w_ins, i, s5w)], name="ev_in_ret")
    tables = _s5_tables(log_step, a_re, a_im, b_re, b_im, c_re, c_im, d_skip)
    ya = _glu(_s5_core(u_t, tables), _cast_bf16(glu_ws, i), glu_b)
    yb = _retention(z4, gn_g, gn_b)
    w_out = _cast_bf16(w_outs, i)
    return _matmul([ya, yb], [w_out[:s5w], w_out[s5w:]], resid=x, out_dtype=F32, name="ev_out")


def _odd_layer(x, h, i, w_ins, q_norm_g, k_norm_g, lq1, lk1, lq2, lk2, sub_norm_g, w_outs,
               rel_bias, lambda_init):
    d = x.shape[1]
    z = _matmul([h], [_cast_bf16(w_ins, i, 0, 2 * d)], name="od_in_qk")
    v_t = _matmul_nt(_cast_bf16(w_ins, i, 2 * d, transpose=True), h, name="od_in_vt")
    reps = d // DIFF_HEAD_DIM
    q_t = _qknorm(z, jnp.tile(q_norm_g, reps) * (DIFF_HEAD_DIM ** -0.5 * LOG2E), 0, True)
    kn = _qknorm(z, jnp.tile(k_norm_g, reps), d, False)
    lam = jnp.exp(jnp.sum(lq1 * lk1)) - jnp.exp(jnp.sum(lq2 * lk2)) + lambda_init
    o = _diff_attention(q_t, kn, v_t, rel_bias, lam, sub_norm_g, lambda_init)
    return _matmul([o], [_cast_bf16(w_outs, i)], resid=x, out_dtype=F32, name="od_out")


def kernel(x, ev_norm_g, ev_w_in, s5_log_step, s5_a_re, s5_a_im, s5_b_re, s5_b_im, s5_c_re, s5_c_im,
           s5_d, s5_glu_w, s5_glu_b, ret_gn_g, ret_gn_b, ev_w_out, od_norm_g, od_w_in, diff_q_norm_g,
           diff_k_norm_g, diff_lq1, diff_lk1, diff_lq2, diff_lk2, diff_sub_norm_g, od_w_out, rel_bias,
           ffn_norm_g, peer_w_q, peer_sub_keys, peer_u, peer_v):
    bsz, t, d = x.shape
    depth = ffn_norm_g.shape[0]
    mixer_norm_g = lambda layer: (ev_norm_g if layer % 2 == 0 else od_norm_g)[layer // 2]
    outs = []
    for b in range(bsz):
        xb = x[b]
        h = _rmsnorm(xb, mixer_norm_g(0))
        for layer in range(depth):
            i = layer // 2
            if layer % 2 == 0:
                xb = _even_layer(xb, h, i, ev_w_in, s5_log_step[i], s5_a_re[i], s5_a_im[i],
                                 s5_b_re[i], s5_b_im[i], s5_c_re[i], s5_c_im[i], s5_d[i], s5_glu_w,
                                 s5_glu_b[i], ret_gn_g[i], ret_gn_b[i], ev_w_out)
            else:
                lambda_init = 0.8 - 0.6 * math.exp(-0.3 * layer)
                xb = _odd_layer(xb, h, i, od_w_in, diff_q_norm_g[i], diff_k_norm_g[i],
                                diff_lq1[i], diff_lk1[i], diff_lq2[i], diff_lk2[i], diff_sub_norm_g[i],
                                od_w_out, rel_bias, lambda_init)
            p_t = _peer(xb, ffn_norm_g[layer], peer_w_q, peer_sub_keys[layer], peer_u, peer_v, layer)
            xb, h = _add_norm(xb, p_t, mixer_norm_g(layer + 1) if layer + 1 < depth else None)
        outs.append(xb)
    return jnp.stack(outs, 0)
```

```python
import functools
import math

import jax
import jax.numpy as jnp
from jax import lax
from jax.experimental import pallas as pl
from jax.experimental.pallas import tpu as pltpu

F32 = jnp.float32
BF16 = jnp.bfloat16

EPS = 1e-6
LOG2E = 1.4426950408889634
LANES = 128
VMEM_LIMIT_BYTES = 56 * 1024 * 1024

S5_GROUP = 16
S5_STATE = 64
S5_CHUNK = 16
RET_HEADS = 8
RET_CHUNK = 128
ROPE_BASE = 10000.0
DIFF_HEADS = 16
DIFF_HEAD_DIM = 128
ATTN_BLOCK = 512
ATTN_HEADS_PER_STEP = 2
ATTN_FAR_GROUP = 2
REL_BUCKETS = 32
REL_MAX_DIST = 128
PEER_HEADS = 8
PEER_N_KEYS = 128
PEER_TOPK = 16


def _cparams(*sem, flags=None):
    return pltpu.CompilerParams(dimension_semantics=sem, vmem_limit_bytes=VMEM_LIMIT_BYTES, flags=flags)


def _dot(a, b):
    return jnp.dot(a, b, preferred_element_type=F32)


def _dot_nt(a, b):
    return lax.dot_general(a, b, (((1,), (1,)), ((), ())), preferred_element_type=F32)


def _dot_tn(a, b):
    return lax.dot_general(a, b, (((0,), (0,)), ((), ())), preferred_element_type=F32)


def _gelu(x):
    return 0.5 * x * (1.0 + jnp.tanh(0.7978845608028654 * (x + 0.044715 * (x * x * x))))


def _norm_kernel(x_ref, g_ref, o_ref):
    x = x_ref[...]
    r = lax.rsqrt(jnp.mean(x * x, axis=-1, keepdims=True) + EPS)
    o_ref[...] = (x * r * g_ref[...]).astype(o_ref.dtype)


def _rmsnorm(x, g, bt=256):
    t, d = x.shape
    return pl.pallas_call(
        _norm_kernel,
        grid=(t // bt,),
        in_specs=[pl.BlockSpec((bt, d), lambda i: (i, 0)), pl.BlockSpec((1, d), lambda i: (0, 0))],
        out_specs=pl.BlockSpec((bt, d), lambda i: (i, 0)),
        out_shape=jax.ShapeDtypeStruct((t, d), BF16),
        compiler_params=_cparams("parallel"),
        name="rmsnorm",
    )(x, g.reshape(1, d).astype(F32))


def _addnorm_kernel(x_ref, pt_ref, g_ref, xo_ref, h_ref=None):
    x = x_ref[...] + pt_ref[...].T
    xo_ref[...] = x
    if h_ref is not None:
        r = lax.rsqrt(jnp.mean(x * x, axis=-1, keepdims=True) + EPS)
        h_ref[...] = (x * r * g_ref[...]).astype(h_ref.dtype)


def _add_norm(x, p_t, g=None, bt=256):
    t, d = x.shape
    with_norm = g is not None
    row = pl.BlockSpec((bt, d), lambda i: (i, 0))
    g = jnp.ones((d,), F32) if g is None else g
    out_specs = [row, row] if with_norm else [row]
    out_shape = [jax.ShapeDtypeStruct((t, d), F32)] + ([jax.ShapeDtypeStruct((t, d), BF16)] if with_norm else [])
    out = pl.pallas_call(
        _addnorm_kernel,
        grid=(t // bt,),
        in_specs=[row, pl.BlockSpec((d, bt), lambda i: (0, i)), pl.BlockSpec((1, d), lambda i: (0, 0))],
        out_specs=out_specs,
        out_shape=out_shape,
        compiler_params=_cparams("parallel"),
        name="add_norm" if with_norm else "add_update",
    )(x, p_t, g.reshape(1, d).astype(F32))
    return (out[0], out[1]) if with_norm else (out[0], None)


def _cast_kernel(w_ref, o_ref, *, transpose):
    w = w_ref[0]
    o_ref[...] = (w.T if transpose else w).astype(o_ref.dtype)


def _cast_bf16(w, layer, col0=0, ncols=None, transpose=False, br=512, bc=4096):
    _, r, c = w.shape
    ncols = c - col0 if ncols is None else ncols
    bc = math.gcd(math.gcd(bc, ncols), col0) if col0 else math.gcd(bc, ncols)
    br = min(br, r)
    off = col0 // bc
    if transpose:
        out_spec = pl.BlockSpec((bc, br), lambda i, j: (j, i))
        out_shape = jax.ShapeDtypeStruct((ncols, r), BF16)
    else:
        out_spec = pl.BlockSpec((br, bc), lambda i, j: (i, j))
        out_shape = jax.ShapeDtypeStruct((r, ncols), BF16)
    return pl.pallas_call(
        functools.partial(_cast_kernel, transpose=transpose),
        grid=(r // br, ncols // bc),
        in_specs=[pl.BlockSpec((1, br, bc), lambda i, j: (layer, i, off + j))],
        out_specs=out_spec,
        out_shape=out_shape,
        compiler_params=_cparams("parallel", "parallel"),
        name="cast_bf16_t" if transpose else "cast_bf16",
    )(w)


def _mm_kernel(*refs, n_lhs, has_resid, tile_major):
    a_refs, b_refs = refs[:n_lhs], refs[n_lhs:2 * n_lhs]
    r_ref = refs[2 * n_lhs] if has_resid else None
    o_ref = refs[-1]
    acc = _dot(a_refs[0][...], b_refs[0][...])
    for a_ref, b_ref in zip(a_refs[1:], b_refs[1:]):
        acc = acc + _dot(a_ref[...], b_ref[...])
    if has_resid:
        acc = acc + r_ref[...]
    if tile_major:
        for jj in range(o_ref.shape[0]):
            o_ref[jj] = acc[:, jj * LANES:(jj + 1) * LANES].astype(o_ref.dtype)
    else:
        o_ref[...] = acc.astype(o_ref.dtype)


def _matmul(lhs, rhs, *, resid=None, out_dtype=BF16, bm=512, bn=1024, tile_major=False, name="mm"):
    m, n = lhs[0].shape[0], rhs[0].shape[1]
    bm, bn = min(bm, m), min(bn, n)
    in_specs = [pl.BlockSpec((bm, a.shape[1]), lambda i, j: (i, 0)) for a in lhs]
    in_specs += [pl.BlockSpec((b.shape[0], bn), lambda i, j: (0, j)) for b in rhs]
    args = list(lhs) + list(rhs)
    if resid is not None:
        in_specs.append(pl.BlockSpec((bm, bn), lambda i, j: (i, j)))
        args.append(resid)
    if tile_major:
        out_spec = pl.BlockSpec((bn // LANES, bm, LANES), lambda i, j: (j, i, 0))
        out_shape = jax.ShapeDtypeStruct((n // LANES, m, LANES), out_dtype)
    else:
        out_spec = pl.BlockSpec((bm, bn), lambda i, j: (i, j))
        out_shape = jax.ShapeDtypeStruct((m, n), out_dtype)
    kern = functools.partial(_mm_kernel, n_lhs=len(lhs), has_resid=resid is not None, tile_major=tile_major)
    return pl.pallas_call(
        kern, grid=(m // bm, n // bn), in_specs=in_specs, out_specs=out_spec, out_shape=out_shape,
        compiler_params=_cparams("parallel", "parallel"), name=name,
    )(*args)


def _mm_nt_kernel(w_ref, h_ref, o_ref):
    o_ref[...] = _dot_nt(w_ref[...], h_ref[...]).astype(o_ref.dtype)


def _matmul_nt(w_t, h, *, bm=512, bn=1024, out_dtype=BF16, name="mm_nt"):
    n, k = w_t.shape
    m = h.shape[0]
    bm, bn = min(bm, m), min(bn, n)
    return pl.pallas_call(
        _mm_nt_kernel,
        grid=(m // bm, n // bn),
        in_specs=[pl.BlockSpec((bn, k), lambda i, j: (j, 0)), pl.BlockSpec((bm, k), lambda i, j: (i, 0))],
        out_specs=pl.BlockSpec((bn, bm), lambda i, j: (j, i)),
        out_shape=jax.ShapeDtypeStruct((n, m), out_dtype),
        compiler_params=_cparams("parallel", "parallel"),
        name=name,
    )(w_t, h)


def _s5_tables(log_step, a_re, a_im, b_re, b_im, c_re, c_im, d_skip):
    L = S5_CHUNK
    g, p = a_re.shape
    gt = LANES // S5_GROUP
    nt = g // gt
    hp = lax.Precision.HIGHEST
    step = jnp.exp(log_step)[:, None]
    lr, li = a_re, a_im
    mag = jnp.exp(lr * step)
    abar_re = mag * jnp.cos(li * step)
    abar_im = mag * jnp.sin(li * step)
    den = lr * lr + li * li
    num_re = abar_re - 1.0
    f_re = (num_re * lr + abar_im * li) / den
    f_im = (abar_im * lr - num_re * li) / den
    bb_re = f_re[..., None] * b_re - f_im[..., None] * b_im
    bb_im = f_re[..., None] * b_im + f_im[..., None] * b_re
    k = jnp.arange(L + 1, dtype=F32)[:, None, None]
    pmag = jnp.exp(k * (lr * step)[None])
    pang = k * (li * step)[None]
    pw_re = pmag * jnp.cos(pang)
    pw_im = pmag * jnp.sin(pang)
    ns = 2 * gt * p

    def same_group(row_g, col_g):
        return (row_g[:, None] == col_g[None, :]).astype(F32)

    cp_re = c_re[None] * pw_re[:L, :, None, :] - c_im[None] * pw_im[:L, :, None, :]
    cp_im = c_re[None] * pw_im[:L, :, None, :] + c_im[None] * pw_re[:L, :, None, :]
    kk = (jnp.einsum("dgop,gpi->dgoi", cp_re, bb_re, precision=hp)
          - jnp.einsum("dgop,gpi->dgoi", cp_im, bb_im, precision=hp))
    kki = kk.transpose(0, 1, 3, 2).reshape(L, nt, LANES, S5_GROUP)
    lane_g = jnp.arange(LANES) // S5_GROUP
    bd = jnp.tile(kki, (1, 1, 1, gt)) * same_group(lane_g, lane_g)
    bd = bd.transpose(1, 0, 2, 3).astype(BF16)

    k_rev = (L - 1) - k[:L]
    pmag_rev = jnp.exp(k_rev * (lr * step)[None])
    pang_rev = k_rev * (li * step)[None]
    pwr, pwi = pmag_rev * jnp.cos(pang_rev), pmag_rev * jnp.sin(pang_rev)
    pb_re = pwr[..., None] * bb_re[None] - pwi[..., None] * bb_im[None]
    pb_im = pwr[..., None] * bb_im[None] + pwi[..., None] * bb_re[None]
    pb = jnp.stack([pb_re, pb_im], 0).reshape(2, L, nt, gt, p, S5_GROUP)
    ps = pb.transpose(2, 1, 3, 5, 0, 4).reshape(nt, L * LANES, 2 * p).astype(BF16)
    rr, cc = jnp.arange(2 * p), jnp.arange(ns)
    ex_p = ((rr[:, None] // p == cc[None, :] // (gt * p)) & (rr[:, None] % p == cc[None, :] % p))
    row_g = (jnp.arange(L * LANES) % LANES) // S5_GROUP
    col_g = (cc % (gt * p)) // p
    p_op = (jnp.einsum("jrk,kc->jrc", ps, ex_p.astype(BF16), preferred_element_type=F32)
            * same_group(row_g, col_g)).astype(BF16)

    qr, qi = pw_re[1:L + 1], pw_im[1:L + 1]
    cn_re = c_re[None] * qr[:, :, None, :] - c_im[None] * qi[:, :, None, :]
    cn_im = c_re[None] * qi[:, :, None, :] + c_im[None] * qr[:, :, None, :]
    nb = jnp.stack([cn_re, -cn_im], 0).reshape(2, L, nt, gt, S5_GROUP, p)
    nsm = nb.transpose(2, 0, 3, 5, 1, 4).reshape(nt, ns, L * S5_GROUP).astype(BF16)
    rr, cc = jnp.arange(L * S5_GROUP), jnp.arange(L * LANES)
    ex_n = ((rr[:, None] // S5_GROUP == cc[None, :] // LANES)
            & (rr[:, None] % S5_GROUP == cc[None, :] % S5_GROUP))
    row_g = (jnp.arange(ns) % (gt * p)) // p
    col_g = (cc % LANES) // S5_GROUP
    n_op = (jnp.einsum("jrk,kc->jrc", nsm, ex_n.astype(BF16), preferred_element_type=F32)
            * same_group(row_g, col_g)).astype(BF16)

    a_l = jnp.stack([pw_re[L], pw_im[L]], 0).reshape(2, nt, gt * p).transpose(1, 0, 2)
    d_t = jnp.tile(d_skip.reshape(nt, 1, LANES), (1, 1, L))
    return bd, p_op, n_op, a_l, d_t


def _s5_kernel(x_ref, bd_ref, p_ref, n_ref, al_ref, d_ref, o_ref, m_ref, sloc_ref, sprev_ref):
    x = x_ref[0]
    nc, ns = sloc_ref.shape
    half = ns // 2
    L = bd_ref.shape[1]
    for tau in range(L):
        for t in range(L):
            blk = bd_ref[0, t - tau] if t >= tau else jnp.zeros((LANES, LANES), m_ref.dtype)
            m_ref[tau * LANES:(tau + 1) * LANES, t * LANES:(t + 1) * LANES] = blk
    sloc_ref[...] = _dot(x, p_ref[0])
    a_re = al_ref[0, 0:1, :]
    a_im = al_ref[0, 1:2, :]

    def body(c, carry):
        s_re, s_im = carry
        row = pl.ds(c, 1)
        sprev_ref[row, 0:half] = s_re
        sprev_ref[row, half:ns] = s_im
        l_re = sloc_ref[row, 0:half]
        l_im = sloc_ref[row, half:ns]
        return (a_re * s_re - a_im * s_im + l_re, a_re * s_im + a_im * s_re + l_im)

    zero = jnp.zeros((1, half), F32)
    lax.fori_loop(0, nc, body, (zero, zero))
    y = _dot(x, m_ref[...]) + _dot(sprev_ref[...].astype(BF16), n_ref[0])
    y = y + d_ref[0] * x.astype(F32)
    o_ref[0] = _gelu(y).astype(o_ref.dtype)


def _s5_core(u_t, tables):
    bd, p_op, n_op, a_l, d_t = tables
    nt, t, _ = u_t.shape
    L = S5_CHUNK
    nc, w, ns = t // L, L * LANES, p_op.shape[2]
    x = u_t.reshape(nt, nc, w)
    out = pl.pallas_call(
        _s5_kernel,
        grid=(nt,),
        in_specs=[
            pl.BlockSpec((1, nc, w), lambda j: (j, 0, 0)),
            pl.BlockSpec((1, L, LANES, LANES), lambda j: (j, 0, 0, 0)),
            pl.BlockSpec((1, w, ns), lambda j: (j, 0, 0)),
            pl.BlockSpec((1, ns, w), lambda j: (j, 0, 0)),
            pl.BlockSpec((1, 2, ns // 2), lambda j: (j, 0, 0)),
            pl.BlockSpec((1, 1, w), lambda j: (j, 0, 0)),
        ],
        out_specs=pl.BlockSpec((1, nc, w), lambda j: (j, 0, 0)),
        out_shape=jax.ShapeDtypeStruct((nt, nc, w), BF16),
        scratch_shapes=[pltpu.VMEM((w, w), BF16), pltpu.VMEM((nc, ns), F32), pltpu.VMEM((nc, ns), F32)],
        compiler_params=_cparams("parallel"),
        name="s5_core",
    )(x, bd, p_op, n_op, a_l, d_t)
    return out.reshape(nt, t, LANES)


def _glu_kernel(y_ref, w_ref, b_ref, o_ref):
    y = jnp.concatenate([y_ref[j] for j in range(y_ref.shape[0])], axis=1)
    acc = _dot(y, w_ref[...]) + b_ref[...]
    o_ref[...] = (y.astype(F32) * jax.nn.sigmoid(acc)).astype(o_ref.dtype)


def _glu(y_t, w, b, bm=512):
    nt, t, _ = y_t.shape
    width = nt * LANES
    return pl.pallas_call(
        _glu_kernel,
        grid=(t // bm,),
        in_specs=[
            pl.BlockSpec((nt, bm, LANES), lambda i: (0, i, 0)),
            pl.BlockSpec((width, width), lambda i: (0, 0)),
            pl.BlockSpec((1, width), lambda i: (0, 0)),
        ],
        out_specs=pl.BlockSpec((bm, width), lambda i: (i, 0)),
        out_shape=jax.ShapeDtypeStruct((t, width), BF16),
        compiler_params=_cparams("parallel"),
        name="s5_glu",
    )(y_t, w, b.reshape(1, width).astype(F32))


def _ret_kernel(q_ref, k_ref, v_ref, g_ref, cos_ref, sin_ref, intra_ref, qd_ref, kd_ref, cd_ref,
                gg_ref, gb_ref, o_ref, state_ref, *, chunk, scale):
    @pl.when(pl.program_id(1) == 0)
    def _():
        state_ref[...] = jnp.zeros_like(state_ref)

    intra, qd, kd, cd = intra_ref[0], qd_ref[0], kd_ref[0], cd_ref[0]
    gg, gb = gg_ref[...], gb_ref[...]
    half = cos_ref.shape[1]

    def body(c, carry):
        rows = pl.ds(pl.multiple_of(c * chunk, chunk), chunk)
        cos, sin = cos_ref[rows, :], sin_ref[rows, :]

        def rot(x):
            x1, x2 = x[:, :half], x[:, half:]
            return jnp.concatenate([x1 * cos - x2 * sin, x1 * sin + x2 * cos], axis=1)

        q = rot(q_ref[rows, :].astype(F32))
        k = rot(k_ref[rows, :].astype(F32)) * scale
        v = v_ref[rows, :]
        st = state_ref[...]
        scores = _dot_nt(q.astype(BF16), k.astype(BF16)) * intra
        out = _dot(scores.astype(BF16), v) + _dot((q * qd).astype(BF16), st.astype(BF16))
        state_ref[...] = cd * st + _dot_tn((k * kd).astype(BF16), v)
        mu = jnp.mean(out, axis=-1, keepdims=True)
        cen = out - mu
        var = jnp.mean(cen * cen, axis=-1, keepdims=True)
        o = cen * lax.rsqrt(var + EPS) * gg + gb
        gt = g_ref[rows, :].astype(F32)
        o_ref[rows, :] = (gt * jax.nn.sigmoid(gt) * o).astype(o_ref.dtype)
        return carry

    lax.fori_loop(0, q_ref.shape[0] // chunk, body, 0)


def _retention(z4, gn_g, gn_b, tb=2048):
    t = z4.shape[0]
    w = z4.shape[1] // 4
    hd = w // RET_HEADS
    half = hd // 2
    c = RET_CHUNK
    tb = min(tb, t)
    pos = jnp.arange(t, dtype=F32)
    freqs = ROPE_BASE ** (-jnp.arange(half, dtype=F32) / half)
    ang = pos[:, None] * freqs[None, :]
    cos, sin = jnp.cos(ang), jnp.sin(ang)
    gamma = 1.0 - 2.0 ** (-5.0 - jnp.arange(RET_HEADS, dtype=F32))
    log_g = jnp.log(gamma)
    idx = jnp.arange(c, dtype=F32)
    rel = idx[:, None] - idx[None, :]
    intra = jnp.where(rel >= 0, jnp.exp(log_g[:, None, None] * jnp.maximum(rel, 0.0)), 0.0)
    q_decay = jnp.exp(log_g[:, None] * (idx + 1.0))[..., None]
    k_decay = jnp.exp(log_g[:, None] * (c - 1.0 - idx))[..., None]
    chunk_decay = jnp.exp(log_g * c)[:, None, None]
    nh = RET_HEADS
    blk = lambda off: pl.BlockSpec((tb, hd), lambda h, s: (s, off + h))
    per_head = lambda shape: pl.BlockSpec((1,) + shape, lambda h, s: (h, 0, 0))
    return pl.pallas_call(
        functools.partial(_ret_kernel, chunk=c, scale=hd ** -0.5),
        grid=(nh, t // tb),
        in_specs=[
            blk(0), blk(nh), blk(2 * nh), blk(3 * nh),
            pl.BlockSpec((tb, half), lambda h, s: (s, 0)),
            pl.BlockSpec((tb, half), lambda h, s: (s, 0)),
            per_head((c, c)), per_head((c, 1)), per_head((c, 1)), per_head((1, 1)),
            pl.BlockSpec((1, hd), lambda h, s: (0, h)),
            pl.BlockSpec((1, hd), lambda h, s: (0, h)),
        ],
        out_specs=pl.BlockSpec((tb, hd), lambda h, s: (s, h)),
        out_shape=jax.ShapeDtypeStruct((t, w), BF16),
        scratch_shapes=[pltpu.VMEM((hd, hd), F32)],
        compiler_params=_cparams("parallel", "arbitrary"),
        name="retention",
    )(z4, z4, z4, z4, cos, sin, intra, q_decay, k_decay, chunk_decay,
      gn_g.reshape(1, w).astype(F32), gn_b.reshape(1, w).astype(F32))


def _qknorm_kernel(x_ref, gain_ref, ones_ref, o_ref, *, inv_dim, transpose_out):
    x = x_ref[...].astype(F32)
    xx = x * x
    hi = xx.astype(BF16)
    lo = (xx - hi.astype(F32)).astype(BF16)
    ss = _dot(hi, ones_ref[...]) + _dot(lo, ones_ref[...])
    y = x * lax.rsqrt(ss * inv_dim + EPS) * gain_ref[...]
    o_ref[...] = (y.T if transpose_out else y).astype(o_ref.dtype)


def _qknorm(z, gain, col_off, transpose_out, bt=1024, bw=256):
    t = z.shape[0]
    w = gain.shape[0]
    bt = min(bt, t)
    off = col_off // bw
    grp = jnp.arange(bw) // DIFF_HEAD_DIM
    ones = (grp[:, None] == grp[None, :]).astype(BF16)
    if transpose_out:
        out_spec = pl.BlockSpec((bw, bt), lambda i, j: (j, i))
        out_shape = jax.ShapeDtypeStruct((w, t), BF16)
    else:
        out_spec = pl.BlockSpec((bt, bw), lambda i, j: (i, j))
        out_shape = jax.ShapeDtypeStruct((t, w), BF16)
    return pl.pallas_call(
        functools.partial(_qknorm_kernel, inv_dim=1.0 / DIFF_HEAD_DIM, transpose_out=transpose_out),
        grid=(t // bt, w // bw),
        in_specs=[
            pl.BlockSpec((bt, bw), lambda i, j: (i, off + j)),
            pl.BlockSpec((1, bw), lambda i, j: (0, j)),
            pl.BlockSpec((bw, bw), lambda i, j: (0, 0)),
        ],
        out_specs=out_spec,
        out_shape=out_shape,
        compiler_params=_cparams("parallel", "parallel"),
        name="q_norm_t" if transpose_out else "k_norm",
    )(z, gain.reshape(1, w).astype(F32), ones)


def _attn_kernel(qt_ref, k_ref, vt_ref, d0_ref, d1_ref, lam_ref, sg_ref, o_ref, m_ref, acc_ref, va_ref,
                 *, blk, out_scale):
    i = pl.program_id(1)
    hd = DIFF_HEAD_DIM
    dv = 2 * hd
    nhs = va_ref.shape[0]

    @pl.when(i == 0)
    def _():
        for hh in range(nhs):
            for jb in range(va_ref.shape[1]):
                va_ref[hh, jb, 0:dv, :] = vt_ref[hh * dv:(hh + 1) * dv, jb * blk:(jb + 1) * blk]
                va_ref[hh, jb, dv:, :] = jnp.ones((va_ref.shape[2] - dv, blk), va_ref.dtype)

    m_ref[...] = jnp.full(m_ref.shape, -1e30, F32)
    acc_ref[...] = jnp.zeros_like(acc_ref)

    def step(j, bias_refs):
        nsub = len(bias_refs)
        rows = pl.ds(pl.multiple_of(j * blk, blk), nsub * blk)
        for hh in range(nhs):
            for mi in range(2):
                c = 2 * hh + mi
                s = _dot(k_ref[rows, c * hd:(c + 1) * hd], qt_ref[c * hd:(c + 1) * hd, :])
                if bias_refs[0] is not None:
                    s = s + jnp.concatenate([b[hh] for b in bias_refs], axis=0)
                m_old = m_ref[c]
                m_new = jnp.maximum(m_old, jnp.max(s, axis=0, keepdims=True))
                p = jnp.exp2(s - m_new).astype(BF16)
                alpha = jnp.exp2(m_old - m_new)
                pv = _dot(va_ref[hh, j], p[0:blk])
                for u in range(1, nsub):
                    pv = pv + _dot(va_ref[hh, j + u], p[u * blk:(u + 1) * blk])
                acc_ref[c] = alpha * acc_ref[c] + pv
                m_ref[c] = m_new

    n_far = jnp.maximum(i - 1, 0)
    grp = ATTN_FAR_GROUP

    def far_group(gi, carry):
        step(grp * gi, (None,) * grp)
        return carry

    def far_single(j, carry):
        step(j, (None,))
        return carry

    lax.fori_loop(0, n_far // grp, far_group, 0)
    lax.fori_loop((n_far // grp) * grp, n_far, far_single, 0)

    @pl.when(i >= 1)
    def _():
        step(i - 1, (d1_ref, d0_ref))

    @pl.when(i == 0)
    def _():
        step(0, (d0_ref,))
    for hh in range(nhs):
        a0, a1 = acc_ref[2 * hh], acc_ref[2 * hh + 1]
        o_t = a0[:dv] / a0[dv:dv + 1] - lam_ref[...] * (a1[:dv] / a1[dv:dv + 1])
        o = o_t.T
        ms = jnp.mean(o * o, axis=-1, keepdims=True)
        o_ref[:, hh * dv:(hh + 1) * dv] = (o * lax.rsqrt(ms + EPS) * sg_ref[...] * out_scale).astype(o_ref.dtype)


def _t5_bucket(n):
    max_exact = REL_BUCKETS // 2
    nf = jnp.maximum(n, 1).astype(F32)
    large = max_exact + (jnp.log(nf / max_exact) / math.log(REL_MAX_DIST / max_exact)
                         * (REL_BUCKETS - max_exact)).astype(jnp.int32)
    large = jnp.minimum(large, REL_BUCKETS - 1)
    return jnp.where(n < max_exact, n, large)


def _diff_attention(q_t, kn, v_t, rel_bias, lam, sub_g, lambda_init):
    t = kn.shape[0]
    nh = DIFF_HEADS
    dv = 2 * DIFF_HEAD_DIM
    ones_rows = 16
    blk = min(ATTN_BLOCK, t)
    nhs = ATTN_HEADS_PER_STEP
    assert blk >= REL_MAX_DIST
    key = jnp.arange(blk)[:, None]
    qry = jnp.arange(blk)[None, :]
    far_bias = rel_bias[REL_BUCKETS - 1]

    def bias_tile(rel):
        onehot = (_t5_bucket(rel).reshape(-1)[None, :] == jnp.arange(REL_BUCKETS)[:, None]).astype(F32)
        tile = jnp.dot((rel_bias - far_bias).T * LOG2E, onehot, precision=lax.Precision.HIGHEST)
        return tile.reshape(nh, blk, blk)

    rel0 = qry - key
    d0 = jnp.where((rel0 >= 0)[None], bias_tile(jnp.maximum(rel0, 0)), -jnp.inf)
    d1 = bias_tile(blk + qry - key)
    once = pl.Buffered(1)
    return pl.pallas_call(
        functools.partial(_attn_kernel, blk=blk, out_scale=1.0 - lambda_init),
        grid=(nh // nhs, t // blk),
        in_specs=[
            pl.BlockSpec((nhs * dv, blk), lambda h, i: (h, i)),
            pl.BlockSpec((t, nhs * dv), lambda h, i: (0, h), pipeline_mode=once),
            pl.BlockSpec((nhs * dv, t), lambda h, i: (h, 0), pipeline_mode=once),
            pl.BlockSpec((nhs, blk, blk), lambda h, i: (h, 0, 0), pipeline_mode=once),
            pl.BlockSpec((nhs, blk, blk), lambda h, i: (h, 0, 0), pipeline_mode=once),
            pl.BlockSpec((1, 1), lambda h, i: (0, 0)),
            pl.BlockSpec((1, dv), lambda h, i: (0, 0)),
        ],
        out_specs=pl.BlockSpec((blk, nhs * dv), lambda h, i: (i, h)),
        out_shape=jax.ShapeDtypeStruct((t, nh * dv), BF16),
        scratch_shapes=[pltpu.VMEM((2 * nhs, 1, blk), F32), pltpu.VMEM((2 * nhs, dv + ones_rows, blk), F32),
                        pltpu.VMEM((nhs, t // blk, dv + ones_rows, blk), BF16)],
        compiler_params=_cparams("parallel", "arbitrary"),
        name="diff_attn",
    )(q_t, kn, v_t, d0, d1, jnp.reshape(lam, (1, 1)).astype(F32), sub_g.reshape(1, dv).astype(F32))


def _pscore_kernel(w_ref, h_ref, sk_ref, o_ref):
    q_t = _dot_nt(w_ref[...], h_ref[...])
    for b in range(sk_ref.shape[0]):
        rows = slice(b * LANES, (b + 1) * LANES)
        o_ref[rows, :] = _dot(sk_ref[b], q_t[rows, :].astype(BF16))


def _peer_scores(hq, wq_t, sub_keys, tm=512, bn=512):
    t, d = hq.shape
    n = wq_t.shape[0]
    tm = min(tm, t)
    nb = bn // LANES
    return pl.pallas_call(
        _pscore_kernel,
        grid=(t // tm, n // bn),
        in_specs=[
            pl.BlockSpec((bn, d), lambda i, j: (j, 0)),
            pl.BlockSpec((tm, d), lambda i, j: (i, 0)),
            pl.BlockSpec((nb, LANES, LANES), lambda i, j: (j, 0, 0)),
        ],
        out_specs=pl.BlockSpec((bn, tm), lambda i, j: (j, i)),
        out_shape=jax.ShapeDtypeStruct((n, t), F32),
        compiler_params=_cparams("parallel", "parallel"),
        name="peer_scores",
    )(wq_t, hq, sub_keys)


def _top_values(s, n):
    vals = []
    for r in range(n):
        m = jnp.max(s, axis=0, keepdims=True)
        vals.append(m)
        if r < n - 1:
            s = jnp.where(s == m, -jnp.inf, s)
    return vals


def _route_kernel(s_ref, e1_ref, e2_ref, c_ref):
    nk = PEER_N_KEYS
    k = PEER_TOPK
    tl = s_ref.shape[1]
    pad = 24
    row = lax.broadcasted_iota(jnp.int32, (pad, tl), 0)
    hrow = lax.broadcasted_iota(jnp.int32, (PEER_HEADS, tl), 0)
    c_all = jnp.zeros((PEER_HEADS, tl), F32)
    for h in range(PEER_HEADS):
        s1 = s_ref[2 * h * nk:(2 * h + 1) * nk, :]
        s2 = s_ref[(2 * h + 1) * nk:(2 * h + 2) * nk, :]
        a = _top_values(s1, k + 1)
        b = _top_values(s2, k + 1)
        ea = [jnp.exp(x - a[0]) for x in a]
        eb = [jnp.exp(x - b[0]) for x in b]
        ea_m = jnp.full((pad, tl), -1.0, F32)
        eb_m = jnp.full((pad, tl), -1.0, F32)
        for r in range(k + 1):
            ea_m = jnp.where(row == r, ea[r], ea_m)
            eb_m = jnp.where(row == r, eb[r], eb_m)
        cand = jnp.concatenate(
            [ea[0] * eb_m, ea[1] * eb_m[:8], ea[2] * eb_m[:8], ea[3] * eb_m[:8],
             eb[0] * ea_m, eb[1] * ea_m[:8], eb[2] * ea_m[:8]], axis=0)
        v = _top_values(cand, k + 1)
        z = v[0]
        for r in range(1, k):
            z = z + v[r]
        rz = 1.0 / z
        e1_ref[h] = jnp.exp(s1 - a[0])
        e2_ref[h] = (jnp.exp(s2 - b[0]) * rz).astype(e2_ref.dtype)
        c_all = jnp.where(hrow == h, 0.5 * (v[k - 1] + v[k]) * rz, c_all)
    c_ref[...] = c_all


def _peer_route(s_t, tl=256):
    n, t = s_t.shape
    tl = min(tl, t)
    nh, nk = PEER_HEADS, PEER_N_KEYS
    return pl.pallas_call(
        _route_kernel,
        grid=(t // tl,),
        in_specs=[pl.BlockSpec((n, tl), lambda i: (0, i))],
        out_specs=[
            pl.BlockSpec((nh, nk, tl), lambda i: (0, 0, i)),
            pl.BlockSpec((nh, nk, tl), lambda i: (0, 0, i)),
            pl.BlockSpec((nh, tl), lambda i: (0, i)),
        ],
        out_shape=[
            jax.ShapeDtypeStruct((nh, nk, t), F32),
            jax.ShapeDtypeStruct((nh, nk, t), BF16),
            jax.ShapeDtypeStruct((nh, t), F32),
        ],
        compiler_params=_cparams("parallel"),
        name="peer_route",
    )(s_t)


def _peer_dense_kernel(u_ref, vt_ref, h_ref, e1_ref, e2_ref, c_ref, o_ref, w_ref):
    e = pl.program_id(1)

    @pl.when(e == 0)
    def _():
        o_ref[...] = jnp.zeros_like(o_ref)

    a_t = _dot_nt(u_ref[...], h_ref[...])
    tm = a_t.shape[1]
    nslab = a_t.shape[0] // LANES
    gdt = e2_ref.dtype
    for ii in range(nslab):
        rows = slice(ii * LANES, (ii + 1) * LANES)
        g = jnp.zeros((LANES, tm), gdt)
        for h in range(PEER_HEADS):
            vv = e2_ref[h] * e1_ref[h, pl.ds(e * nslab + ii, 1), :].astype(gdt)
            g = g + jnp.where(vv >= c_ref[h:h + 1, :].astype(gdt), vv, jnp.zeros_like(vv))
        w_ref[rows, :] = (g * _gelu(a_t[rows, :].astype(gdt))).astype(w_ref.dtype)
    o_ref[...] += _dot(vt_ref[...], w_ref[...])


def _peer_dense(hq, u_tab, v_tab_t, e1_t, e2, c, tm=512, te=512):
    t, d = hq.shape
    ne = u_tab.shape[0]
    tm = min(tm, t)
    nh, nk = PEER_HEADS, PEER_N_KEYS
    once = pl.Buffered(1)
    return pl.pallas_call(
        _peer_dense_kernel,
        grid=(t // tm, ne // te),
        in_specs=[
            pl.BlockSpec((te, d), lambda i, e: (e, 0)),
            pl.BlockSpec((d, te), lambda i, e: (0, e)),
            pl.BlockSpec((tm, d), lambda i, e: (i, 0), pipeline_mode=once),
            pl.BlockSpec((nh, nk, tm), lambda i, e: (0, 0, i), pipeline_mode=once),
            pl.BlockSpec((nh, nk, tm), lambda i, e: (0, 0, i), pipeline_mode=once),
            pl.BlockSpec((nh, tm), lambda i, e: (0, i)),
        ],
        out_specs=pl.BlockSpec((d, tm), lambda i, e: (0, i)),
        out_shape=jax.ShapeDtypeStruct((d, t), F32),
        scratch_shapes=[pltpu.VMEM((te, tm), BF16)],
        compiler_params=_cparams("parallel", "arbitrary"),
        name="peer_dense",
    )(u_tab, v_tab_t, hq, e1_t, e2, c)


def _peer(x, norm_g, w_qs, sub_keys, u_tabs, v_tabs, layer):
    hq = _rmsnorm(x, norm_g)
    nh, nk = PEER_HEADS, PEER_N_KEYS
    s_t = _peer_scores(hq, _cast_bf16(w_qs, layer, transpose=True), sub_keys.reshape(2 * nh, nk, -1).astype(BF16))
    e1, e2, c = _peer_route(s_t)
    return _peer_dense(hq, _cast_bf16(u_tabs, layer), _cast_bf16(v_tabs, layer, transpose=True), e1, e2, c)


def _even_layer(x, h, i, w_ins, log_step, a_re, a_im, b_re, b_im, c_re, c_im, d_skip,
                glu_ws, glu_b, gn_g, gn_b, w_outs):
    s5w = d_skip.shape[0]
    u_t = _matmul([h], [_cast_bf16(w_ins, i, 0, s5w)], tile_major=True, name="ev_in_s5")
    z4 = _matmul([h], [_cast_bf16(w_ins, i, s5w)], name="ev_in_ret")
    tables = _s5_tables(log_step, a_re, a_im, b_re, b_im, c_re, c_im, d_skip)
    ya = _glu(_s5_core(u_t, tables), _cast_bf16(glu_ws, i), glu_b)
    yb = _retention(z4, gn_g, gn_b)
    w_out = _cast_bf16(w_outs, i)
    return _matmul([ya, yb], [w_out[:s5w], w_out[s5w:]], resid=x, out_dtype=F32, name="ev_out")


def _odd_layer(x, h, i, w_ins, q_norm_g, k_norm_g, lq1, lk1, lq2, lk2, sub_norm_g, w_outs,
               rel_bias, lambda_init):
    d = x.shape[1]
    z = _matmul([h], [_cast_bf16(w_ins, i, 0, 2 * d)], name="od_in_qk")
    v_t = _matmul_nt(_cast_bf16(w_ins, i, 2 * d, transpose=True), h, name="od_in_vt")
    reps = d // DIFF_HEAD_DIM
    q_t = _qknorm(z, jnp.tile(q_norm_g, reps) * (DIFF_HEAD_DIM ** -0.5 * LOG2E), 0, True)
    kn = _qknorm(z, jnp.tile(k_norm_g, reps), d, False)
    lam = jnp.exp(jnp.sum(lq1 * lk1)) - jnp.exp(jnp.sum(lq2 * lk2)) + lambda_init
    o = _diff_attention(q_t, kn, v_t, rel_bias, lam, sub_norm_g, lambda_init)
    return _matmul([o], [_cast_bf16(w_outs, i)], resid=x, out_dtype=F32, name="od_out")


def kernel(x, ev_norm_g, ev_w_in, s5_log_step, s5_a_re, s5_a_im, s5_b_re, s5_b_im, s5_c_re, s5_c_im,
           s5_d, s5_glu_w, s5_glu_b, ret_gn_g, ret_gn_b, ev_w_out, od_norm_g, od_w_in, diff_q_norm_g,
           diff_k_norm_g, diff_lq1, diff_lk1, diff_lq2, diff_lk2, diff_sub_norm_g, od_w_out, rel_bias,
           ffn_norm_g, peer_w_q, peer_sub_keys, peer_u, peer_v):
    bsz, t, d = x.shape
    depth = ffn_norm_g.shape[0]
    mixer_norm_g = lambda layer: (ev_norm_g if layer % 2 == 0 else od_norm_g)[layer // 2]
    outs = []
    for b in range(bsz):
        xb = x[b]
        h = _rmsnorm(xb, mixer_norm_g(0))
        for layer in range(depth):
            i = layer // 2
            if layer % 2 == 0:
                xb = _even_layer(xb, h, i, ev_w_in, s5_log_step[i], s5_a_re[i], s5_a_im[i],
                                 s5_b_re[i], s5_b_im[i], s5_c_re[i], s5_c_im[i], s5_d[i], s5_glu_w,
                                 s5_glu_b[i], ret_gn_g[i], ret_gn_b[i], ev_w_out)
            else:
                lambda_init = 0.8 - 0.6 * math.exp(-0.3 * layer)
                xb = _odd_layer(xb, h, i, od_w_in, diff_q_norm_g[i], diff_k_norm_g[i],
                                diff_lq1[i], diff_lk1[i], diff_lq2[i], diff_lk2[i], diff_sub_norm_g[i],
                                od_w_out, rel_bias, lambda_init)
            p_t = _peer(xb, ffn_norm_g[layer], peer_w_q, peer_sub_keys[layer], peer_u, peer_v, layer)
            xb, h = _add_norm(xb, p_t, mixer_norm_g(layer + 1) if layer + 1 < depth else None)
        outs.append(xb)
    return jnp.stack(outs, 0)
```

```python
import functools
import math

import jax
import jax.numpy as jnp
from jax import lax
from jax.experimental import pallas as pl
from jax.experimental.pallas import tpu as pltpu

F32 = jnp.float32
BF16 = jnp.bfloat16

EPS = 1e-6
LOG2E = 1.4426950408889634
LANES = 128
VMEM_LIMIT_BYTES = 56 * 1024 * 1024

S5_GROUP = 16
S5_STATE = 64
S5_CHUNK = 16
RET_HEADS = 8
RET_CHUNK = 128
RET_HEADS_PER_STEP = 2
ROPE_BASE = 10000.0
DIFF_HEADS = 16
DIFF_HEAD_DIM = 128
ATTN_BLOCK = 512
ATTN_HEADS_PER_STEP = 2
ATTN_FAR_GROUPS = (4, 2, 1)
REL_BUCKETS = 32
REL_MAX_DIST = 128
PEER_HEADS = 8
PEER_N_KEYS = 128
PEER_TOPK = 16


def _cparams(*sem, flags=None):
    return pltpu.CompilerParams(dimension_semantics=sem, vmem_limit_bytes=VMEM_LIMIT_BYTES, flags=flags)


def _dot(a, b):
    return jnp.dot(a, b, preferred_element_type=F32)


def _dot_nt(a, b):
    return lax.dot_general(a, b, (((1,), (1,)), ((), ())), preferred_element_type=F32)


def _dot_tn(a, b):
    return lax.dot_general(a, b, (((0,), (0,)), ((), ())), preferred_element_type=F32)


def _gelu(x):
    return 0.5 * x * (1.0 + jnp.tanh(0.7978845608028654 * (x + 0.044715 * (x * x * x))))


def _norm_kernel(x_ref, g_ref, o_ref):
    x = x_ref[...]
    r = lax.rsqrt(jnp.mean(x * x, axis=-1, keepdims=True) + EPS)
    o_ref[...] = (x * r * g_ref[...]).astype(o_ref.dtype)


def _rmsnorm(x, g, bt=256):
    t, d = x.shape
    return pl.pallas_call(
        _norm_kernel,
        grid=(t // bt,),
        in_specs=[pl.BlockSpec((bt, d), lambda i: (i, 0)), pl.BlockSpec((1, d), lambda i: (0, 0))],
        out_specs=pl.BlockSpec((bt, d), lambda i: (i, 0)),
        out_shape=jax.ShapeDtypeStruct((t, d), BF16),
        compiler_params=_cparams("parallel"),
        name="rmsnorm",
    )(x, g.reshape(1, d).astype(F32))


def _addnorm_kernel(x_ref, pt_ref, g_ref, xo_ref, h_ref=None):
    x = x_ref[...] + pt_ref[...].T
    xo_ref[...] = x
    if h_ref is not None:
        r = lax.rsqrt(jnp.mean(x * x, axis=-1, keepdims=True) + EPS)
        h_ref[...] = (x * r * g_ref[...]).astype(h_ref.dtype)


def _add_norm(x, p_t, g=None, bt=256):
    t, d = x.shape
    with_norm = g is not None
    row = pl.BlockSpec((bt, d), lambda i: (i, 0))
    g = jnp.ones((d,), F32) if g is None else g
    out_specs = [row, row] if with_norm else [row]
    out_shape = [jax.ShapeDtypeStruct((t, d), F32)] + ([jax.ShapeDtypeStruct((t, d), BF16)] if with_norm else [])
    out = pl.pallas_call(
        _addnorm_kernel,
        grid=(t // bt,),
        in_specs=[row, pl.BlockSpec((d, bt), lambda i: (0, i)), pl.BlockSpec((1, d), lambda i: (0, 0))],
        out_specs=out_specs,
        out_shape=out_shape,
        compiler_params=_cparams("parallel"),
        name="add_norm" if with_norm else "add_update",
    )(x, p_t, g.reshape(1, d).astype(F32))
    return (out[0], out[1]) if with_norm else (out[0], None)


def _cast_kernel(w_ref, o_ref, *, transpose):
    w = w_ref[0]
    o_ref[...] = (w.T if transpose else w).astype(o_ref.dtype)


def _cast_bf16(w, layer, col0=0, ncols=None, transpose=False, br=512, bc=4096):
    _, r, c = w.shape
    ncols = c - col0 if ncols is None else ncols
    bc = math.gcd(math.gcd(bc, ncols), col0) if col0 else math.gcd(bc, ncols)
    br = min(br, r)
    off = col0 // bc
    if transpose:
        out_spec = pl.BlockSpec((bc, br), lambda i, j: (j, i))
        out_shape = jax.ShapeDtypeStruct((ncols, r), BF16)
    else:
        out_spec = pl.BlockSpec((br, bc), lambda i, j: (i, j))
        out_shape = jax.ShapeDtypeStruct((r, ncols), BF16)
    return pl.pallas_call(
        functools.partial(_cast_kernel, transpose=transpose),
        grid=(r // br, ncols // bc),
        in_specs=[pl.BlockSpec((1, br, bc), lambda i, j: (layer, i, off + j))],
        out_specs=out_spec,
        out_shape=out_shape,
        compiler_params=_cparams("parallel", "parallel"),
        name="cast_bf16_t" if transpose else "cast_bf16",
    )(w)


def _mm_kernel(*refs, n_lhs, has_resid, tile_major):
    a_refs, b_refs = refs[:n_lhs], refs[n_lhs:2 * n_lhs]
    r_ref = refs[2 * n_lhs] if has_resid else None
    o_ref = refs[-1]
    acc = _dot(a_refs[0][...], b_refs[0][...])
    for a_ref, b_ref in zip(a_refs[1:], b_refs[1:]):
        acc = acc + _dot(a_ref[...], b_ref[...])
    if has_resid:
        acc = acc + r_ref[...]
    if tile_major:
        for jj in range(o_ref.shape[0]):
            o_ref[jj] = acc[:, jj * LANES:(jj + 1) * LANES].astype(o_ref.dtype)
    else:
        o_ref[...] = acc.astype(o_ref.dtype)


def _matmul(lhs, rhs, *, resid=None, out_dtype=BF16, bm=512, bn=1024, tile_major=False, name="mm"):
    m, n = lhs[0].shape[0], rhs[0].shape[1]
    bm, bn = min(bm, m), min(bn, n)
    in_specs = [pl.BlockSpec((bm, a.shape[1]), lambda i, j: (i, 0)) for a in lhs]
    in_specs += [pl.BlockSpec((b.shape[0], bn), lambda i, j: (0, j)) for b in rhs]
    args = list(lhs) + list(rhs)
    if resid is not None:
        in_specs.append(pl.BlockSpec((bm, bn), lambda i, j: (i, j)))
        args.append(resid)
    if tile_major:
        out_spec = pl.BlockSpec((bn // LANES, bm, LANES), lambda i, j: (j, i, 0))
        out_shape = jax.ShapeDtypeStruct((n // LANES, m, LANES), out_dtype)
    else:
        out_spec = pl.BlockSpec((bm, bn), lambda i, j: (i, j))
        out_shape = jax.ShapeDtypeStruct((m, n), out_dtype)
    kern = functools.partial(_mm_kernel, n_lhs=len(lhs), has_resid=resid is not None, tile_major=tile_major)
    return pl.pallas_call(
        kern, grid=(m // bm, n // bn), in_specs=in_specs, out_specs=out_spec, out_shape=out_shape,
        compiler_params=_cparams("parallel", "parallel"), name=name,
    )(*args)


def _mm_nt_kernel(w_ref, h_ref, o_ref):
    o_ref[...] = _dot_nt(w_ref[...], h_ref[...]).astype(o_ref.dtype)


def _matmul_nt(w_t, h, *, bm=512, bn=1024, out_dtype=BF16, name="mm_nt"):
    n, k = w_t.shape
    m = h.shape[0]
    bm, bn = min(bm, m), min(bn, n)
    return pl.pallas_call(
        _mm_nt_kernel,
        grid=(m // bm, n // bn),
        in_specs=[pl.BlockSpec((bn, k), lambda i, j: (j, 0)), pl.BlockSpec((bm, k), lambda i, j: (i, 0))],
        out_specs=pl.BlockSpec((bn, bm), lambda i, j: (j, i)),
        out_shape=jax.ShapeDtypeStruct((n, m), out_dtype),
        compiler_params=_cparams("parallel", "parallel"),
        name=name,
    )(w_t, h)


def _s5_tables(log_step, a_re, a_im, b_re, b_im, c_re, c_im, d_skip):
    L = S5_CHUNK
    g, p = a_re.shape
    gt = LANES // S5_GROUP
    nt = g // gt
    hp = lax.Precision.HIGHEST
    step = jnp.exp(log_step)[:, None]
    lr, li = a_re, a_im
    mag = jnp.exp(lr * step)
    abar_re = mag * jnp.cos(li * step)
    abar_im = mag * jnp.sin(li * step)
    den = lr * lr + li * li
    num_re = abar_re - 1.0
    f_re = (num_re * lr + abar_im * li) / den
    f_im = (abar_im * lr - num_re * li) / den
    bb_re = f_re[..., None] * b_re - f_im[..., None] * b_im
    bb_im = f_re[..., None] * b_im + f_im[..., None] * b_re
    k = jnp.arange(L + 1, dtype=F32)[:, None, None]
    pmag = jnp.exp(k * (lr * step)[None])
    pang = k * (li * step)[None]
    pw_re = pmag * jnp.cos(pang)
    pw_im = pmag * jnp.sin(pang)
    ns = 2 * gt * p

    def same_group(row_g, col_g):
        return (row_g[:, None] == col_g[None, :]).astype(F32)

    cp_re = c_re[None] * pw_re[:L, :, None, :] - c_im[None] * pw_im[:L, :, None, :]
    cp_im = c_re[None] * pw_im[:L, :, None, :] + c_im[None] * pw_re[:L, :, None, :]
    kk = (jnp.einsum("dgop,gpi->dgoi", cp_re, bb_re, precision=hp)
          - jnp.einsum("dgop,gpi->dgoi", cp_im, bb_im, precision=hp))
    kki = kk.transpose(0, 1, 3, 2).reshape(L, nt, LANES, S5_GROUP)
    lane_g = jnp.arange(LANES) // S5_GROUP
    bd = jnp.tile(kki, (1, 1, 1, gt)) * same_group(lane_g, lane_g)
    bd = bd.transpose(1, 0, 2, 3).astype(BF16)

    k_rev = (L - 1) - k[:L]
    pmag_rev = jnp.exp(k_rev * (lr * step)[None])
    pang_rev = k_rev * (li * step)[None]
    pwr, pwi = pmag_rev * jnp.cos(pang_rev), pmag_rev * jnp.sin(pang_rev)
    pb_re = pwr[..., None] * bb_re[None] - pwi[..., None] * bb_im[None]
    pb_im = pwr[..., None] * bb_im[None] + pwi[..., None] * bb_re[None]
    pb = jnp.stack([pb_re, pb_im], 0).reshape(2, L, nt, gt, p, S5_GROUP)
    ps = pb.transpose(2, 1, 3, 5, 0, 4).reshape(nt, L * LANES, 2 * p).astype(BF16)
    rr, cc = jnp.arange(2 * p), jnp.arange(ns)
    ex_p = ((rr[:, None] // p == cc[None, :] // (gt * p)) & (rr[:, None] % p == cc[None, :] % p))
    row_g = (jnp.arange(L * LANES) % LANES) // S5_GROUP
    col_g = (cc % (gt * p)) // p
    p_op = (jnp.einsum("jrk,kc->jrc", ps, ex_p.astype(BF16), preferred_element_type=BF16)
            * same_group(row_g, col_g).astype(BF16))

    qr, qi = pw_re[1:L + 1], pw_im[1:L + 1]
    cn_re = c_re[None] * qr[:, :, None, :] - c_im[None] * qi[:, :, None, :]
    cn_im = c_re[None] * qi[:, :, None, :] + c_im[None] * qr[:, :, None, :]
    nb = jnp.stack([cn_re, -cn_im], 0).reshape(2, L, nt, gt, S5_GROUP, p)
    nsm = nb.transpose(2, 0, 3, 5, 1, 4).reshape(nt, ns, L * S5_GROUP).astype(BF16)
    rr, cc = jnp.arange(L * S5_GROUP), jnp.arange(L * LANES)
    ex_n = ((rr[:, None] // S5_GROUP == cc[None, :] // LANES)
            & (rr[:, None] % S5_GROUP == cc[None, :] % S5_GROUP))
    row_g = (jnp.arange(ns) % (gt * p)) // p
    col_g = (cc % LANES) // S5_GROUP
    n_op = (jnp.einsum("jrk,kc->jrc", nsm, ex_n.astype(BF16), preferred_element_type=BF16)
            * same_group(row_g, col_g).astype(BF16))

    a_l = jnp.stack([pw_re[L], pw_im[L]], 0).reshape(2, nt, gt * p).transpose(1, 0, 2)
    d_t = jnp.tile(d_skip.reshape(nt, 1, LANES), (1, 1, L))
    return bd, p_op, n_op, a_l, d_t


def _s5_kernel(x_ref, bd_ref, p_ref, n_ref, al_ref, d_ref, o_ref, m_ref, sloc_ref, sprev_ref):
    x = x_ref[0]
    nc, ns = sloc_ref.shape
    half = ns // 2
    L = bd_ref.shape[1]
    for tau in range(L):
        for t in range(L):
            blk = bd_ref[0, t - tau] if t >= tau else jnp.zeros((LANES, LANES), m_ref.dtype)
            m_ref[tau * LANES:(tau + 1) * LANES, t * LANES:(t + 1) * LANES] = blk
    sloc_ref[...] = _dot(x, p_ref[0])
    a_re = al_ref[0, 0:1, :]
    a_im = al_ref[0, 1:2, :]

    def body(c, carry):
        s_re, s_im = carry
        row = pl.ds(c, 1)
        sprev_ref[row, 0:half] = s_re
        sprev_ref[row, half:ns] = s_im
        l_re = sloc_ref[row, 0:half]
        l_im = sloc_ref[row, half:ns]
        return (a_re * s_re - a_im * s_im + l_re, a_re * s_im + a_im * s_re + l_im)

    zero = jnp.zeros((1, half), F32)
    lax.fori_loop(0, nc, body, (zero, zero))
    y = _dot(x, m_ref[...]) + _dot(sprev_ref[...].astype(BF16), n_ref[0])
    y = y + d_ref[0] * x.astype(F32)
    o_ref[0] = _gelu(y).astype(o_ref.dtype)


def _s5_core(u_t, tables):
    bd, p_op, n_op, a_l, d_t = tables
    nt, t, _ = u_t.shape
    L = S5_CHUNK
    nc, w, ns = t // L, L * LANES, p_op.shape[2]
    x = u_t.reshape(nt, nc, w)
    out = pl.pallas_call(
        _s5_kernel,
        grid=(nt,),
        in_specs=[
            pl.BlockSpec((1, nc, w), lambda j: (j, 0, 0)),
            pl.BlockSpec((1, L, LANES, LANES), lambda j: (j, 0, 0, 0)),
            pl.BlockSpec((1, w, ns), lambda j: (j, 0, 0)),
            pl.BlockSpec((1, ns, w), lambda j: (j, 0, 0)),
            pl.BlockSpec((1, 2, ns // 2), lambda j: (j, 0, 0)),
            pl.BlockSpec((1, 1, w), lambda j: (j, 0, 0)),
        ],
        out_specs=pl.BlockSpec((1, nc, w), lambda j: (j, 0, 0)),
        out_shape=jax.ShapeDtypeStruct((nt, nc, w), BF16),
        scratch_shapes=[pltpu.VMEM((w, w), BF16), pltpu.VMEM((nc, ns), F32), pltpu.VMEM((nc, ns), F32)],
        compiler_params=_cparams("parallel"),
        name="s5_core",
    )(x, bd, p_op, n_op, a_l, d_t)
    return out.reshape(nt, t, LANES)


def _glu_kernel(y_ref, w_ref, b_ref, o_ref):
    y = jnp.concatenate([y_ref[j] for j in range(y_ref.shape[0])], axis=1)
    acc = _dot(y, w_ref[...]) + b_ref[...]
    o_ref[...] = (y.astype(F32) * jax.nn.sigmoid(acc)).astype(o_ref.dtype)


def _glu(y_t, w, b, bm=512):
    nt, t, _ = y_t.shape
    width = nt * LANES
    return pl.pallas_call(
        _glu_kernel,
        grid=(t // bm,),
        in_specs=[
            pl.BlockSpec((nt, bm, LANES), lambda i: (0, i, 0)),
            pl.BlockSpec((width, width), lambda i: (0, 0)),
            pl.BlockSpec((1, width), lambda i: (0, 0)),
        ],
        out_specs=pl.BlockSpec((bm, width), lambda i: (i, 0)),
        out_shape=jax.ShapeDtypeStruct((t, width), BF16),
        compiler_params=_cparams("parallel"),
        name="s5_glu",
    )(y_t, w, b.reshape(1, width).astype(F32))


def _ret_kernel(q_ref, k_ref, v_ref, g_ref, cos_ref, sin_ref, intra_ref, qd_ref, kd_ref, cd_ref,
                gg_ref, gb_ref, o_ref, state_ref, *, chunk, scale):
    @pl.when(pl.program_id(1) == 0)
    def _():
        state_ref[...] = jnp.zeros_like(state_ref)

    half = cos_ref.shape[1]
    hd = 2 * half
    nhs = state_ref.shape[0]

    def body(c, carry):
        rows = pl.ds(pl.multiple_of(c * chunk, chunk), chunk)
        cos, sin = cos_ref[rows, :], sin_ref[rows, :]

        def rot(x):
            x1, x2 = x[:, :half], x[:, half:]
            return jnp.concatenate([x1 * cos - x2 * sin, x1 * sin + x2 * cos], axis=1)

        for hh in range(nhs):
            cols = slice(hh * hd, (hh + 1) * hd)
            q = rot(q_ref[rows, cols].astype(F32))
            k = rot(k_ref[rows, cols].astype(F32)) * scale
            v = v_ref[rows, cols]
            st = state_ref[hh]
            scores = _dot_nt(q.astype(BF16), k.astype(BF16)) * intra_ref[hh]
            out = _dot(scores.astype(BF16), v) + _dot((q * qd_ref[hh]).astype(BF16), st.astype(BF16))
            state_ref[hh] = cd_ref[hh] * st + _dot_tn((k * kd_ref[hh]).astype(BF16), v)
            mu = jnp.mean(out, axis=-1, keepdims=True)
            cen = out - mu
            var = jnp.mean(cen * cen, axis=-1, keepdims=True)
            o = cen * lax.rsqrt(var + EPS) * gg_ref[:, cols] + gb_ref[:, cols]
            gt = g_ref[rows, cols].astype(F32)
            o_ref[rows, cols] = (gt * jax.nn.sigmoid(gt) * o).astype(o_ref.dtype)
        return carry

    lax.fori_loop(0, q_ref.shape[0] // chunk, body, 0)


def _retention(z4, gn_g, gn_b, tb=2048):
    t = z4.shape[0]
    w = z4.shape[1] // 4
    hd = w // RET_HEADS
    half = hd // 2
    c = RET_CHUNK
    tb = min(tb, t)
    pos = jnp.arange(t, dtype=F32)
    freqs = ROPE_BASE ** (-jnp.arange(half, dtype=F32) / half)
    ang = pos[:, None] * freqs[None, :]
    cos, sin = jnp.cos(ang), jnp.sin(ang)
    gamma = 1.0 - 2.0 ** (-5.0 - jnp.arange(RET_HEADS, dtype=F32))
    log_g = jnp.log(gamma)
    idx = jnp.arange(c, dtype=F32)
    rel = idx[:, None] - idx[None, :]
    intra = jnp.where(rel >= 0, jnp.exp(log_g[:, None, None] * jnp.maximum(rel, 0.0)), 0.0)
    q_decay = jnp.exp(log_g[:, None] * (idx + 1.0))[..., None]
    k_decay = jnp.exp(log_g[:, None] * (c - 1.0 - idx))[..., None]
    chunk_decay = jnp.exp(log_g * c)[:, None, None]
    nhs = RET_HEADS_PER_STEP
    ng = RET_HEADS // nhs
    blk = lambda off: pl.BlockSpec((tb, nhs * hd), lambda h, s: (s, off + h))
    per_head = lambda shape: pl.BlockSpec((nhs,) + shape, lambda h, s: (h, 0, 0))
    return pl.pallas_call(
        functools.partial(_ret_kernel, chunk=c, scale=hd ** -0.5),
        grid=(ng, t // tb),
        in_specs=[
            blk(0), blk(ng), blk(2 * ng), blk(3 * ng),
            pl.BlockSpec((tb, half), lambda h, s: (s, 0)),
            pl.BlockSpec((tb, half), lambda h, s: (s, 0)),
            per_head((c, c)), per_head((c, 1)), per_head((c, 1)), per_head((1, 1)),
            pl.BlockSpec((1, nhs * hd), lambda h, s: (0, h)),
            pl.BlockSpec((1, nhs * hd), lambda h, s: (0, h)),
        ],
        out_specs=pl.BlockSpec((tb, nhs * hd), lambda h, s: (s, h)),
        out_shape=jax.ShapeDtypeStruct((t, w), BF16),
        scratch_shapes=[pltpu.VMEM((nhs, hd, hd), F32)],
        compiler_params=_cparams("parallel", "arbitrary"),
        name="retention",
    )(z4, z4, z4, z4, cos, sin, intra, q_decay, k_decay, chunk_decay,
      gn_g.reshape(1, w).astype(F32), gn_b.reshape(1, w).astype(F32))


def _qknorm_kernel(x_ref, gain_ref, ones_ref, o_ref, *, inv_dim, transpose_out):
    x = x_ref[...].astype(F32)
    xx = x * x
    hi = xx.astype(BF16)
    lo = (xx - hi.astype(F32)).astype(BF16)
    ss = _dot(hi, ones_ref[...]) + _dot(lo, ones_ref[...])
    y = x * lax.rsqrt(ss * inv_dim + EPS) * gain_ref[...]
    o_ref[...] = (y.T if transpose_out else y).astype(o_ref.dtype)


def _qknorm(z, gain, col_off, transpose_out, bt=1024, bw=256):
    t = z.shape[0]
    w = gain.shape[0]
    bt = min(bt, t)
    off = col_off // bw
    grp = jnp.arange(bw) // DIFF_HEAD_DIM
    ones = (grp[:, None] == grp[None, :]).astype(BF16)
    if transpose_out:
        out_spec = pl.BlockSpec((bw, bt), lambda i, j: (j, i))
        out_shape = jax.ShapeDtypeStruct((w, t), BF16)
    else:
        out_spec = pl.BlockSpec((bt, bw), lambda i, j: (i, j))
        out_shape = jax.ShapeDtypeStruct((t, w), BF16)
    return pl.pallas_call(
        functools.partial(_qknorm_kernel, inv_dim=1.0 / DIFF_HEAD_DIM, transpose_out=transpose_out),
        grid=(t // bt, w // bw),
        in_specs=[
            pl.BlockSpec((bt, bw), lambda i, j: (i, off + j)),
            pl.BlockSpec((1, bw), lambda i, j: (0, j)),
            pl.BlockSpec((bw, bw), lambda i, j: (0, 0)),
        ],
        out_specs=out_spec,
        out_shape=out_shape,
        compiler_params=_cparams("parallel", "parallel"),
        name="q_norm_t" if transpose_out else "k_norm",
    )(z, gain.reshape(1, w).astype(F32), ones)


def _attn_kernel(qt_ref, k_ref, vt_ref, d0_ref, d1_ref, lam_ref, sg_ref, o_ref, m_ref, acc_ref, va_ref,
                 *, blk, out_scale):
    i = pl.program_id(1)
    hd = DIFF_HEAD_DIM
    dv = 2 * hd
    nhs = va_ref.shape[0]

    @pl.when(i == 0)
    def _():
        for hh in range(nhs):
            for jb in range(va_ref.shape[1]):
                va_ref[hh, jb, 0:dv, :] = vt_ref[hh * dv:(hh + 1) * dv, jb * blk:(jb + 1) * blk]
                va_ref[hh, jb, dv:, :] = jnp.ones((va_ref.shape[2] - dv, blk), va_ref.dtype)

    m_ref[...] = jnp.full(m_ref.shape, -1e30, F32)
    acc_ref[...] = jnp.zeros_like(acc_ref)

    def step(j, bias_refs):
        nsub = len(bias_refs)
        rows = pl.ds(pl.multiple_of(j * blk, blk), nsub * blk)
        for hh in range(nhs):
            for mi in range(2):
                c = 2 * hh + mi
                s = _dot(k_ref[rows, c * hd:(c + 1) * hd], qt_ref[c * hd:(c + 1) * hd, :])
                if bias_refs[0] is not None:
                    s = s + jnp.concatenate([b[hh] for b in bias_refs], axis=0)
                m_old = m_ref[c]
                m_new = jnp.maximum(m_old, jnp.max(s, axis=0, keepdims=True))
                p = jnp.exp2(s - m_new).astype(BF16)
                alpha = jnp.exp2(m_old - m_new)
                pv = _dot(va_ref[hh, j], p[0:blk])
                for u in range(1, nsub):
                    pv = pv + _dot(va_ref[hh, j + u], p[u * blk:(u + 1) * blk])
                acc_ref[c] = alpha * acc_ref[c] + pv
                m_ref[c] = m_new

    n_far = jnp.maximum(i - 1, 0)
    done = 0
    for grp in ATTN_FAR_GROUPS:
        n_grp = (n_far - done) // grp

        def far_group(gi, carry, grp=grp, done=done):
            step(done + grp * gi, (None,) * grp)
            return carry

        lax.fori_loop(0, n_grp, far_group, 0)
        done = done + n_grp * grp

    @pl.when(i >= 1)
    def _():
        step(i - 1, (d1_ref, d0_ref))

    @pl.when(i == 0)
    def _():
        step(0, (d0_ref,))
    for hh in range(nhs):
        a0, a1 = acc_ref[2 * hh], acc_ref[2 * hh + 1]
        o_t = a0[:dv] / a0[dv:dv + 1] - lam_ref[...] * (a1[:dv] / a1[dv:dv + 1])
        o = o_t.T
        ms = jnp.mean(o * o, axis=-1, keepdims=True)
        o_ref[:, hh * dv:(hh + 1) * dv] = (o * lax.rsqrt(ms + EPS) * sg_ref[...] * out_scale).astype(o_ref.dtype)


def _t5_bucket(n):
    max_exact = REL_BUCKETS // 2
    nf = jnp.maximum(n, 1).astype(F32)
    large = max_exact + (jnp.log(nf / max_exact) / math.log(REL_MAX_DIST / max_exact)
                         * (REL_BUCKETS - max_exact)).astype(jnp.int32)
    large = jnp.minimum(large, REL_BUCKETS - 1)
    return jnp.where(n < max_exact, n, large)


def _diff_attention(q_t, kn, v_t, rel_bias, lam, sub_g, lambda_init):
    t = kn.shape[0]
    nh = DIFF_HEADS
    dv = 2 * DIFF_HEAD_DIM
    ones_rows = 16
    blk = min(ATTN_BLOCK, t)
    nhs = ATTN_HEADS_PER_STEP
    assert blk >= REL_MAX_DIST
    key = jnp.arange(blk)[:, None]
    qry = jnp.arange(blk)[None, :]
    far_bias = rel_bias[REL_BUCKETS - 1]

    def bias_tile(rel):
        onehot = (_t5_bucket(rel).reshape(-1)[None, :] == jnp.arange(REL_BUCKETS)[:, None]).astype(F32)
        tile = jnp.dot((rel_bias - far_bias).T * LOG2E, onehot, precision=lax.Precision.HIGHEST)
        return tile.reshape(nh, blk, blk)

    rel0 = qry - key
    d0 = jnp.where((rel0 >= 0)[None], bias_tile(jnp.maximum(rel0, 0)), -jnp.inf)
    d1 = bias_tile(blk + qry - key)
    once = pl.Buffered(1)
    return pl.pallas_call(
        functools.partial(_attn_kernel, blk=blk, out_scale=1.0 - lambda_init),
        grid=(nh // nhs, t // blk),
        in_specs=[
            pl.BlockSpec((nhs * dv, blk), lambda h, i: (h, i)),
            pl.BlockSpec((t, nhs * dv), lambda h, i: (0, h), pipeline_mode=once),
            pl.BlockSpec((nhs * dv, t), lambda h, i: (h, 0), pipeline_mode=once),
            pl.BlockSpec((nhs, blk, blk), lambda h, i: (h, 0, 0), pipeline_mode=once),
            pl.BlockSpec((nhs, blk, blk), lambda h, i: (h, 0, 0), pipeline_mode=once),
            pl.BlockSpec((1, 1), lambda h, i: (0, 0)),
            pl.BlockSpec((1, dv), lambda h, i: (0, 0)),
        ],
        out_specs=pl.BlockSpec((blk, nhs * dv), lambda h, i: (i, h)),
        out_shape=jax.ShapeDtypeStruct((t, nh * dv), BF16),
        scratch_shapes=[pltpu.VMEM((2 * nhs, 1, blk), F32), pltpu.VMEM((2 * nhs, dv + ones_rows, blk), F32),
                        pltpu.VMEM((nhs, t // blk, dv + ones_rows, blk), BF16)],
        compiler_params=_cparams("parallel", "arbitrary"),
        name="diff_attn",
    )(q_t, kn, v_t, d0, d1, jnp.reshape(lam, (1, 1)).astype(F32), sub_g.reshape(1, dv).astype(F32))


def _pscore_kernel(w_ref, h_ref, sk_ref, o_ref):
    q_t = _dot_nt(w_ref[...], h_ref[...])
    for b in range(sk_ref.shape[0]):
        rows = slice(b * LANES, (b + 1) * LANES)
        o_ref[rows, :] = _dot(sk_ref[b], q_t[rows, :].astype(BF16))


def _peer_scores(hq, wq_t, sub_keys, tm=512, bn=512):
    t, d = hq.shape
    n = wq_t.shape[0]
    tm = min(tm, t)
    nb = bn // LANES
    return pl.pallas_call(
        _pscore_kernel,
        grid=(t // tm, n // bn),
        in_specs=[
            pl.BlockSpec((bn, d), lambda i, j: (j, 0)),
            pl.BlockSpec((tm, d), lambda i, j: (i, 0)),
            pl.BlockSpec((nb, LANES, LANES), lambda i, j: (j, 0, 0)),
        ],
        out_specs=pl.BlockSpec((bn, tm), lambda i, j: (j, i)),
        out_shape=jax.ShapeDtypeStruct((n, t), F32),
        compiler_params=_cparams("parallel", "parallel"),
        name="peer_scores",
    )(wq_t, hq, sub_keys)


def _top_values(s, n):
    vals = []
    for r in range(n):
        m = jnp.max(s, axis=0, keepdims=True)
        vals.append(m)
        if r < n - 1:
            s = jnp.where(s == m, -jnp.inf, s)
    return vals


def _route_kernel(s_ref, e1_ref, e2_ref, c_ref):
    nk = PEER_N_KEYS
    k = PEER_TOPK
    tl = s_ref.shape[1]
    pad = 24
    row = lax.broadcasted_iota(jnp.int32, (pad, tl), 0)
    hrow = lax.broadcasted_iota(jnp.int32, (PEER_HEADS, tl), 0)
    c_all = jnp.zeros((PEER_HEADS, tl), F32)
    for h in range(PEER_HEADS):
        s1 = s_ref[2 * h * nk:(2 * h + 1) * nk, :]
        s2 = s_ref[(2 * h + 1) * nk:(2 * h + 2) * nk, :]
        a = _top_values(s1, k + 1)
        b = _top_values(s2, k + 1)
        ea = [jnp.exp(x - a[0]) for x in a]
        eb = [jnp.exp(x - b[0]) for x in b]
        ea_m = jnp.full((pad, tl), -1.0, F32)
        eb_m = jnp.full((pad, tl), -1.0, F32)
        for r in range(k + 1):
            ea_m = jnp.where(row == r, ea[r], ea_m)
            eb_m = jnp.where(row == r, eb[r], eb_m)
        cand = jnp.concatenate(
            [ea[0] * eb_m, ea[1] * eb_m[:8], ea[2] * eb_m[:8], ea[3] * eb_m[:8],
             eb[0] * ea_m, eb[1] * ea_m[:8], eb[2] * ea_m[:8]], axis=0)
        v = _top_values(cand, k + 1)
        z = v[0]
        for r in range(1, k):
            z = z + v[r]
        rz = 1.0 / z
        e1_ref[h] = jnp.exp(s1 - a[0])
        e2_ref[h] = (jnp.exp(s2 - b[0]) * rz).astype(e2_ref.dtype)
        c_all = jnp.where(hrow == h, 0.5 * (v[k - 1] + v[k]) * rz, c_all)
    c_ref[...] = c_all


def _peer_route(s_t, tl=256):
    n, t = s_t.shape
    tl = min(tl, t)
    nh, nk = PEER_HEADS, PEER_N_KEYS
    return pl.pallas_call(
        _route_kernel,
        grid=(t // tl,),
        in_specs=[pl.BlockSpec((n, tl), lambda i: (0, i))],
        out_specs=[
            pl.BlockSpec((nh, nk, tl), lambda i: (0, 0, i)),
            pl.BlockSpec((nh, nk, tl), lambda i: (0, 0, i)),
            pl.BlockSpec((nh, tl), lambda i: (0, i)),
        ],
        out_shape=[
            jax.ShapeDtypeStruct((nh, nk, t), F32),
            jax.ShapeDtypeStruct((nh, nk, t), BF16),
            jax.ShapeDtypeStruct((nh, t), F32),
        ],
        compiler_params=_cparams("parallel"),
        name="peer_route",
    )(s_t)


def _peer_dense_kernel(u_ref, vt_ref, h_ref, e1_ref, e2_ref, c_ref, o_ref, w_ref):
    e = pl.program_id(1)

    @pl.when(e == 0)
    def _():
        o_ref[...] = jnp.zeros_like(o_ref)

    a_t = _dot_nt(u_ref[...], h_ref[...])
    tm = a_t.shape[1]
    nslab = a_t.shape[0] // LANES
    gdt = e2_ref.dtype
    for ii in range(nslab):
        rows = slice(ii * LANES, (ii + 1) * LANES)
        g = jnp.zeros((LANES, tm), gdt)
        for h in range(PEER_HEADS):
            vv = e2_ref[h] * e1_ref[h, pl.ds(e * nslab + ii, 1), :].astype(gdt)
            g = g + jnp.where(vv >= c_ref[h:h + 1, :].astype(gdt), vv, jnp.zeros_like(vv))
        w_ref[rows, :] = (g * _gelu(a_t[rows, :].astype(gdt))).astype(w_ref.dtype)
    o_ref[...] += _dot(vt_ref[...], w_ref[...])


def _peer_dense(hq, u_tab, v_tab_t, e1_t, e2, c, tm=512, te=512):
    t, d = hq.shape
    ne = u_tab.shape[0]
    tm = min(tm, t)
    nh, nk = PEER_HEADS, PEER_N_KEYS
    once = pl.Buffered(1)
    return pl.pallas_call(
        _peer_dense_kernel,
        grid=(t // tm, ne // te),
        in_specs=[
            pl.BlockSpec((te, d), lambda i, e: (e, 0)),
            pl.BlockSpec((d, te), lambda i, e: (0, e)),
            pl.BlockSpec((tm, d), lambda i, e: (i, 0), pipeline_mode=once),
            pl.BlockSpec((nh, nk, tm), lambda i, e: (0, 0, i), pipeline_mode=once),
            pl.BlockSpec((nh, nk, tm), lambda i, e: (0, 0, i), pipeline_mode=once),
            pl.BlockSpec((nh, tm), lambda i, e: (0, i)),
        ],
        out_specs=pl.BlockSpec((d, tm), lambda i, e: (0, i)),
        out_shape=jax.ShapeDtypeStruct((d, t), F32),
        scratch_shapes=[pltpu.VMEM((te, tm), BF16)],
        compiler_params=_cparams("parallel", "arbitrary"),
        name="peer_dense",
    )(u_tab, v_tab_t, hq, e1_t, e2, c)


def _peer(x, norm_g, w_qs, sub_keys, u_tabs, v_tabs, layer):
    hq = _rmsnorm(x, norm_g)
    nh, nk = PEER_HEADS, PEER_N_KEYS
    s_t = _peer_scores(hq, _cast_bf16(w_qs, layer, transpose=True), sub_keys.reshape(2 * nh, nk, -1).astype(BF16))
    e1, e2, c = _peer_route(s_t)
    return _peer_dense(hq, _cast_bf16(u_tabs, layer), _cast_bf16(v_tabs, layer, transpose=True), e1, e2, c)


def _even_layer(x, h, i, w_ins, log_step, a_re, a_im, b_re, b_im, c_re, c_im, d_skip,
                glu_ws, glu_b, gn_g, gn_b, w_outs):
    s5w = d_skip.shape[0]
    u_t = _matmul([h], [_cast_bf16(w_ins, i, 0, s5w)], tile_major=True, name="ev_in_s5")
    z4 = _matmul([h], [_cast_bf16(w_ins, i, s5w)], name="ev_in_ret")
    tables = _s5_tables(log_step, a_re, a_im, b_re, b_im, c_re, c_im, d_skip)
    ya = _glu(_s5_core(u_t, tables), _cast_bf16(glu_ws, i), glu_b)
    yb = _retention(z4, gn_g, gn_b)
    w_out = _cast_bf16(w_outs, i)
    return _matmul([ya, yb], [w_out[:s5w], w_out[s5w:]], resid=x, out_dtype=F32, name="ev_out")


def _odd_layer(x, h, i, w_ins, q_norm_g, k_norm_g, lq1, lk1, lq2, lk2, sub_norm_g, w_outs,
               rel_bias, lambda_init):
    d = x.shape[1]
    z = _matmul([h], [_cast_bf16(w_ins, i, 0, 2 * d)], name="od_in_qk")
    v_t = _matmul_nt(_cast_bf16(w_ins, i, 2 * d, transpose=True), h, name="od_in_vt")
    reps = d // DIFF_HEAD_DIM
    q_t = _qknorm(z, jnp.tile(q_norm_g, reps) * (DIFF_HEAD_DIM ** -0.5 * LOG2E), 0, True)
    kn = _qknorm(z, jnp.tile(k_norm_g, reps), d, False)
    lam = jnp.exp(jnp.sum(lq1 * lk1)) - jnp.exp(jnp.sum(lq2 * lk2)) + lambda_init
    o = _diff_attention(q_t, kn, v_t, rel_bias, lam, sub_norm_g, lambda_init)
    return _matmul([o], [_cast_bf16(w_outs, i)], resid=x, out_dtype=F32, name="od_out")


def kernel(x, ev_norm_g, ev_w_in, s5_log_step, s5_a_re, s5_a_im, s5_b_re, s5_b_im, s5_c_re, s5_c_im,
           s5_d, s5_glu_w, s5_glu_b, ret_gn_g, ret_gn_b, ev_w_out, od_norm_g, od_w_in, diff_q_norm_g,
           diff_k_norm_g, diff_lq1, diff_lk1, diff_lq2, diff_lk2, diff_sub_norm_g, od_w_out, rel_bias,
           ffn_norm_g, peer_w_q, peer_sub_keys, peer_u, peer_v):
    bsz, t, d = x.shape
    depth = ffn_norm_g.shape[0]
    mixer_norm_g = lambda layer: (ev_norm_g if layer % 2 == 0 else od_norm_g)[layer // 2]
    outs = []
    for b in range(bsz):
        xb = x[b]
        h = _rmsnorm(xb, mixer_norm_g(0))
        for layer in range(depth):
            i = layer // 2
            if layer % 2 == 0:
                xb = _even_layer(xb, h, i, ev_w_in, s5_log_step[i], s5_a_re[i], s5_a_im[i],
                                 s5_b_re[i], s5_b_im[i], s5_c_re[i], s5_c_im[i], s5_d[i], s5_glu_w,
                                 s5_glu_b[i], ret_gn_g[i], ret_gn_b[i], ev_w_out)
            else:
                lambda_init = 0.8 - 0.6 * math.exp(-0.3 * layer)
                xb = _odd_layer(xb, h, i, od_w_in, diff_q_norm_g[i], diff_k_norm_g[i],
                                diff_lq1[i], diff_lk1[i], diff_lq2[i], diff_lk2[i], diff_sub_norm_g[i],
                                od_w_out, rel_bias, lambda_init)
            p_t = _peer(xb, ffn_norm_g[layer], peer_w_q, peer_sub_keys[layer], peer_u, peer_v, layer)
            xb, h = _add_norm(xb, p_t, mixer_norm_g(layer + 1) if layer + 1 < depth else None)
        outs.append(xb)
    return jnp.stack(outs, 0)
```

```python
import functools
import math

import jax
import jax.numpy as jnp
from jax import lax
from jax.experimental import pallas as pl
from jax.experimental.pallas import tpu as pltpu

F32 = jnp.float32
BF16 = jnp.bfloat16

EPS = 1e-6
LOG2E = 1.4426950408889634
LANES = 128
VMEM_LIMIT_BYTES = 56 * 1024 * 1024

S5_GROUP = 16
S5_STATE = 64
S5_CHUNK = 16
RET_HEADS = 8
RET_CHUNK = 128
RET_HEADS_PER_STEP = 2
ROPE_BASE = 10000.0
DIFF_HEADS = 16
DIFF_HEAD_DIM = 128
ATTN_BLOCK = 512
ATTN_HEADS_PER_STEP = 2
ATTN_FAR_GROUPS = (4, 2, 1)
REL_BUCKETS = 32
REL_MAX_DIST = 128
PEER_HEADS = 8
PEER_N_KEYS = 128
PEER_TOPK = 16


def _cparams(*sem, flags=None):
    return pltpu.CompilerParams(dimension_semantics=sem, vmem_limit_bytes=VMEM_LIMIT_BYTES, flags=flags)


def _dot(a, b):
    return jnp.dot(a, b, preferred_element_type=F32)


def _dot_nt(a, b):
    return lax.dot_general(a, b, (((1,), (1,)), ((), ())), preferred_element_type=F32)


def _dot_tn(a, b):
    return lax.dot_general(a, b, (((0,), (0,)), ((), ())), preferred_element_type=F32)


def _gelu(x):
    return 0.5 * x * (1.0 + jnp.tanh(0.7978845608028654 * (x + 0.044715 * (x * x * x))))


def _norm_kernel(x_ref, g_ref, o_ref):
    x = x_ref[...]
    r = lax.rsqrt(jnp.mean(x * x, axis=-1, keepdims=True) + EPS)
    o_ref[...] = (x * r * g_ref[...]).astype(o_ref.dtype)


def _rmsnorm(x, g, bt=256):
    t, d = x.shape
    return pl.pallas_call(
        _norm_kernel,
        grid=(t // bt,),
        in_specs=[pl.BlockSpec((bt, d), lambda i: (i, 0)), pl.BlockSpec((1, d), lambda i: (0, 0))],
        out_specs=pl.BlockSpec((bt, d), lambda i: (i, 0)),
        out_shape=jax.ShapeDtypeStruct((t, d), BF16),
        compiler_params=_cparams("parallel"),
        name="rmsnorm",
    )(x, g.reshape(1, d).astype(F32))


def _addnorm_kernel(x_ref, pt_ref, g_ref, xo_ref, h_ref=None):
    x = x_ref[...] + pt_ref[...].T
    xo_ref[...] = x
    if h_ref is not None:
        r = lax.rsqrt(jnp.mean(x * x, axis=-1, keepdims=True) + EPS)
        h_ref[...] = (x * r * g_ref[...]).astype(h_ref.dtype)


def _add_norm(x, p_t, g=None, bt=256):
    t, d = x.shape
    with_norm = g is not None
    row = pl.BlockSpec((bt, d), lambda i: (i, 0))
    g = jnp.ones((d,), F32) if g is None else g
    out_specs = [row, row] if with_norm else [row]
    out_shape = [jax.ShapeDtypeStruct((t, d), F32)] + ([jax.ShapeDtypeStruct((t, d), BF16)] if with_norm else [])
    out = pl.pallas_call(
        _addnorm_kernel,
        grid=(t // bt,),
        in_specs=[row, pl.BlockSpec((d, bt), lambda i: (0, i)), pl.BlockSpec((1, d), lambda i: (0, 0))],
        out_specs=out_specs,
        out_shape=out_shape,
        compiler_params=_cparams("parallel"),
        name="add_norm" if with_norm else "add_update",
    )(x, p_t, g.reshape(1, d).astype(F32))
    return (out[0], out[1]) if with_norm else (out[0], None)


def _cast_kernel(w_ref, o_ref, *, transpose):
    w = w_ref[0]
    o_ref[...] = (w.T if transpose else w).astype(o_ref.dtype)


def _cast_bf16(w, layer, col0=0, ncols=None, transpose=False, br=512, bc=4096):
    _, r, c = w.shape
    ncols = c - col0 if ncols is None else ncols
    bc = math.gcd(math.gcd(bc, ncols), col0) if col0 else math.gcd(bc, ncols)
    br = min(br, r)
    off = col0 // bc
    if transpose:
        out_spec = pl.BlockSpec((bc, br), lambda i, j: (j, i))
        out_shape = jax.ShapeDtypeStruct((ncols, r), BF16)
    else:
        out_spec = pl.BlockSpec((br, bc), lambda i, j: (i, j))
        out_shape = jax.ShapeDtypeStruct((r, ncols), BF16)
    return pl.pallas_call(
        functools.partial(_cast_kernel, transpose=transpose),
        grid=(r // br, ncols // bc),
        in_specs=[pl.BlockSpec((1, br, bc), lambda i, j: (layer, i, off + j))],
        out_specs=out_spec,
        out_shape=out_shape,
        compiler_params=_cparams("parallel", "parallel"),
        name="cast_bf16_t" if transpose else "cast_bf16",
    )(w)


def _mm_kernel(*refs, n_lhs, has_resid, tile_major):
    a_refs, b_refs = refs[:n_lhs], refs[n_lhs:2 * n_lhs]
    r_ref = refs[2 * n_lhs] if has_resid else None
    o_ref = refs[-1]
    acc = _dot(a_refs[0][...], b_refs[0][...])
    for a_ref, b_ref in zip(a_refs[1:], b_refs[1:]):
        acc = acc + _dot(a_ref[...], b_ref[...])
    if has_resid:
        acc = acc + r_ref[...]
    if tile_major:
        for jj in range(o_ref.shape[0]):
            o_ref[jj] = acc[:, jj * LANES:(jj + 1) * LANES].astype(o_ref.dtype)
    else:
        o_ref[...] = acc.astype(o_ref.dtype)


def _matmul(lhs, rhs, *, resid=None, out_dtype=BF16, bm=512, bn=1024, tile_major=False, name="mm"):
    m, n = lhs[0].shape[0], rhs[0].shape[1]
    bm, bn = min(bm, m), min(bn, n)
    in_specs = [pl.BlockSpec((bm, a.shape[1]), lambda i, j: (i, 0)) for a in lhs]
    in_specs += [pl.BlockSpec((b.shape[0], bn), lambda i, j: (0, j)) for b in rhs]
    args = list(lhs) + list(rhs)
    if resid is not None:
        in_specs.append(pl.BlockSpec((bm, bn), lambda i, j: (i, j)))
        args.append(resid)
    if tile_major:
        out_spec = pl.BlockSpec((bn // LANES, bm, LANES), lambda i, j: (j, i, 0))
        out_shape = jax.ShapeDtypeStruct((n // LANES, m, LANES), out_dtype)
    else:
        out_spec = pl.BlockSpec((bm, bn), lambda i, j: (i, j))
        out_shape = jax.ShapeDtypeStruct((m, n), out_dtype)
    kern = functools.partial(_mm_kernel, n_lhs=len(lhs), has_resid=resid is not None, tile_major=tile_major)
    return pl.pallas_call(
        kern, grid=(m // bm, n // bn), in_specs=in_specs, out_specs=out_spec, out_shape=out_shape,
        compiler_params=_cparams("parallel", "parallel"), name=name,
    )(*args)


def _mm_nt_kernel(w_ref, h_ref, o_ref):
    o_ref[...] = _dot_nt(w_ref[...], h_ref[...]).astype(o_ref.dtype)


def _matmul_nt(w_t, h, *, bm=512, bn=1024, out_dtype=BF16, name="mm_nt"):
    n, k = w_t.shape
    m = h.shape[0]
    bm, bn = min(bm, m), min(bn, n)
    return pl.pallas_call(
        _mm_nt_kernel,
        grid=(m // bm, n // bn),
        in_specs=[pl.BlockSpec((bn, k), lambda i, j: (j, 0)), pl.BlockSpec((bm, k), lambda i, j: (i, 0))],
        out_specs=pl.BlockSpec((bn, bm), lambda i, j: (j, i)),
        out_shape=jax.ShapeDtypeStruct((n, m), out_dtype),
        compiler_params=_cparams("parallel", "parallel"),
        name=name,
    )(w_t, h)


def _s5_tables(log_step, a_re, a_im, b_re, b_im, c_re, c_im, d_skip):
    L = S5_CHUNK
    g, p = a_re.shape
    gt = LANES // S5_GROUP
    nt = g // gt
    hp = lax.Precision.HIGHEST
    step = jnp.exp(log_step)[:, None]
    lr, li = a_re, a_im
    mag = jnp.exp(lr * step)
    abar_re = mag * jnp.cos(li * step)
    abar_im = mag * jnp.sin(li * step)
    den = lr * lr + li * li
    num_re = abar_re - 1.0
    f_re = (num_re * lr + abar_im * li) / den
    f_im = (abar_im * lr - num_re * li) / den
    bb_re = f_re[..., None] * b_re - f_im[..., None] * b_im
    bb_im = f_re[..., None] * b_im + f_im[..., None] * b_re
    k = jnp.arange(L + 1, dtype=F32)[:, None, None]
    pmag = jnp.exp(k * (lr * step)[None])
    pang = k * (li * step)[None]
    pw_re = pmag * jnp.cos(pang)
    pw_im = pmag * jnp.sin(pang)
    ns = 2 * gt * p

    def same_group(row_g, col_g):
        return (row_g[:, None] == col_g[None, :]).astype(F32)

    cp_re = c_re[None] * pw_re[:L, :, None, :] - c_im[None] * pw_im[:L, :, None, :]
    cp_im = c_re[None] * pw_im[:L, :, None, :] + c_im[None] * pw_re[:L, :, None, :]
    kk = (jnp.einsum("dgop,gpi->dgoi", cp_re, bb_re, precision=hp)
          - jnp.einsum("dgop,gpi->dgoi", cp_im, bb_im, precision=hp))
    kki = kk.transpose(0, 1, 3, 2).reshape(L, nt, LANES, S5_GROUP)
    lane_g = jnp.arange(LANES) // S5_GROUP
    bd = jnp.tile(kki, (1, 1, 1, gt)) * same_group(lane_g, lane_g)
    bd = bd.transpose(1, 0, 2, 3).astype(BF16)

    k_rev = (L - 1) - k[:L]
    pmag_rev = jnp.exp(k_rev * (lr * step)[None])
    pang_rev = k_rev * (li * step)[None]
    pwr, pwi = pmag_rev * jnp.cos(pang_rev), pmag_rev * jnp.sin(pang_rev)
    pb_re = pwr[..., None] * bb_re[None] - pwi[..., None] * bb_im[None]
    pb_im = pwr[..., None] * bb_im[None] + pwi[..., None] * bb_re[None]
    pb = jnp.stack([pb_re, pb_im], 0).reshape(2, L, nt, gt, p, S5_GROUP)
    ps = pb.transpose(2, 1, 3, 5, 0, 4).reshape(nt, L * LANES, 2 * p).astype(BF16)
    rr, cc = jnp.arange(2 * p), jnp.arange(ns)
    ex_p = ((rr[:, None] // p == cc[None, :] // (gt * p)) & (rr[:, None] % p == cc[None, :] % p))
    row_g = (jnp.arange(L * LANES) % LANES) // S5_GROUP
    col_g = (cc % (gt * p)) // p
    p_parts = (ps, ex_p.astype(BF16), same_group(row_g, col_g).astype(BF16))

    qr, qi = pw_re[1:L + 1], pw_im[1:L + 1]
    cn_re = c_re[None] * qr[:, :, None, :] - c_im[None] * qi[:, :, None, :]
    cn_im = c_re[None] * qi[:, :, None, :] + c_im[None] * qr[:, :, None, :]
    nb = jnp.stack([cn_re, -cn_im], 0).reshape(2, L, nt, gt, S5_GROUP, p)
    nsm = nb.transpose(2, 0, 3, 5, 1, 4).reshape(nt, ns, L * S5_GROUP).astype(BF16)
    rr, cc = jnp.arange(L * S5_GROUP), jnp.arange(L * LANES)
    ex_n = ((rr[:, None] // S5_GROUP == cc[None, :] // LANES)
            & (rr[:, None] % S5_GROUP == cc[None, :] % S5_GROUP))
    row_g = (jnp.arange(ns) % (gt * p)) // p
    col_g = (cc % LANES) // S5_GROUP
    n_parts = (nsm, ex_n.astype(BF16), same_group(row_g, col_g).astype(BF16))

    a_l = jnp.stack([pw_re[L], pw_im[L]], 0).reshape(2, nt, gt * p).transpose(1, 0, 2)
    d_t = jnp.tile(d_skip.reshape(nt, 1, LANES), (1, 1, L))
    return bd, p_parts, n_parts, a_l, d_t


def _s5_kernel(x_ref, bd_ref, ps_ref, exp_ref, mkp_ref, ns_ref, exn_ref, mkn_ref, al_ref, d_ref, o_ref,
               m_ref, sloc_ref, sprev_ref):
    x = x_ref[0]
    nc, ns = sloc_ref.shape
    half = ns // 2
    L = bd_ref.shape[1]
    p_op = _dot(ps_ref[0], exp_ref[...]).astype(BF16) * mkp_ref[...]
    n_op = _dot(ns_ref[0], exn_ref[...]).astype(BF16) * mkn_ref[...]
    for tau in range(L):
        for t in range(L):
            blk = bd_ref[0, t - tau] if t >= tau else jnp.zeros((LANES, LANES), m_ref.dtype)
            m_ref[tau * LANES:(tau + 1) * LANES, t * LANES:(t + 1) * LANES] = blk
    sloc_ref[...] = _dot(x, p_op)
    a_re = al_ref[0, 0:1, :]
    a_im = al_ref[0, 1:2, :]

    def body(c, carry):
        s_re, s_im = carry
        row = pl.ds(c, 1)
        sprev_ref[row, 0:half] = s_re
        sprev_ref[row, half:ns] = s_im
        l_re = sloc_ref[row, 0:half]
        l_im = sloc_ref[row, half:ns]
        return (a_re * s_re - a_im * s_im + l_re, a_re * s_im + a_im * s_re + l_im)

    zero = jnp.zeros((1, half), F32)
    lax.fori_loop(0, nc, body, (zero, zero))
    y = _dot(x, m_ref[...]) + _dot(sprev_ref[...].astype(BF16), n_op)
    y = y + d_ref[0] * x.astype(F32)
    o_ref[0] = _gelu(y).astype(o_ref.dtype)


def _s5_core(u_t, tables):
    bd, (ps, ex_p, mk_p), (nsm, ex_n, mk_n), a_l, d_t = tables
    nt, t, _ = u_t.shape
    L = S5_CHUNK
    nc, w, ns = t // L, L * LANES, mk_p.shape[1]
    x = u_t.reshape(nt, nc, w)
    per_tile = lambda a: pl.BlockSpec((1,) + a.shape[1:], lambda j: (j,) + (0,) * (a.ndim - 1))
    shared = lambda a: pl.BlockSpec(a.shape, lambda j: (0,) * a.ndim, pipeline_mode=pl.Buffered(1))
    out = pl.pallas_call(
        _s5_kernel,
        grid=(nt,),
        in_specs=[
            pl.BlockSpec((1, nc, w), lambda j: (j, 0, 0)),
            per_tile(bd),
            per_tile(ps), shared(ex_p), shared(mk_p),
            per_tile(nsm), shared(ex_n), shared(mk_n),
            per_tile(a_l),
            per_tile(d_t),
        ],
        out_specs=pl.BlockSpec((1, nc, w), lambda j: (j, 0, 0)),
        out_shape=jax.ShapeDtypeStruct((nt, nc, w), BF16),
        scratch_shapes=[pltpu.VMEM((w, w), BF16), pltpu.VMEM((nc, ns), F32), pltpu.VMEM((nc, ns), F32)],
        compiler_params=_cparams("parallel"),
        name="s5_core",
    )(x, bd, ps, ex_p, mk_p, nsm, ex_n, mk_n, a_l, d_t)
    return out.reshape(nt, t, LANES)


def _glu_kernel(y_ref, w_ref, b_ref, o_ref):
    y = jnp.concatenate([y_ref[j] for j in range(y_ref.shape[0])], axis=1)
    acc = _dot(y, w_ref[...]) + b_ref[...]
    o_ref[...] = (y.astype(F32) * jax.nn.sigmoid(acc)).astype(o_ref.dtype)


def _glu(y_t, w, b, bm=512):
    nt, t, _ = y_t.shape
    width = nt * LANES
    return pl.pallas_call(
        _glu_kernel,
        grid=(t // bm,),
        in_specs=[
            pl.BlockSpec((nt, bm, LANES), lambda i: (0, i, 0)),
            pl.BlockSpec((width, width), lambda i: (0, 0)),
            pl.BlockSpec((1, width), lambda i: (0, 0)),
        ],
        out_specs=pl.BlockSpec((bm, width), lambda i: (i, 0)),
        out_shape=jax.ShapeDtypeStruct((t, width), BF16),
        compiler_params=_cparams("parallel"),
        name="s5_glu",
    )(y_t, w, b.reshape(1, width).astype(F32))


def _ret_kernel(q_ref, k_ref, v_ref, g_ref, cos_ref, sin_ref, intra_ref, qd_ref, kd_ref, cd_ref,
                gg_ref, gb_ref, o_ref, state_ref, *, chunk, scale):
    @pl.when(pl.program_id(1) == 0)
    def _():
        state_ref[...] = jnp.zeros_like(state_ref)

    half = cos_ref.shape[1]
    hd = 2 * half
    nhs = state_ref.shape[0]

    def body(c, carry):
        rows = pl.ds(pl.multiple_of(c * chunk, chunk), chunk)
        cos, sin = cos_ref[rows, :], sin_ref[rows, :]

        def rot(x):
            x1, x2 = x[:, :half], x[:, half:]
            return jnp.concatenate([x1 * cos - x2 * sin, x1 * sin + x2 * cos], axis=1)

        for hh in range(nhs):
            cols = slice(hh * hd, (hh + 1) * hd)
            q = rot(q_ref[rows, cols].astype(F32))
            k = rot(k_ref[rows, cols].astype(F32)) * scale
            v = v_ref[rows, cols]
            st = state_ref[hh]
            scores = _dot_nt(q.astype(BF16), k.astype(BF16)) * intra_ref[hh]
            out = _dot(scores.astype(BF16), v) + _dot((q * qd_ref[hh]).astype(BF16), st.astype(BF16))
            state_ref[hh] = cd_ref[hh] * st + _dot_tn((k * kd_ref[hh]).astype(BF16), v)
            mu = jnp.mean(out, axis=-1, keepdims=True)
            cen = out - mu
            var = jnp.mean(cen * cen, axis=-1, keepdims=True)
            o = cen * lax.rsqrt(var + EPS) * gg_ref[:, cols] + gb_ref[:, cols]
            gt = g_ref[rows, cols].astype(F32)
            o_ref[rows, cols] = (gt * jax.nn.sigmoid(gt) * o).astype(o_ref.dtype)
        return carry

    lax.fori_loop(0, q_ref.shape[0] // chunk, body, 0)


def _retention(z4, gn_g, gn_b, tb=2048):
    t = z4.shape[0]
    w = z4.shape[1] // 4
    hd = w // RET_HEADS
    half = hd // 2
    c = RET_CHUNK
    tb = min(tb, t)
    pos = jnp.arange(t, dtype=F32)
    freqs = ROPE_BASE ** (-jnp.arange(half, dtype=F32) / half)
    ang = pos[:, None] * freqs[None, :]
    cos, sin = jnp.cos(ang), jnp.sin(ang)
    gamma = 1.0 - 2.0 ** (-5.0 - jnp.arange(RET_HEADS, dtype=F32))
    log_g = jnp.log(gamma)
    idx = jnp.arange(c, dtype=F32)
    rel = idx[:, None] - idx[None, :]
    intra = jnp.where(rel >= 0, jnp.exp(log_g[:, None, None] * jnp.maximum(rel, 0.0)), 0.0)
    q_decay = jnp.exp(log_g[:, None] * (idx + 1.0))[..., None]
    k_decay = jnp.exp(log_g[:, None] * (c - 1.0 - idx))[..., None]
    chunk_decay = jnp.exp(log_g * c)[:, None, None]
    nhs = RET_HEADS_PER_STEP
    ng = RET_HEADS // nhs
    blk = lambda off: pl.BlockSpec((tb, nhs * hd), lambda h, s: (s, off + h))
    per_head = lambda shape: pl.BlockSpec((nhs,) + shape, lambda h, s: (h, 0, 0))
    return pl.pallas_call(
        functools.partial(_ret_kernel, chunk=c, scale=hd ** -0.5),
        grid=(ng, t // tb),
        in_specs=[
            blk(0), blk(ng), blk(2 * ng), blk(3 * ng),
            pl.BlockSpec((tb, half), lambda h, s: (s, 0)),
            pl.BlockSpec((tb, half), lambda h, s: (s, 0)),
            per_head((c, c)), per_head((c, 1)), per_head((c, 1)), per_head((1, 1)),
            pl.BlockSpec((1, nhs * hd), lambda h, s: (0, h)),
            pl.BlockSpec((1, nhs * hd), lambda h, s: (0, h)),
        ],
        out_specs=pl.BlockSpec((tb, nhs * hd), lambda h, s: (s, h)),
        out_shape=jax.ShapeDtypeStruct((t, w), BF16),
        scratch_shapes=[pltpu.VMEM((nhs, hd, hd), F32)],
        compiler_params=_cparams("parallel", "arbitrary"),
        name="retention",
    )(z4, z4, z4, z4, cos, sin, intra, q_decay, k_decay, chunk_decay,
      gn_g.reshape(1, w).astype(F32), gn_b.reshape(1, w).astype(F32))


def _qknorm_kernel(x_ref, gain_ref, ones_ref, o_ref, *, inv_dim, transpose_out):
    x = x_ref[...].astype(F32)
    xx = x * x
    hi = xx.astype(BF16)
    lo = (xx - hi.astype(F32)).astype(BF16)
    ss = _dot(hi, ones_ref[...]) + _dot(lo, ones_ref[...])
    y = x * lax.rsqrt(ss * inv_dim + EPS) * gain_ref[...]
    o_ref[...] = (y.T if transpose_out else y).astype(o_ref.dtype)


def _qknorm(z, gain, col_off, transpose_out, bt=1024, bw=256):
    t = z.shape[0]
    w = gain.shape[0]
    bt = min(bt, t)
    off = col_off // bw
    grp = jnp.arange(bw) // DIFF_HEAD_DIM
    ones = (grp[:, None] == grp[None, :]).astype(BF16)
    if transpose_out:
        out_spec = pl.BlockSpec((bw, bt), lambda i, j: (j, i))
        out_shape = jax.ShapeDtypeStruct((w, t), BF16)
    else:
        out_spec = pl.BlockSpec((bt, bw), lambda i, j: (i, j))
        out_shape = jax.ShapeDtypeStruct((t, w), BF16)
    return pl.pallas_call(
        functools.partial(_qknorm_kernel, inv_dim=1.0 / DIFF_HEAD_DIM, transpose_out=transpose_out),
        grid=(t // bt, w // bw),
        in_specs=[
            pl.BlockSpec((bt, bw), lambda i, j: (i, off + j)),
            pl.BlockSpec((1, bw), lambda i, j: (0, j)),
            pl.BlockSpec((bw, bw), lambda i, j: (0, 0)),
        ],
        out_specs=out_spec,
        out_shape=out_shape,
        compiler_params=_cparams("parallel", "parallel"),
        name="q_norm_t" if transpose_out else "k_norm",
    )(z, gain.reshape(1, w).astype(F32), ones)


def _attn_kernel(qt_ref, k_ref, vt_ref, d0_ref, d1_ref, lam_ref, sg_ref, o_ref, m_ref, acc_ref, va_ref,
                 *, blk, out_scale):
    i = pl.program_id(1)
    hd = DIFF_HEAD_DIM
    dv = 2 * hd
    nhs = va_ref.shape[0]

    @pl.when(i == 0)
    def _():
        for hh in range(nhs):
            for jb in range(va_ref.shape[1]):
                va_ref[hh, jb, 0:dv, :] = vt_ref[hh * dv:(hh + 1) * dv, jb * blk:(jb + 1) * blk]
                va_ref[hh, jb, dv:, :] = jnp.ones((va_ref.shape[2] - dv, blk), va_ref.dtype)

    m_ref[...] = jnp.full(m_ref.shape, -1e30, F32)
    acc_ref[...] = jnp.zeros_like(acc_ref)

    def step(j, bias_refs):
        nsub = len(bias_refs)
        rows = pl.ds(pl.multiple_of(j * blk, blk), nsub * blk)
        for hh in range(nhs):
            for mi in range(2):
                c = 2 * hh + mi
                s = _dot(k_ref[rows, c * hd:(c + 1) * hd], qt_ref[c * hd:(c + 1) * hd, :])
                if bias_refs[0] is not None:
                    s = s + jnp.concatenate([b[hh] for b in bias_refs], axis=0)
                m_old = m_ref[c]
                m_new = jnp.maximum(m_old, jnp.max(s, axis=0, keepdims=True))
                p = jnp.exp2(s - m_new).astype(BF16)
                alpha = jnp.exp2(m_old - m_new)
                pv = _dot(va_ref[hh, j], p[0:blk])
                for u in range(1, nsub):
                    pv = pv + _dot(va_ref[hh, j + u], p[u * blk:(u + 1) * blk])
                acc_ref[c] = alpha * acc_ref[c] + pv
                m_ref[c] = m_new

    n_far = jnp.maximum(i - 1, 0)
    done = 0
    for grp in ATTN_FAR_GROUPS:
        n_grp = (n_far - done) // grp

        def far_group(gi, carry, grp=grp, done=done):
            step(done + grp * gi, (None,) * grp)
            return carry

        lax.fori_loop(0, n_grp, far_group, 0)
        done = done + n_grp * grp

    @pl.when(i >= 1)
    def _():
        step(i - 1, (d1_ref, d0_ref))

    @pl.when(i == 0)
    def _():
        step(0, (d0_ref,))
    for hh in range(nhs):
        a0, a1 = acc_ref[2 * hh], acc_ref[2 * hh + 1]
        o_t = a0[:dv] / a0[dv:dv + 1] - lam_ref[...] * (a1[:dv] / a1[dv:dv + 1])
        o = o_t.T
        ms = jnp.mean(o * o, axis=-1, keepdims=True)
        o_ref[:, hh * dv:(hh + 1) * dv] = (o * lax.rsqrt(ms + EPS) * sg_ref[...] * out_scale).astype(o_ref.dtype)


def _t5_bucket(n):
    max_exact = REL_BUCKETS // 2
    nf = jnp.maximum(n, 1).astype(F32)
    large = max_exact + (jnp.log(nf / max_exact) / math.log(REL_MAX_DIST / max_exact)
                         * (REL_BUCKETS - max_exact)).astype(jnp.int32)
    large = jnp.minimum(large, REL_BUCKETS - 1)
    return jnp.where(n < max_exact, n, large)


def _diff_attention(q_t, kn, v_t, rel_bias, lam, sub_g, lambda_init):
    t = kn.shape[0]
    nh = DIFF_HEADS
    dv = 2 * DIFF_HEAD_DIM
    ones_rows = 16
    blk = min(ATTN_BLOCK, t)
    nhs = ATTN_HEADS_PER_STEP
    assert blk >= REL_MAX_DIST
    key = jnp.arange(blk)[:, None]
    qry = jnp.arange(blk)[None, :]
    far_bias = rel_bias[REL_BUCKETS - 1]

    def bias_tile(rel):
        onehot = (_t5_bucket(rel).reshape(-1)[None, :] == jnp.arange(REL_BUCKETS)[:, None]).astype(F32)
        tile = jnp.dot((rel_bias - far_bias).T * LOG2E, onehot, precision=lax.Precision.HIGHEST)
        return tile.reshape(nh, blk, blk)

    rel0 = qry - key
    d0 = jnp.where((rel0 >= 0)[None], bias_tile(jnp.maximum(rel0, 0)), -jnp.inf)
    d1 = bias_tile(blk + qry - key)
    once = pl.Buffered(1)
    return pl.pallas_call(
        functools.partial(_attn_kernel, blk=blk, out_scale=1.0 - lambda_init),
        grid=(nh // nhs, t // blk),
        in_specs=[
            pl.BlockSpec((nhs * dv, blk), lambda h, i: (h, i)),
            pl.BlockSpec((t, nhs * dv), lambda h, i: (0, h), pipeline_mode=once),
            pl.BlockSpec((nhs * dv, t), lambda h, i: (h, 0), pipeline_mode=once),
            pl.BlockSpec((nhs, blk, blk), lambda h, i: (h, 0, 0), pipeline_mode=once),
            pl.BlockSpec((nhs, blk, blk), lambda h, i: (h, 0, 0), pipeline_mode=once),
            pl.BlockSpec((1, 1), lambda h, i: (0, 0)),
            pl.BlockSpec((1, dv), lambda h, i: (0, 0)),
        ],
        out_specs=pl.BlockSpec((blk, nhs * dv), lambda h, i: (i, h)),
        out_shape=jax.ShapeDtypeStruct((t, nh * dv), BF16),
        scratch_shapes=[pltpu.VMEM((2 * nhs, 1, blk), F32), pltpu.VMEM((2 * nhs, dv + ones_rows, blk), F32),
                        pltpu.VMEM((nhs, t // blk, dv + ones_rows, blk), BF16)],
        compiler_params=_cparams("parallel", "arbitrary"),
        name="diff_attn",
    )(q_t, kn, v_t, d0, d1, jnp.reshape(lam, (1, 1)).astype(F32), sub_g.reshape(1, dv).astype(F32))


def _fold_keys_kernel(sk_ref, w_ref, o_ref):
    w = w_ref[0]
    for b in range(sk_ref.shape[0]):
        cols = slice(b * LANES, (b + 1) * LANES)
        o_ref[cols, :] = _dot_nt(sk_ref[b], w[:, cols]).astype(o_ref.dtype)


def _fold_keys(w_qs, layer, sub_keys, bd=1024, nb=4):
    _, d, n = w_qs.shape
    return pl.pallas_call(
        _fold_keys_kernel,
        grid=(d // bd, n // (nb * LANES)),
        in_specs=[
            pl.BlockSpec((nb, LANES, LANES), lambda i, j: (j, 0, 0)),
            pl.BlockSpec((1, bd, nb * LANES), lambda i, j: (layer, i, j)),
        ],
        out_specs=pl.BlockSpec((nb * LANES, bd), lambda i, j: (j, i)),
        out_shape=jax.ShapeDtypeStruct((n, d), BF16),
        compiler_params=_cparams("parallel", "parallel"),
        name="peer_fold_keys",
    )(sub_keys, w_qs)


def _top_values(s, n):
    vals = []
    for r in range(n):
        m = jnp.max(s, axis=0, keepdims=True)
        vals.append(m)
        if r < n - 1:
            s = jnp.where(s == m, -jnp.inf, s)
    return vals


def _route_kernel(s_ref, e1_ref, e2_ref, c_ref):
    nk = PEER_N_KEYS
    k = PEER_TOPK
    tl = s_ref.shape[1]
    pad = 24
    row = lax.broadcasted_iota(jnp.int32, (pad, tl), 0)
    hrow = lax.broadcasted_iota(jnp.int32, (PEER_HEADS, tl), 0)
    c_all = jnp.zeros((PEER_HEADS, tl), F32)
    for h in range(PEER_HEADS):
        s1 = s_ref[2 * h * nk:(2 * h + 1) * nk, :]
        s2 = s_ref[(2 * h + 1) * nk:(2 * h + 2) * nk, :]
        a = _top_values(s1, k + 1)
        b = _top_values(s2, k + 1)
        ea = [jnp.exp(x - a[0]) for x in a]
        eb = [jnp.exp(x - b[0]) for x in b]
        ea_m = jnp.full((pad, tl), -1.0, F32)
        eb_m = jnp.full((pad, tl), -1.0, F32)
        for r in range(k + 1):
            ea_m = jnp.where(row == r, ea[r], ea_m)
            eb_m = jnp.where(row == r, eb[r], eb_m)
        cand = jnp.concatenate(
            [ea[0] * eb_m, ea[1] * eb_m[:8], ea[2] * eb_m[:8], ea[3] * eb_m[:8],
             eb[0] * ea_m, eb[1] * ea_m[:8], eb[2] * ea_m[:8]], axis=0)
        v = _top_values(cand, k + 1)
        z = v[0]
        for r in range(1, k):
            z = z + v[r]
        rz = 1.0 / z
        e1_ref[h] = jnp.exp(s1 - a[0])
        e2_ref[h] = (jnp.exp(s2 - b[0]) * rz).astype(e2_ref.dtype)
        c_all = jnp.where(hrow == h, 0.5 * (v[k - 1] + v[k]) * rz, c_all)
    c_ref[...] = c_all


def _peer_route(s_t, tl=256):
    n, t = s_t.shape
    tl = min(tl, t)
    nh, nk = PEER_HEADS, PEER_N_KEYS
    return pl.pallas_call(
        _route_kernel,
        grid=(t // tl,),
        in_specs=[pl.BlockSpec((n, tl), lambda i: (0, i))],
        out_specs=[
            pl.BlockSpec((nh, nk, tl), lambda i: (0, 0, i)),
            pl.BlockSpec((nh, nk, tl), lambda i: (0, 0, i)),
            pl.BlockSpec((nh, tl), lambda i: (0, i)),
        ],
        out_shape=[
            jax.ShapeDtypeStruct((nh, nk, t), F32),
            jax.ShapeDtypeStruct((nh, nk, t), BF16),
            jax.ShapeDtypeStruct((nh, t), F32),
        ],
        compiler_params=_cparams("parallel"),
        name="peer_route",
    )(s_t)


def _peer_dense_kernel(u_ref, vt_ref, h_ref, e1_ref, e2_ref, c_ref, o_ref, w_ref):
    e = pl.program_id(1)

    @pl.when(e == 0)
    def _():
        o_ref[...] = jnp.zeros_like(o_ref)

    a_t = _dot_nt(u_ref[...], h_ref[...])
    tm = a_t.shape[1]
    nslab = a_t.shape[0] // LANES
    gdt = e2_ref.dtype
    for ii in range(nslab):
        rows = slice(ii * LANES, (ii + 1) * LANES)
        g = jnp.zeros((LANES, tm), gdt)
        for h in range(PEER_HEADS):
            vv = e2_ref[h] * e1_ref[h, pl.ds(e * nslab + ii, 1), :].astype(gdt)
            g = g + jnp.where(vv >= c_ref[h:h + 1, :].astype(gdt), vv, jnp.zeros_like(vv))
        w_ref[rows, :] = (g * _gelu(a_t[rows, :].astype(gdt))).astype(w_ref.dtype)
    o_ref[...] += _dot(vt_ref[...], w_ref[...])


def _peer_dense(hq, u_tab, v_tab_t, e1_t, e2, c, tm=512, te=512):
    t, d = hq.shape
    ne = u_tab.shape[0]
    tm = min(tm, t)
    nh, nk = PEER_HEADS, PEER_N_KEYS
    once = pl.Buffered(1)
    return pl.pallas_call(
        _peer_dense_kernel,
        grid=(t // tm, ne // te),
        in_specs=[
            pl.BlockSpec((te, d), lambda i, e: (e, 0)),
            pl.BlockSpec((d, te), lambda i, e: (0, e)),
            pl.BlockSpec((tm, d), lambda i, e: (i, 0), pipeline_mode=once),
            pl.BlockSpec((nh, nk, tm), lambda i, e: (0, 0, i), pipeline_mode=once),
            pl.BlockSpec((nh, nk, tm), lambda i, e: (0, 0, i), pipeline_mode=once),
            pl.BlockSpec((nh, tm), lambda i, e: (0, i)),
        ],
        out_specs=pl.BlockSpec((d, tm), lambda i, e: (0, i)),
        out_shape=jax.ShapeDtypeStruct((d, t), F32),
        scratch_shapes=[pltpu.VMEM((te, tm), BF16)],
        compiler_params=_cparams("parallel", "arbitrary"),
        name="peer_dense",
    )(u_tab, v_tab_t, hq, e1_t, e2, c)


def _peer(x, norm_g, w_qs, sub_keys, u_tabs, v_tabs, layer):
    hq = _rmsnorm(x, norm_g)
    nh, nk = PEER_HEADS, PEER_N_KEYS
    s_t = _matmul_nt(_fold_keys(w_qs, layer, sub_keys.reshape(2 * nh, nk, -1)), hq, out_dtype=F32, name="peer_scores")
    e1, e2, c = _peer_route(s_t)
    return _peer_dense(hq, _cast_bf16(u_tabs, layer), _cast_bf16(v_tabs, layer, transpose=True), e1, e2, c)


def _even_layer(x, h, i, w_ins, log_step, a_re, a_im, b_re, b_im, c_re, c_im, d_skip,
                glu_ws, glu_b, gn_g, gn_b, w_outs):
    s5w = d_skip.shape[0]
    u_t = _matmul([h], [_cast_bf16(w_ins, i, 0, s5w)], tile_major=True, name="ev_in_s5")
    z4 = _matmul([h], [_cast_bf16(w_ins, i, s5w)], name="ev_in_ret")
    tables = _s5_tables(log_step, a_re, a_im, b_re, b_im, c_re, c_im, d_skip)
    ya = _glu(_s5_core(u_t, tables), _cast_bf16(glu_ws, i), glu_b)
    yb = _retention(z4, gn_g, gn_b)
    w_out = _cast_bf16(w_outs, i)
    return _matmul([ya, yb], [w_out[:s5w], w_out[s5w:]], resid=x, out_dtype=F32, name="ev_out")


def _odd_layer(x, h, i, w_ins, q_norm_g, k_norm_g, lq1, lk1, lq2, lk2, sub_norm_g, w_outs,
               rel_bias, lambda_init):
    d = x.shape[1]
    z = _matmul([h], [_cast_bf16(w_ins, i, 0, 2 * d)], name="od_in_qk")
    v_t = _matmul_nt(_cast_bf16(w_ins, i, 2 * d, transpose=True), h, name="od_in_vt")
    reps = d // DIFF_HEAD_DIM
    q_t = _qknorm(z, jnp.tile(q_norm_g, reps) * (DIFF_HEAD_DIM ** -0.5 * LOG2E), 0, True)
    kn = _qknorm(z, jnp.tile(k_norm_g, reps), d, False)
    lam = jnp.exp(jnp.sum(lq1 * lk1)) - jnp.exp(jnp.sum(lq2 * lk2)) + lambda_init
    o = _diff_attention(q_t, kn, v_t, rel_bias, lam, sub_norm_g, lambda_init)
    return _matmul([o], [_cast_bf16(w_outs, i)], resid=x, out_dtype=F32, name="od_out")


def kernel(x, ev_norm_g, ev_w_in, s5_log_step, s5_a_re, s5_a_im, s5_b_re, s5_b_im, s5_c_re, s5_c_im,
           s5_d, s5_glu_w, s5_glu_b, ret_gn_g, ret_gn_b, ev_w_out, od_norm_g, od_w_in, diff_q_norm_g,
           diff_k_norm_g, diff_lq1, diff_lk1, diff_lq2, diff_lk2, diff_sub_norm_g, od_w_out, rel_bias,
           ffn_norm_g, peer_w_q, peer_sub_keys, peer_u, peer_v):
    bsz, t, d = x.shape
    depth = ffn_norm_g.shape[0]
    mixer_norm_g = lambda layer: (ev_norm_g if layer % 2 == 0 else od_norm_g)[layer // 2]
    outs = []
    for b in range(bsz):
        xb = x[b]
        h = _rmsnorm(xb, mixer_norm_g(0))
        for layer in range(depth):
            i = layer // 2
            if layer % 2 == 0:
                xb = _even_layer(xb, h, i, ev_w_in, s5_log_step[i], s5_a_re[i], s5_a_im[i],
                                 s5_b_re[i], s5_b_im[i], s5_c_re[i], s5_c_im[i], s5_d[i], s5_glu_w,
                                 s5_glu_b[i], ret_gn_g[i], ret_gn_b[i], ev_w_out)
            else:
                lambda_init = 0.8 - 0.6 * math.exp(-0.3 * layer)
                xb = _odd_layer(xb, h, i, od_w_in, diff_q_norm_g[i], diff_k_norm_g[i],
                                diff_lq1[i], diff_lk1[i], diff_lq2[i], diff_lk2[i], diff_sub_norm_g[i],
                                od_w_out, rel_bias, lambda_init)
            p_t = _peer(xb, ffn_norm_g[layer], peer_w_q, peer_sub_keys[layer], peer_u, peer_v, layer)
            xb, h = _add_norm(xb, p_t, mixer_norm_g(layer + 1) if layer + 1 < depth else None)
        outs.append(xb)
    return jnp.stack(outs, 0)
```

```python
import functools
import math

import jax
import jax.numpy as jnp
from jax import lax
from jax.experimental import pallas as pl
from jax.experimental.pallas import tpu as pltpu

F32 = jnp.float32
BF16 = jnp.bfloat16

EPS = 1e-6
LOG2E = 1.4426950408889634
LANES = 128
VMEM_LIMIT_BYTES = 56 * 1024 * 1024

S5_GROUP = 16
S5_STATE = 64
S5_CHUNK = 16
RET_HEADS = 8
RET_CHUNK = 128
RET_HEADS_PER_STEP = 2
ROPE_BASE = 10000.0
DIFF_HEADS = 16
DIFF_HEAD_DIM = 128
ATTN_BLOCK = 512
ATTN_HEADS_PER_STEP = 2
ATTN_FAR_GROUPS = (4, 2, 1)
REL_BUCKETS = 32
REL_MAX_DIST = 128
PEER_HEADS = 8
PEER_N_KEYS = 128
PEER_TOPK = 16


def _cparams(*sem, flags=None):
    return pltpu.CompilerParams(dimension_semantics=sem, vmem_limit_bytes=VMEM_LIMIT_BYTES, flags=flags)


def _dot(a, b):
    return jnp.dot(a, b, preferred_element_type=F32)


def _dot_nt(a, b):
    return lax.dot_general(a, b, (((1,), (1,)), ((), ())), preferred_element_type=F32)


def _dot_tn(a, b):
    return lax.dot_general(a, b, (((0,), (0,)), ((), ())), preferred_element_type=F32)


def _gelu(x):
    return 0.5 * x * (1.0 + jnp.tanh(0.7978845608028654 * (x + 0.044715 * (x * x * x))))


def _norm_kernel(x_ref, g_ref, o_ref):
    x = x_ref[...]
    r = lax.rsqrt(jnp.mean(x * x, axis=-1, keepdims=True) + EPS)
    o_ref[...] = (x * r * g_ref[...]).astype(o_ref.dtype)


def _rmsnorm(x, g, bt=256):
    t, d = x.shape
    return pl.pallas_call(
        _norm_kernel,
        grid=(t // bt,),
        in_specs=[pl.BlockSpec((bt, d), lambda i: (i, 0)), pl.BlockSpec((1, d), lambda i: (0, 0))],
        out_specs=pl.BlockSpec((bt, d), lambda i: (i, 0)),
        out_shape=jax.ShapeDtypeStruct((t, d), BF16),
        compiler_params=_cparams("parallel"),
        name="rmsnorm",
    )(x, g.reshape(1, d).astype(F32))


def _addnorm_kernel(x_ref, pt_ref, g_ref, xo_ref, h_ref=None):
    x = x_ref[...] + pt_ref[...].T
    xo_ref[...] = x
    if h_ref is not None:
        r = lax.rsqrt(jnp.mean(x * x, axis=-1, keepdims=True) + EPS)
        h_ref[...] = (x * r * g_ref[...]).astype(h_ref.dtype)


def _add_norm(x, p_t, g=None, bt=256):
    t, d = x.shape
    with_norm = g is not None
    row = pl.BlockSpec((bt, d), lambda i: (i, 0))
    g = jnp.ones((d,), F32) if g is None else g
    out_specs = [row, row] if with_norm else [row]
    out_shape = [jax.ShapeDtypeStruct((t, d), F32)] + ([jax.ShapeDtypeStruct((t, d), BF16)] if with_norm else [])
    out = pl.pallas_call(
        _addnorm_kernel,
        grid=(t // bt,),
        in_specs=[row, pl.BlockSpec((d, bt), lambda i: (0, i)), pl.BlockSpec((1, d), lambda i: (0, 0))],
        out_specs=out_specs,
        out_shape=out_shape,
        compiler_params=_cparams("parallel"),
        name="add_norm" if with_norm else "add_update",
    )(x, p_t, g.reshape(1, d).astype(F32))
    return (out[0], out[1]) if with_norm else (out[0], None)


def _cast_kernel(w_ref, o_ref, *, transpose):
    w = w_ref[0]
    o_ref[...] = (w.T if transpose else w).astype(o_ref.dtype)


def _cast_bf16(w, layer, col0=0, ncols=None, transpose=False, br=512, bc=4096):
    _, r, c = w.shape
    ncols = c - col0 if ncols is None else ncols
    bc = math.gcd(math.gcd(bc, ncols), col0) if col0 else math.gcd(bc, ncols)
    br = min(br, r)
    off = col0 // bc
    if transpose:
        out_spec = pl.BlockSpec((bc, br), lambda i, j: (j, i))
        out_shape = jax.ShapeDtypeStruct((ncols, r), BF16)
    else:
        out_spec = pl.BlockSpec((br, bc), lambda i, j: (i, j))
        out_shape = jax.ShapeDtypeStruct((r, ncols), BF16)
    return pl.pallas_call(
        functools.partial(_cast_kernel, transpose=transpose),
        grid=(r // br, ncols // bc),
        in_specs=[pl.BlockSpec((1, br, bc), lambda i, j: (layer, i, off + j))],
        out_specs=out_spec,
        out_shape=out_shape,
        compiler_params=_cparams("parallel", "parallel"),
        name="cast_bf16_t" if transpose else "cast_bf16",
    )(w)


def _mm_kernel(*refs, n_lhs, has_resid, tile_major):
    a_refs, b_refs = refs[:n_lhs], refs[n_lhs:2 * n_lhs]
    r_ref = refs[2 * n_lhs] if has_resid else None
    o_ref = refs[-1]
    acc = _dot(a_refs[0][...], b_refs[0][...])
    for a_ref, b_ref in zip(a_refs[1:], b_refs[1:]):
        acc = acc + _dot(a_ref[...], b_ref[...])
    if has_resid:
        acc = acc + r_ref[...]
    if tile_major:
        for jj in range(o_ref.shape[0]):
            o_ref[jj] = acc[:, jj * LANES:(jj + 1) * LANES].astype(o_ref.dtype)
    else:
        o_ref[...] = acc.astype(o_ref.dtype)


def _matmul(lhs, rhs, *, resid=None, out_dtype=BF16, bm=512, bn=1024, tile_major=False, name="mm"):
    m, n = lhs[0].shape[0], rhs[0].shape[1]
    bm, bn = min(bm, m), min(bn, n)
    in_specs = [pl.BlockSpec((bm, a.shape[1]), lambda i, j: (i, 0)) for a in lhs]
    in_specs += [pl.BlockSpec((b.shape[0], bn), lambda i, j: (0, j)) for b in rhs]
    args = list(lhs) + list(rhs)
    if resid is not None:
        in_specs.append(pl.BlockSpec((bm, bn), lambda i, j: (i, j)))
        args.append(resid)
    if tile_major:
        out_spec = pl.BlockSpec((bn // LANES, bm, LANES), lambda i, j: (j, i, 0))
        out_shape = jax.ShapeDtypeStruct((n // LANES, m, LANES), out_dtype)
    else:
        out_spec = pl.BlockSpec((bm, bn), lambda i, j: (i, j))
        out_shape = jax.ShapeDtypeStruct((m, n), out_dtype)
    kern = functools.partial(_mm_kernel, n_lhs=len(lhs), has_resid=resid is not None, tile_major=tile_major)
    return pl.pallas_call(
        kern, grid=(m // bm, n // bn), in_specs=in_specs, out_specs=out_spec, out_shape=out_shape,
        compiler_params=_cparams("parallel", "parallel"), name=name,
    )(*args)


def _mm_nt_kernel(w_ref, h_ref, o_ref):
    o_ref[...] = _dot_nt(w_ref[...], h_ref[...]).astype(o_ref.dtype)


def _matmul_nt(w_t, h, *, bm=512, bn=1024, out_dtype=BF16, name="mm_nt"):
    n, k = w_t.shape
    m = h.shape[0]
    bm, bn = min(bm, m), min(bn, n)
    return pl.pallas_call(
        _mm_nt_kernel,
        grid=(m // bm, n // bn),
        in_specs=[pl.BlockSpec((bn, k), lambda i, j: (j, 0)), pl.BlockSpec((bm, k), lambda i, j: (i, 0))],
        out_specs=pl.BlockSpec((bn, bm), lambda i, j: (j, i)),
        out_shape=jax.ShapeDtypeStruct((n, m), out_dtype),
        compiler_params=_cparams("parallel", "parallel"),
        name=name,
    )(w_t, h)


def _s5_tables(log_step, a_re, a_im, b_re, b_im, c_re, c_im, d_skip):
    L = S5_CHUNK
    g, p = a_re.shape
    gt = LANES // S5_GROUP
    nt = g // gt
    hp = lax.Precision.HIGHEST
    step = jnp.exp(log_step)[:, None]
    lr, li = a_re, a_im
    mag = jnp.exp(lr * step)
    abar_re = mag * jnp.cos(li * step)
    abar_im = mag * jnp.sin(li * step)
    den = lr * lr + li * li
    num_re = abar_re - 1.0
    f_re = (num_re * lr + abar_im * li) / den
    f_im = (abar_im * lr - num_re * li) / den
    bb_re = f_re[..., None] * b_re - f_im[..., None] * b_im
    bb_im = f_re[..., None] * b_im + f_im[..., None] * b_re
    k = jnp.arange(L + 1, dtype=F32)[:, None, None]
    pmag = jnp.exp(k * (lr * step)[None])
    pang = k * (li * step)[None]
    pw_re = pmag * jnp.cos(pang)
    pw_im = pmag * jnp.sin(pang)
    ns = 2 * gt * p

    def same_group(row_g, col_g):
        return (row_g[:, None] == col_g[None, :]).astype(F32)

    cp_re = c_re[None] * pw_re[:L, :, None, :] - c_im[None] * pw_im[:L, :, None, :]
    cp_im = c_re[None] * pw_im[:L, :, None, :] + c_im[None] * pw_re[:L, :, None, :]
    kk = (jnp.einsum("dgop,gpi->dgoi", cp_re, bb_re, precision=hp)
          - jnp.einsum("dgop,gpi->dgoi", cp_im, bb_im, precision=hp))
    kki = kk.transpose(0, 1, 3, 2).reshape(L, nt, LANES, S5_GROUP)
    lane_g = jnp.arange(LANES) // S5_GROUP
    bd = jnp.tile(kki, (1, 1, 1, gt)) * same_group(lane_g, lane_g)
    bd = bd.transpose(1, 0, 2, 3).astype(BF16)

    k_rev = (L - 1) - k[:L]
    pmag_rev = jnp.exp(k_rev * (lr * step)[None])
    pang_rev = k_rev * (li * step)[None]
    pwr, pwi = pmag_rev * jnp.cos(pang_rev), pmag_rev * jnp.sin(pang_rev)
    pb_re = pwr[..., None] * bb_re[None] - pwi[..., None] * bb_im[None]
    pb_im = pwr[..., None] * bb_im[None] + pwi[..., None] * bb_re[None]
    pb = jnp.stack([pb_re, pb_im], 0).reshape(2, L, nt, gt, p, S5_GROUP)
    ps = pb.transpose(2, 1, 3, 5, 0, 4).reshape(nt, L * LANES, 2 * p).astype(BF16)
    rr, cc = jnp.arange(2 * p), jnp.arange(ns)
    ex_p = ((rr[:, None] // p == cc[None, :] // (gt * p)) & (rr[:, None] % p == cc[None, :] % p))
    row_g = (jnp.arange(L * LANES) % LANES) // S5_GROUP
    col_g = (cc % (gt * p)) // p
    p_parts = (ps, ex_p.astype(BF16), same_group(row_g, col_g).astype(BF16))

    qr, qi = pw_re[1:L + 1], pw_im[1:L + 1]
    cn_re = c_re[None] * qr[:, :, None, :] - c_im[None] * qi[:, :, None, :]
    cn_im = c_re[None] * qi[:, :, None, :] + c_im[None] * qr[:, :, None, :]
    nb = jnp.stack([cn_re, -cn_im], 0).reshape(2, L, nt, gt, S5_GROUP, p)
    nsm = nb.transpose(2, 0, 3, 5, 1, 4).reshape(nt, ns, L * S5_GROUP).astype(BF16)
    rr, cc = jnp.arange(L * S5_GROUP), jnp.arange(L * LANES)
    ex_n = ((rr[:, None] // S5_GROUP == cc[None, :] // LANES)
            & (rr[:, None] % S5_GROUP == cc[None, :] % S5_GROUP))
    row_g = (jnp.arange(ns) % (gt * p)) // p
    col_g = (cc % LANES) // S5_GROUP
    n_parts = (nsm, ex_n.astype(BF16), same_group(row_g, col_g).astype(BF16))

    a_l = jnp.stack([pw_re[L], pw_im[L]], 0).reshape(2, nt, gt * p).transpose(1, 0, 2)
    d_t = jnp.tile(d_skip.reshape(nt, 1, LANES), (1, 1, L))
    return bd, p_parts, n_parts, a_l, d_t


def _s5_kernel(x_ref, bd_ref, ps_ref, exp_ref, mkp_ref, ns_ref, exn_ref, mkn_ref, al_ref, d_ref, o_ref,
               m_ref, sloc_ref, sprev_ref):
    x = x_ref[0]
    nc, ns = sloc_ref.shape
    half = ns // 2
    L = bd_ref.shape[1]
    p_op = _dot(ps_ref[0], exp_ref[...]).astype(BF16) * mkp_ref[...]
    n_op = _dot(ns_ref[0], exn_ref[...]).astype(BF16) * mkn_ref[...]
    for tau in range(L):
        for t in range(L):
            blk = bd_ref[0, t - tau] if t >= tau else jnp.zeros((LANES, LANES), m_ref.dtype)
            m_ref[tau * LANES:(tau + 1) * LANES, t * LANES:(t + 1) * LANES] = blk
    sloc_ref[...] = _dot(x, p_op)
    a_re = al_ref[0, 0:1, :]
    a_im = al_ref[0, 1:2, :]

    def body(c, carry):
        s_re, s_im = carry
        row = pl.ds(c, 1)
        sprev_ref[row, 0:half] = s_re
        sprev_ref[row, half:ns] = s_im
        l_re = sloc_ref[row, 0:half]
        l_im = sloc_ref[row, half:ns]
        return (a_re * s_re - a_im * s_im + l_re, a_re * s_im + a_im * s_re + l_im)

    zero = jnp.zeros((1, half), F32)
    lax.fori_loop(0, nc, body, (zero, zero))
    y = _dot(x, m_ref[...]) + _dot(sprev_ref[...].astype(BF16), n_op)
    y = y + d_ref[0] * x.astype(F32)
    o_ref[0] = _gelu(y).astype(o_ref.dtype)


def _s5_core(u_t, tables):
    bd, (ps, ex_p, mk_p), (nsm, ex_n, mk_n), a_l, d_t = tables
    nt, t, _ = u_t.shape
    L = S5_CHUNK
    nc, w, ns = t // L, L * LANES, mk_p.shape[1]
    x = u_t.reshape(nt, nc, w)
    per_tile = lambda a: pl.BlockSpec((1,) + a.shape[1:], lambda j: (j,) + (0,) * (a.ndim - 1))
    shared = lambda a: pl.BlockSpec(a.shape, lambda j: (0,) * a.ndim, pipeline_mode=pl.Buffered(1))
    out = pl.pallas_call(
        _s5_kernel,
        grid=(nt,),
        in_specs=[
            pl.BlockSpec((1, nc, w), lambda j: (j, 0, 0)),
            per_tile(bd),
            per_tile(ps), shared(ex_p), shared(mk_p),
            per_tile(nsm), shared(ex_n), shared(mk_n),
            per_tile(a_l),
            per_tile(d_t),
        ],
        out_specs=pl.BlockSpec((1, nc, w), lambda j: (j, 0, 0)),
        out_shape=jax.ShapeDtypeStruct((nt, nc, w), BF16),
        scratch_shapes=[pltpu.VMEM((w, w), BF16), pltpu.VMEM((nc, ns), F32), pltpu.VMEM((nc, ns), F32)],
        compiler_params=_cparams("parallel"),
        name="s5_core",
    )(x, bd, ps, ex_p, mk_p, nsm, ex_n, mk_n, a_l, d_t)
    return out.reshape(nt, t, LANES)


def _glu_kernel(y_ref, w_ref, b_ref, o_ref):
    y = jnp.concatenate([y_ref[j] for j in range(y_ref.shape[0])], axis=1)
    acc = _dot(y, w_ref[...]) + b_ref[...]
    o_ref[...] = (y.astype(F32) * jax.nn.sigmoid(acc)).astype(o_ref.dtype)


def _glu(y_t, w, b, bm=512):
    nt, t, _ = y_t.shape
    width = nt * LANES
    return pl.pallas_call(
        _glu_kernel,
        grid=(t // bm,),
        in_specs=[
            pl.BlockSpec((nt, bm, LANES), lambda i: (0, i, 0)),
            pl.BlockSpec((width, width), lambda i: (0, 0)),
            pl.BlockSpec((1, width), lambda i: (0, 0)),
        ],
        out_specs=pl.BlockSpec((bm, width), lambda i: (i, 0)),
        out_shape=jax.ShapeDtypeStruct((t, width), BF16),
        compiler_params=_cparams("parallel"),
        name="s5_glu",
    )(y_t, w, b.reshape(1, width).astype(F32))


def _ret_kernel(q_ref, k_ref, v_ref, g_ref, cos_ref, sin_ref, intra_ref, qd_ref, kd_ref, cd_ref,
                gg_ref, gb_ref, o_ref, state_ref, *, chunk, scale):
    @pl.when(pl.program_id(1) == 0)
    def _():
        state_ref[...] = jnp.zeros_like(state_ref)

    half = cos_ref.shape[1]
    hd = 2 * half
    nhs = state_ref.shape[0]

    def body(c, carry):
        rows = pl.ds(pl.multiple_of(c * chunk, chunk), chunk)
        cos, sin = cos_ref[rows, :], sin_ref[rows, :]

        def rot(x):
            x1, x2 = x[:, :half], x[:, half:]
            return jnp.concatenate([x1 * cos - x2 * sin, x1 * sin + x2 * cos], axis=1)

        for hh in range(nhs):
            cols = slice(hh * hd, (hh + 1) * hd)
            q = rot(q_ref[rows, cols].astype(F32))
            k = rot(k_ref[rows, cols].astype(F32)) * scale
            v = v_ref[rows, cols]
            st = state_ref[hh]
            scores = _dot_nt(q.astype(BF16), k.astype(BF16)) * intra_ref[hh]
            out = _dot(scores.astype(BF16), v) + _dot((q * qd_ref[hh]).astype(BF16), st.astype(BF16))
            state_ref[hh] = cd_ref[hh] * st + _dot_tn((k * kd_ref[hh]).astype(BF16), v)
            mu = jnp.mean(out, axis=-1, keepdims=True)
            cen = out - mu
            var = jnp.mean(cen * cen, axis=-1, keepdims=True)
            o = cen * lax.rsqrt(var + EPS) * gg_ref[:, cols] + gb_ref[:, cols]
            gt = g_ref[rows, cols].astype(F32)
            o_ref[rows, cols] = (gt * jax.nn.sigmoid(gt) * o).astype(o_ref.dtype)
        return carry

    lax.fori_loop(0, q_ref.shape[0] // chunk, body, 0)


def _retention(z4, gn_g, gn_b, tb=2048):
    t = z4.shape[0]
    w = z4.shape[1] // 4
    hd = w // RET_HEADS
    half = hd // 2
    c = RET_CHUNK
    tb = min(tb, t)
    pos = jnp.arange(t, dtype=F32)
    freqs = ROPE_BASE ** (-jnp.arange(half, dtype=F32) / half)
    ang = pos[:, None] * freqs[None, :]
    cos, sin = jnp.cos(ang), jnp.sin(ang)
    gamma = 1.0 - 2.0 ** (-5.0 - jnp.arange(RET_HEADS, dtype=F32))
    log_g = jnp.log(gamma)
    idx = jnp.arange(c, dtype=F32)
    rel = idx[:, None] - idx[None, :]
    intra = jnp.where(rel >= 0, jnp.exp(log_g[:, None, None] * jnp.maximum(rel, 0.0)), 0.0)
    q_decay = jnp.exp(log_g[:, None] * (idx + 1.0))[..., None]
    k_decay = jnp.exp(log_g[:, None] * (c - 1.0 - idx))[..., None]
    chunk_decay = jnp.exp(log_g * c)[:, None, None]
    nhs = RET_HEADS_PER_STEP
    ng = RET_HEADS // nhs
    blk = lambda off: pl.BlockSpec((tb, nhs * hd), lambda h, s: (s, off + h))
    per_head = lambda shape: pl.BlockSpec((nhs,) + shape, lambda h, s: (h, 0, 0))
    return pl.pallas_call(
        functools.partial(_ret_kernel, chunk=c, scale=hd ** -0.5),
        grid=(ng, t // tb),
        in_specs=[
            blk(0), blk(ng), blk(2 * ng), blk(3 * ng),
            pl.BlockSpec((tb, half), lambda h, s: (s, 0)),
            pl.BlockSpec((tb, half), lambda h, s: (s, 0)),
            per_head((c, c)), per_head((c, 1)), per_head((c, 1)), per_head((1, 1)),
            pl.BlockSpec((1, nhs * hd), lambda h, s: (0, h)),
            pl.BlockSpec((1, nhs * hd), lambda h, s: (0, h)),
        ],
        out_specs=pl.BlockSpec((tb, nhs * hd), lambda h, s: (s, h)),
        out_shape=jax.ShapeDtypeStruct((t, w), BF16),
        scratch_shapes=[pltpu.VMEM((nhs, hd, hd), F32)],
        compiler_params=_cparams("parallel", "arbitrary"),
        name="retention",
    )(z4, z4, z4, z4, cos, sin, intra, q_decay, k_decay, chunk_decay,
      gn_g.reshape(1, w).astype(F32), gn_b.reshape(1, w).astype(F32))


def _qknorm_kernel(x_ref, gain_ref, ones_ref, o_ref, *, inv_dim, transpose_out):
    x = x_ref[...].astype(F32)
    xx = x * x
    hi = xx.astype(BF16)
    lo = (xx - hi.astype(F32)).astype(BF16)
    ss = _dot(hi, ones_ref[...]) + _dot(lo, ones_ref[...])
    y = x * lax.rsqrt(ss * inv_dim + EPS) * gain_ref[...]
    o_ref[...] = (y.T if transpose_out else y).astype(o_ref.dtype)


def _qknorm(z, gain, col_off, transpose_out, bt=1024, bw=256):
    t = z.shape[0]
    w = gain.shape[0]
    bt = min(bt, t)
    off = col_off // bw
    grp = jnp.arange(bw) // DIFF_HEAD_DIM
    ones = (grp[:, None] == grp[None, :]).astype(BF16)
    if transpose_out:
        out_spec = pl.BlockSpec((bw, bt), lambda i, j: (j, i))
        out_shape = jax.ShapeDtypeStruct((w, t), BF16)
    else:
        out_spec = pl.BlockSpec((bt, bw), lambda i, j: (i, j))
        out_shape = jax.ShapeDtypeStruct((t, w), BF16)
    return pl.pallas_call(
        functools.partial(_qknorm_kernel, inv_dim=1.0 / DIFF_HEAD_DIM, transpose_out=transpose_out),
        grid=(t // bt, w // bw),
        in_specs=[
            pl.BlockSpec((bt, bw), lambda i, j: (i, off + j)),
            pl.BlockSpec((1, bw), lambda i, j: (0, j)),
            pl.BlockSpec((bw, bw), lambda i, j: (0, 0)),
        ],
        out_specs=out_spec,
        out_shape=out_shape,
        compiler_params=_cparams("parallel", "parallel"),
        name="q_norm_t" if transpose_out else "k_norm",
    )(z, gain.reshape(1, w).astype(F32), ones)


def _attn_kernel(qt_ref, k_ref, vt_ref, d0_ref, d1_ref, lam_ref, sg_ref, o_ref, m_ref, acc_ref, va_ref,
                 *, blk, out_scale):
    i = pl.program_id(1)
    hd = DIFF_HEAD_DIM
    dv = 2 * hd
    nhs = va_ref.shape[0]

    @pl.when(i == 0)
    def _():
        for hh in range(nhs):
            for jb in range(va_ref.shape[1]):
                va_ref[hh, jb, 0:dv, :] = vt_ref[hh * dv:(hh + 1) * dv, jb * blk:(jb + 1) * blk]
                va_ref[hh, jb, dv:, :] = jnp.ones((va_ref.shape[2] - dv, blk), va_ref.dtype)

    m_ref[...] = jnp.full(m_ref.shape, -1e30, F32)
    acc_ref[...] = jnp.zeros_like(acc_ref)

    def step(j, bias_refs):
        nsub = len(bias_refs)
        rows = pl.ds(pl.multiple_of(j * blk, blk), nsub * blk)
        for hh in range(nhs):
            for mi in range(2):
                c = 2 * hh + mi
                s = _dot(k_ref[rows, c * hd:(c + 1) * hd], qt_ref[c * hd:(c + 1) * hd, :])
                if bias_refs[0] is not None:
                    s = s + jnp.concatenate([b[hh] for b in bias_refs], axis=0)
                m_old = m_ref[c]
                m_new = jnp.maximum(m_old, jnp.max(s, axis=0, keepdims=True))
                p = jnp.exp2(s - m_new).astype(BF16)
                alpha = jnp.exp2(m_old - m_new)
                pv = _dot(va_ref[hh, j], p[0:blk])
                for u in range(1, nsub):
                    pv = pv + _dot(va_ref[hh, j + u], p[u * blk:(u + 1) * blk])
                acc_ref[c] = alpha * acc_ref[c] + pv
                m_ref[c] = m_new

    n_far = jnp.maximum(i - 1, 0)
    done = 0
    for grp in ATTN_FAR_GROUPS:
        n_grp = (n_far - done) // grp

        def far_group(gi, carry, grp=grp, done=done):
            step(done + grp * gi, (None,) * grp)
            return carry

        lax.fori_loop(0, n_grp, far_group, 0)
        done = done + n_grp * grp

    @pl.when(i >= 1)
    def _():
        step(i - 1, (d1_ref, d0_ref))

    @pl.when(i == 0)
    def _():
        step(0, (d0_ref,))
    for hh in range(nhs):
        a0, a1 = acc_ref[2 * hh], acc_ref[2 * hh + 1]
        o_t = a0[:dv] / a0[dv:dv + 1] - lam_ref[...] * (a1[:dv] / a1[dv:dv + 1])
        o = o_t.T
        ms = jnp.mean(o * o, axis=-1, keepdims=True)
        o_ref[:, hh * dv:(hh + 1) * dv] = (o * lax.rsqrt(ms + EPS) * sg_ref[...] * out_scale).astype(o_ref.dtype)


def _t5_bucket(n):
    max_exact = REL_BUCKETS // 2
    nf = jnp.maximum(n, 1).astype(F32)
    large = max_exact + (jnp.log(nf / max_exact) / math.log(REL_MAX_DIST / max_exact)
                         * (REL_BUCKETS - max_exact)).astype(jnp.int32)
    large = jnp.minimum(large, REL_BUCKETS - 1)
    return jnp.where(n < max_exact, n, large)


def _toeplitz_kernel(w_ref, d0_ref, d1_ref, *, blk):
    for which, o_ref in enumerate((d0_ref, d1_ref)):
        rows = jnp.broadcast_to(w_ref[0, which:which + 1, :], (blk, 2 * blk))
        o_ref[0] = pltpu.roll(rows, blk + 1, 1, stride=1, stride_axis=0)[:, :blk]


def _toeplitz_tiles(w, blk):
    nh = w.shape[0]
    tile = pl.BlockSpec((1, blk, blk), lambda h: (h, 0, 0))
    return pl.pallas_call(
        functools.partial(_toeplitz_kernel, blk=blk),
        grid=(nh,),
        in_specs=[pl.BlockSpec((1, 2, 2 * blk), lambda h: (h, 0, 0))],
        out_specs=[tile, tile],
        out_shape=[jax.ShapeDtypeStruct((nh, blk, blk), F32)] * 2,
        compiler_params=_cparams("parallel"),
        name="bias_tiles",
    )(w)


def _diff_attention(q_t, kn, v_t, rel_bias, lam, sub_g, lambda_init):
    t = kn.shape[0]
    nh = DIFF_HEADS
    dv = 2 * DIFF_HEAD_DIM
    ones_rows = 16
    blk = min(ATTN_BLOCK, t)
    nhs = ATTN_HEADS_PER_STEP
    assert blk >= REL_MAX_DIST
    far_bias = rel_bias[REL_BUCKETS - 1]

    r = jnp.arange(2 * blk)
    shifted = lambda rel: ((rel_bias[_t5_bucket(jnp.maximum(rel, 0))] - far_bias) * LOG2E).T
    rel_d = r - (blk - 1)
    w0 = jnp.where((rel_d >= 0)[None, :], shifted(rel_d), -jnp.inf)
    w1 = shifted(r + 1)
    d0, d1 = _toeplitz_tiles(jnp.stack([w0, w1], 1).astype(F32), blk)
    once = pl.Buffered(1)
    return pl.pallas_call(
        functools.partial(_attn_kernel, blk=blk, out_scale=1.0 - lambda_init),
        grid=(nh // nhs, t // blk),
        in_specs=[
            pl.BlockSpec((nhs * dv, blk), lambda h, i: (h, i)),
            pl.BlockSpec((t, nhs * dv), lambda h, i: (0, h), pipeline_mode=once),
            pl.BlockSpec((nhs * dv, t), lambda h, i: (h, 0), pipeline_mode=once),
            pl.BlockSpec((nhs, blk, blk), lambda h, i: (h, 0, 0), pipeline_mode=once),
            pl.BlockSpec((nhs, blk, blk), lambda h, i: (h, 0, 0), pipeline_mode=once),
            pl.BlockSpec((1, 1), lambda h, i: (0, 0)),
            pl.BlockSpec((1, dv), lambda h, i: (0, 0)),
        ],
        out_specs=pl.BlockSpec((blk, nhs * dv), lambda h, i: (i, h)),
        out_shape=jax.ShapeDtypeStruct((t, nh * dv), BF16),
        scratch_shapes=[pltpu.VMEM((2 * nhs, 1, blk), F32), pltpu.VMEM((2 * nhs, dv + ones_rows, blk), F32),
                        pltpu.VMEM((nhs, t // blk, dv + ones_rows, blk), BF16)],
        compiler_params=_cparams("parallel", "arbitrary"),
        name="diff_attn",
    )(q_t, kn, v_t, d0, d1, jnp.reshape(lam, (1, 1)).astype(F32), sub_g.reshape(1, dv).astype(F32))


def _fold_keys_kernel(sk_ref, w_ref, o_ref):
    w = w_ref[0]
    for b in range(sk_ref.shape[0]):
        cols = slice(b * LANES, (b + 1) * LANES)
        o_ref[cols, :] = _dot_nt(sk_ref[b], w[:, cols]).astype(o_ref.dtype)


def _fold_keys(w_qs, layer, sub_keys, bd=1024, nb=4):
    _, d, n = w_qs.shape
    return pl.pallas_call(
        _fold_keys_kernel,
        grid=(d // bd, n // (nb * LANES)),
        in_specs=[
            pl.BlockSpec((nb, LANES, LANES), lambda i, j: (j, 0, 0)),
            pl.BlockSpec((1, bd, nb * LANES), lambda i, j: (layer, i, j)),
        ],
        out_specs=pl.BlockSpec((nb * LANES, bd), lambda i, j: (j, i)),
        out_shape=jax.ShapeDtypeStruct((n, d), BF16),
        compiler_params=_cparams("parallel", "parallel"),
        name="peer_fold_keys",
    )(sub_keys, w_qs)


def _top_values(s, n):
    vals = []
    for r in range(n):
        m = jnp.max(s, axis=0, keepdims=True)
        vals.append(m)
        if r < n - 1:
            s = jnp.where(s == m, -jnp.inf, s)
    return vals


def _route_kernel(s_ref, e1_ref, e2_ref, c_ref):
    nk = PEER_N_KEYS
    k = PEER_TOPK
    tl = s_ref.shape[1]
    pad = 24
    row = lax.broadcasted_iota(jnp.int32, (pad, tl), 0)
    hrow = lax.broadcasted_iota(jnp.int32, (PEER_HEADS, tl), 0)
    c_all = jnp.zeros((PEER_HEADS, tl), F32)
    for h in range(PEER_HEADS):
        s1 = s_ref[2 * h * nk:(2 * h + 1) * nk, :]
        s2 = s_ref[(2 * h + 1) * nk:(2 * h + 2) * nk, :]
        a = _top_values(s1, k + 1)
        b = _top_values(s2, k + 1)
        ea = [jnp.exp(x - a[0]) for x in a]
        eb = [jnp.exp(x - b[0]) for x in b]
        ea_m = jnp.full((pad, tl), -1.0, F32)
        eb_m = jnp.full((pad, tl), -1.0, F32)
        for r in range(k + 1):
            ea_m = jnp.where(row == r, ea[r], ea_m)
            eb_m = jnp.where(row == r, eb[r], eb_m)
        cand = jnp.concatenate(
            [ea[0] * eb_m, ea[1] * eb_m[:8], ea[2] * eb_m[:8], ea[3] * eb_m[:8],
             eb[0] * ea_m, eb[1] * ea_m[:8], eb[2] * ea_m[:8]], axis=0)
        v = _top_values(cand, k + 1)
        z = v[0]
        for r in range(1, k):
            z = z + v[r]
        rz = 1.0 / z
        e1_ref[h] = jnp.exp(s1 - a[0])
        e2_ref[h] = (jnp.exp(s2 - b[0]) * rz).astype(e2_ref.dtype)
        c_all = jnp.where(hrow == h, 0.5 * (v[k - 1] + v[k]) * rz, c_all)
    c_ref[...] = c_all


def _peer_route(s_t, tl=256):
    n, t = s_t.shape
    tl = min(tl, t)
    nh, nk = PEER_HEADS, PEER_N_KEYS
    return pl.pallas_call(
        _route_kernel,
        grid=(t // tl,),
        in_specs=[pl.BlockSpec((n, tl), lambda i: (0, i))],
        out_specs=[
            pl.BlockSpec((nh, nk, tl), lambda i: (0, 0, i)),
            pl.BlockSpec((nh, nk, tl), lambda i: (0, 0, i)),
            pl.BlockSpec((nh, tl), lambda i: (0, i)),
        ],
        out_shape=[
            jax.ShapeDtypeStruct((nh, nk, t), F32),
            jax.ShapeDtypeStruct((nh, nk, t), BF16),
            jax.ShapeDtypeStruct((nh, t), F32),
        ],
        compiler_params=_cparams("parallel"),
        name="peer_route",
    )(s_t)


def _peer_dense_kernel(u_ref, vt_ref, h_ref, e1_ref, e2_ref, c_ref, o_ref, w_ref):
    e = pl.program_id(1)

    @pl.when(e == 0)
    def _():
        o_ref[...] = jnp.zeros_like(o_ref)

    a_t = _dot_nt(u_ref[...], h_ref[...])
    tm = a_t.shape[1]
    nslab = a_t.shape[0] // LANES
    gdt = e2_ref.dtype
    for ii in range(nslab):
        rows = slice(ii * LANES, (ii + 1) * LANES)
        g = jnp.zeros((LANES, tm), gdt)
        for h in range(PEER_HEADS):
            vv = e2_ref[h] * e1_ref[h, pl.ds(e * nslab + ii, 1), :].astype(gdt)
            g = g + jnp.where(vv >= c_ref[h:h + 1, :].astype(gdt), vv, jnp.zeros_like(vv))
        w_ref[rows, :] = (g * _gelu(a_t[rows, :].astype(gdt))).astype(w_ref.dtype)
    o_ref[...] += _dot(vt_ref[...], w_ref[...])


def _peer_dense(hq, u_tab, v_tab_t, e1_t, e2, c, tm=512, te=512):
    t, d = hq.shape
    ne = u_tab.shape[0]
    tm = min(tm, t)
    nh, nk = PEER_HEADS, PEER_N_KEYS
    once = pl.Buffered(1)
    return pl.pallas_call(
        _peer_dense_kernel,
        grid=(t // tm, ne // te),
        in_specs=[
            pl.BlockSpec((te, d), lambda i, e: (e, 0)),
            pl.BlockSpec((d, te), lambda i, e: (0, e)),
            pl.BlockSpec((tm, d), lambda i, e: (i, 0), pipeline_mode=once),
            pl.BlockSpec((nh, nk, tm), lambda i, e: (0, 0, i), pipeline_mode=once),
            pl.BlockSpec((nh, nk, tm), lambda i, e: (0, 0, i), pipeline_mode=once),
            pl.BlockSpec((nh, tm), lambda i, e: (0, i)),
        ],
        out_specs=pl.BlockSpec((d, tm), lambda i, e: (0, i)),
        out_shape=jax.ShapeDtypeStruct((d, t), F32),
        scratch_shapes=[pltpu.VMEM((te, tm), BF16)],
        compiler_params=_cparams("parallel", "arbitrary"),
        name="peer_dense",
    )(u_tab, v_tab_t, hq, e1_t, e2, c)


def _peer(x, norm_g, w_qs, sub_keys, u_tabs, v_tabs, layer):
    hq = _rmsnorm(x, norm_g)
    nh, nk = PEER_HEADS, PEER_N_KEYS
    s_t = _matmul_nt(_fold_keys(w_qs, layer, sub_keys.reshape(2 * nh, nk, -1)), hq, out_dtype=F32, name="peer_scores")
    e1, e2, c = _peer_route(s_t)
    return _peer_dense(hq, _cast_bf16(u_tabs, layer), _cast_bf16(v_tabs, layer, transpose=True), e1, e2, c)


def _even_layer(x, h, i, w_ins, log_step, a_re, a_im, b_re, b_im, c_re, c_im, d_skip,
                glu_ws, glu_b, gn_g, gn_b, w_outs):
    s5w = d_skip.shape[0]
    u_t = _matmul([h], [_cast_bf16(w_ins, i, 0, s5w)], tile_major=True, name="ev_in_s5")
    z4 = _matmul([h], [_cast_bf16(w_ins, i, s5w)], name="ev_in_ret")
    tables = _s5_tables(log_step, a_re, a_im, b_re, b_im, c_re, c_im, d_skip)
    ya = _glu(_s5_core(u_t, tables), _cast_bf16(glu_ws, i), glu_b)
    yb = _retention(z4, gn_g, gn_b)
    w_out = _cast_bf16(w_outs, i)
    return _matmul([ya, yb], [w_out[:s5w], w_out[s5w:]], resid=x, out_dtype=F32, name="ev_out")


def _odd_layer(x, h, i, w_ins, q_norm_g, k_norm_g, lq1, lk1, lq2, lk2, sub_norm_g, w_outs,
               rel_bias, lambda_init):
    d = x.shape[1]
    z = _matmul([h], [_cast_bf16(w_ins, i, 0, 2 * d)], name="od_in_qk")
    v_t = _matmul_nt(_cast_bf16(w_ins, i, 2 * d, transpose=True), h, name="od_in_vt")
    reps = d // DIFF_HEAD_DIM
    q_t = _qknorm(z, jnp.tile(q_norm_g, reps) * (DIFF_HEAD_DIM ** -0.5 * LOG2E), 0, True)
    kn = _qknorm(z, jnp.tile(k_norm_g, reps), d, False)
    lam = jnp.exp(jnp.sum(lq1 * lk1)) - jnp.exp(jnp.sum(lq2 * lk2)) + lambda_init
    o = _diff_attention(q_t, kn, v_t, rel_bias, lam, sub_norm_g, lambda_init)
    return _matmul([o], [_cast_bf16(w_outs, i)], resid=x, out_dtype=F32, name="od_out")


def kernel(x, ev_norm_g, ev_w_in, s5_log_step, s5_a_re, s5_a_im, s5_b_re, s5_b_im, s5_c_re, s5_c_im,
           s5_d, s5_glu_w, s5_glu_b, ret_gn_g, ret_gn_b, ev_w_out, od_norm_g, od_w_in, diff_q_norm_g,
           diff_k_norm_g, diff_lq1, diff_lk1, diff_lq2, diff_lk2, diff_sub_norm_g, od_w_out, rel_bias,
           ffn_norm_g, peer_w_q, peer_sub_keys, peer_u, peer_v):
    bsz, t, d = x.shape
    depth = ffn_norm_g.shape[0]
    mixer_norm_g = lambda layer: (ev_norm_g if layer % 2 == 0 else od_norm_g)[layer // 2]
    outs = []
    for b in range(bsz):
        xb = x[b]
        h = _rmsnorm(xb, mixer_norm_g(0))
        for layer in range(depth):
            i = layer // 2
            if layer % 2 == 0:
                xb = _even_layer(xb, h, i, ev_w_in, s5_log_step[i], s5_a_re[i], s5_a_im[i],
                                 s5_b_re[i], s5_b_im[i], s5_c_re[i], s5_c_im[i], s5_d[i], s5_glu_w,
                                 s5_glu_b[i], ret_gn_g[i], ret_gn_b[i], ev_w_out)
            else:
                lambda_init = 0.8 - 0.6 * math.exp(-0.3 * layer)
                xb = _odd_layer(xb, h, i, od_w_in, diff_q_norm_g[i], diff_k_norm_g[i],
                                diff_lq1[i], diff_lk1[i], diff_lq2[i], diff_lk2[i], diff_sub_norm_g[i],
                                od_w_out, rel_bias, lambda_init)
            p_t = _peer(xb, ffn_norm_g[layer], peer_w_q, peer_sub_keys[layer], peer_u, peer_v, layer)
            xb, h = _add_norm(xb, p_t, mixer_norm_g(layer + 1) if layer + 1 < depth else None)
        outs.append(xb)
    return jnp.stack(outs, 0)
```

```python
import functools
import math

import jax
import jax.numpy as jnp
from jax import lax
from jax.experimental import pallas as pl
from jax.experimental.pallas import tpu as pltpu

F32 = jnp.float32
BF16 = jnp.bfloat16

EPS = 1e-6
LOG2E = 1.4426950408889634
LANES = 128
VMEM_LIMIT_BYTES = 56 * 1024 * 1024

S5_GROUP = 16
S5_STATE = 64
S5_CHUNK = 16
RET_HEADS = 8
RET_CHUNK = 128
RET_HEADS_PER_STEP = 2
ROPE_BASE = 10000.0
DIFF_HEADS = 16
DIFF_HEAD_DIM = 128
ATTN_BLOCK = 512
ATTN_HEADS_PER_STEP = 2
ATTN_FAR_GROUPS = (4, 2, 1)
REL_BUCKETS = 32
REL_MAX_DIST = 128
PEER_HEADS = 8
PEER_N_KEYS = 128
PEER_TOPK = 16


def _cparams(*sem, flags=None):
    return pltpu.CompilerParams(dimension_semantics=sem, vmem_limit_bytes=VMEM_LIMIT_BYTES, flags=flags)


def _dot(a, b):
    return jnp.dot(a, b, preferred_element_type=F32)


def _dot_nt(a, b):
    return lax.dot_general(a, b, (((1,), (1,)), ((), ())), preferred_element_type=F32)


def _dot_tn(a, b):
    return lax.dot_general(a, b, (((0,), (0,)), ((), ())), preferred_element_type=F32)


def _gelu(x):
    return 0.5 * x * (1.0 + jnp.tanh(0.7978845608028654 * (x + 0.044715 * (x * x * x))))


def _norm_kernel(x_ref, g_ref, o_ref):
    x = x_ref[...]
    r = lax.rsqrt(jnp.mean(x * x, axis=-1, keepdims=True) + EPS)
    o_ref[...] = (x * r * g_ref[...]).astype(o_ref.dtype)


def _rmsnorm(x, g, bt=256):
    t, d = x.shape
    return pl.pallas_call(
        _norm_kernel,
        grid=(t // bt,),
        in_specs=[pl.BlockSpec((bt, d), lambda i: (i, 0)), pl.BlockSpec((1, d), lambda i: (0, 0))],
        out_specs=pl.BlockSpec((bt, d), lambda i: (i, 0)),
        out_shape=jax.ShapeDtypeStruct((t, d), BF16),
        compiler_params=_cparams("parallel"),
        name="rmsnorm",
    )(x, g.reshape(1, d).astype(F32))


def _addnorm_kernel(x_ref, pt_ref, g_ref, xo_ref, h_ref=None):
    x = x_ref[...] + pt_ref[...].T
    xo_ref[...] = x
    if h_ref is not None:
        r = lax.rsqrt(jnp.mean(x * x, axis=-1, keepdims=True) + EPS)
        h_ref[...] = (x * r * g_ref[...]).astype(h_ref.dtype)


def _add_norm(x, p_t, g=None, bt=256):
    t, d = x.shape
    with_norm = g is not None
    row = pl.BlockSpec((bt, d), lambda i: (i, 0))
    g = jnp.ones((d,), F32) if g is None else g
    out_specs = [row, row] if with_norm else [row]
    out_shape = [jax.ShapeDtypeStruct((t, d), F32)] + ([jax.ShapeDtypeStruct((t, d), BF16)] if with_norm else [])
    out = pl.pallas_call(
        _addnorm_kernel,
        grid=(t // bt,),
        in_specs=[row, pl.BlockSpec((d, bt), lambda i: (0, i)), pl.BlockSpec((1, d), lambda i: (0, 0))],
        out_specs=out_specs,
        out_shape=out_shape,
        compiler_params=_cparams("parallel"),
        name="add_norm" if with_norm else "add_update",
    )(x, p_t, g.reshape(1, d).astype(F32))
    return (out[0], out[1]) if with_norm else (out[0], None)


def _cast_kernel(w_ref, o_ref, *, transpose):
    w = w_ref[0]
    o_ref[...] = (w.T if transpose else w).astype(o_ref.dtype)


def _cast_bf16(w, layer, col0=0, ncols=None, transpose=False, br=512, bc=4096):
    _, r, c = w.shape
    ncols = c - col0 if ncols is None else ncols
    bc = math.gcd(math.gcd(bc, ncols), col0) if col0 else math.gcd(bc, ncols)
    br = min(br, r)
    off = col0 // bc
    if transpose:
        out_spec = pl.BlockSpec((bc, br), lambda i, j: (j, i))
        out_shape = jax.ShapeDtypeStruct((ncols, r), BF16)
    else:
        out_spec = pl.BlockSpec((br, bc), lambda i, j: (i, j))
        out_shape = jax.ShapeDtypeStruct((r, ncols), BF16)
    return pl.pallas_call(
        functools.partial(_cast_kernel, transpose=transpose),
        grid=(r // br, ncols // bc),
        in_specs=[pl.BlockSpec((1, br, bc), lambda i, j: (layer, i, off + j))],
        out_specs=out_spec,
        out_shape=out_shape,
        compiler_params=_cparams("parallel", "parallel"),
        name="cast_bf16_t" if transpose else "cast_bf16",
    )(w)


def _mm_kernel(*refs, n_lhs, has_resid, tile_major):
    a_refs, b_refs = refs[:n_lhs], refs[n_lhs:2 * n_lhs]
    r_ref = refs[2 * n_lhs] if has_resid else None
    o_ref = refs[-1]
    acc = _dot(a_refs[0][...], b_refs[0][...])
    for a_ref, b_ref in zip(a_refs[1:], b_refs[1:]):
        acc = acc + _dot(a_ref[...], b_ref[...])
    if has_resid:
        acc = acc + r_ref[...]
    if tile_major:
        for jj in range(o_ref.shape[0]):
            o_ref[jj] = acc[:, jj * LANES:(jj + 1) * LANES].astype(o_ref.dtype)
    else:
        o_ref[...] = acc.astype(o_ref.dtype)


def _matmul(lhs, rhs, *, resid=None, out_dtype=BF16, bm=512, bn=1024, tile_major=False, name="mm"):
    m, n = lhs[0].shape[0], rhs[0].shape[1]
    bm, bn = min(bm, m), min(bn, n)
    in_specs = [pl.BlockSpec((bm, a.shape[1]), lambda i, j: (i, 0)) for a in lhs]
    in_specs += [pl.BlockSpec((b.shape[0], bn), lambda i, j: (0, j)) for b in rhs]
    args = list(lhs) + list(rhs)
    if resid is not None:
        in_specs.append(pl.BlockSpec((bm, bn), lambda i, j: (i, j)))
        args.append(resid)
    if tile_major:
        out_spec = pl.BlockSpec((bn // LANES, bm, LANES), lambda i, j: (j, i, 0))
        out_shape = jax.ShapeDtypeStruct((n // LANES, m, LANES), out_dtype)
    else:
        out_spec = pl.BlockSpec((bm, bn), lambda i, j: (i, j))
        out_shape = jax.ShapeDtypeStruct((m, n), out_dtype)
    kern = functools.partial(_mm_kernel, n_lhs=len(lhs), has_resid=resid is not None, tile_major=tile_major)
    return pl.pallas_call(
        kern, grid=(m // bm, n // bn), in_specs=in_specs, out_specs=out_spec, out_shape=out_shape,
        compiler_params=_cparams("parallel", "parallel"), name=name,
    )(*args)


def _mm_nt_kernel(w_ref, h_ref, o_ref):
    o_ref[...] = _dot_nt(w_ref[...], h_ref[...]).astype(o_ref.dtype)


def _matmul_nt(w_t, h, *, bm=512, bn=1024, out_dtype=BF16, name="mm_nt"):
    n, k = w_t.shape
    m = h.shape[0]
    bm, bn = min(bm, m), min(bn, n)
    return pl.pallas_call(
        _mm_nt_kernel,
        grid=(m // bm, n // bn),
        in_specs=[pl.BlockSpec((bn, k), lambda i, j: (j, 0)), pl.BlockSpec((bm, k), lambda i, j: (i, 0))],
        out_specs=pl.BlockSpec((bn, bm), lambda i, j: (j, i)),
        out_shape=jax.ShapeDtypeStruct((n, m), out_dtype),
        compiler_params=_cparams("parallel", "parallel"),
        name=name,
    )(w_t, h)


def _s5_tables(log_step, a_re, a_im, b_re, b_im, c_re, c_im, d_skip):
    L = S5_CHUNK
    g, p = a_re.shape
    gt = LANES // S5_GROUP
    nt = g // gt
    hp = lax.Precision.HIGHEST
    step = jnp.exp(log_step)[:, None]
    lr, li = a_re, a_im
    mag = jnp.exp(lr * step)
    abar_re = mag * jnp.cos(li * step)
    abar_im = mag * jnp.sin(li * step)
    den = lr * lr + li * li
    num_re = abar_re - 1.0
    f_re = (num_re * lr + abar_im * li) / den
    f_im = (abar_im * lr - num_re * li) / den
    bb_re = f_re[..., None] * b_re - f_im[..., None] * b_im
    bb_im = f_re[..., None] * b_im + f_im[..., None] * b_re
    k = jnp.arange(L + 1, dtype=F32)[:, None, None]
    pmag = jnp.exp(k * (lr * step)[None])
    pang = k * (li * step)[None]
    pw_re = pmag * jnp.cos(pang)
    pw_im = pmag * jnp.sin(pang)
    ns = 2 * gt * p

    def same_group(row_g, col_g):
        return (row_g[:, None] == col_g[None, :]).astype(F32)

    pwt_re, pwt_im = pw_re[:L].transpose(1, 2, 0), pw_im[:L].transpose(1, 2, 0)
    ct_re, ct_im = c_re.transpose(0, 2, 1), c_im.transpose(0, 2, 1)
    cpt_re = ct_re[:, :, None, :] * pwt_re[..., None] - ct_im[:, :, None, :] * pwt_im[..., None]
    cpt_im = ct_re[:, :, None, :] * pwt_im[..., None] + ct_im[:, :, None, :] * pwt_re[..., None]
    a_t = jnp.concatenate([cpt_re, -cpt_im], axis=1).reshape(g, 2 * p, L * S5_GROUP)
    b_t = jnp.concatenate([bb_re, bb_im], axis=1).transpose(0, 2, 1)
    kkt = jnp.einsum("gip,gpx->gix", b_t, a_t, precision=hp)
    kki = kkt.reshape(nt, gt, S5_GROUP, L, S5_GROUP).transpose(0, 3, 1, 2, 4)
    kki = kki.reshape(nt, L, LANES, S5_GROUP)
    lane_g = jnp.arange(LANES) // S5_GROUP
    bd = (jnp.tile(kki, (1, 1, 1, gt)) * same_group(lane_g, lane_g)).astype(BF16)

    k_rev = (L - 1) - k[:L]
    pmag_rev = jnp.exp(k_rev * (lr * step)[None])
    pang_rev = k_rev * (li * step)[None]
    pwr, pwi = pmag_rev * jnp.cos(pang_rev), pmag_rev * jnp.sin(pang_rev)
    pb_re = pwr[..., None] * bb_re[None] - pwi[..., None] * bb_im[None]
    pb_im = pwr[..., None] * bb_im[None] + pwi[..., None] * bb_re[None]
    pb = jnp.stack([pb_re, pb_im], 0).reshape(2, L, nt, gt, p, S5_GROUP)
    ps = pb.transpose(2, 1, 3, 5, 0, 4).reshape(nt, L * LANES, 2 * p).astype(BF16)
    rr, cc = jnp.arange(2 * p), jnp.arange(ns)
    ex_p = ((rr[:, None] // p == cc[None, :] // (gt * p)) & (rr[:, None] % p == cc[None, :] % p))
    row_g = (jnp.arange(L * LANES) % LANES) // S5_GROUP
    col_g = (cc % (gt * p)) // p
    p_parts = (ps, ex_p.astype(BF16), same_group(row_g, col_g).astype(BF16))

    qr, qi = pw_re[1:L + 1], pw_im[1:L + 1]
    cn_re = c_re[None] * qr[:, :, None, :] - c_im[None] * qi[:, :, None, :]
    cn_im = c_re[None] * qi[:, :, None, :] + c_im[None] * qr[:, :, None, :]
    nb = jnp.stack([cn_re, -cn_im], 0).reshape(2, L, nt, gt, S5_GROUP, p)
    nsm = nb.transpose(2, 0, 3, 5, 1, 4).reshape(nt, ns, L * S5_GROUP).astype(BF16)
    rr, cc = jnp.arange(L * S5_GROUP), jnp.arange(L * LANES)
    ex_n = ((rr[:, None] // S5_GROUP == cc[None, :] // LANES)
            & (rr[:, None] % S5_GROUP == cc[None, :] % S5_GROUP))
    row_g = (jnp.arange(ns) % (gt * p)) // p
    col_g = (cc % LANES) // S5_GROUP
    n_parts = (nsm, ex_n.astype(BF16), same_group(row_g, col_g).astype(BF16))

    a_l = jnp.stack([pw_re[L], pw_im[L]], 0).reshape(2, nt, gt * p).transpose(1, 0, 2)
    d_t = jnp.tile(d_skip.reshape(nt, 1, LANES), (1, 1, L))
    return bd, p_parts, n_parts, a_l, d_t


def _s5_kernel(x_ref, bd_ref, ps_ref, exp_ref, mkp_ref, ns_ref, exn_ref, mkn_ref, al_ref, d_ref, o_ref,
               m_ref, sloc_ref, sprev_ref):
    x = x_ref[0]
    nc, ns = sloc_ref.shape
    half = ns // 2
    L = bd_ref.shape[1]
    p_op = _dot(ps_ref[0], exp_ref[...]).astype(BF16) * mkp_ref[...]
    n_op = _dot(ns_ref[0], exn_ref[...]).astype(BF16) * mkn_ref[...]
    for tau in range(L):
        for t in range(L):
            blk = bd_ref[0, t - tau] if t >= tau else jnp.zeros((LANES, LANES), m_ref.dtype)
            m_ref[tau * LANES:(tau + 1) * LANES, t * LANES:(t + 1) * LANES] = blk
    sloc_ref[...] = _dot(x, p_op)
    a_re = al_ref[0, 0:1, :]
    a_im = al_ref[0, 1:2, :]

    def body(c, carry):
        s_re, s_im = carry
        row = pl.ds(c, 1)
        sprev_ref[row, 0:half] = s_re
        sprev_ref[row, half:ns] = s_im
        l_re = sloc_ref[row, 0:half]
        l_im = sloc_ref[row, half:ns]
        return (a_re * s_re - a_im * s_im + l_re, a_re * s_im + a_im * s_re + l_im)

    zero = jnp.zeros((1, half), F32)
    lax.fori_loop(0, nc, body, (zero, zero))
    y = _dot(x, m_ref[...]) + _dot(sprev_ref[...].astype(BF16), n_op)
    y = y + d_ref[0] * x.astype(F32)
    o_ref[0] = _gelu(y).astype(o_ref.dtype)


def _s5_core(u_t, tables):
    bd, (ps, ex_p, mk_p), (nsm, ex_n, mk_n), a_l, d_t = tables
    nt, t, _ = u_t.shape
    L = S5_CHUNK
    nc, w, ns = t // L, L * LANES, mk_p.shape[1]
    x = u_t.reshape(nt, nc, w)
    per_tile = lambda a: pl.BlockSpec((1,) + a.shape[1:], lambda j: (j,) + (0,) * (a.ndim - 1))
    shared = lambda a: pl.BlockSpec(a.shape, lambda j: (0,) * a.ndim, pipeline_mode=pl.Buffered(1))
    out = pl.pallas_call(
        _s5_kernel,
        grid=(nt,),
        in_specs=[
            pl.BlockSpec((1, nc, w), lambda j: (j, 0, 0)),
            per_tile(bd),
            per_tile(ps), shared(ex_p), shared(mk_p),
            per_tile(nsm), shared(ex_n), shared(mk_n),
            per_tile(a_l),
            per_tile(d_t),
        ],
        out_specs=pl.BlockSpec((1, nc, w), lambda j: (j, 0, 0)),
        out_shape=jax.ShapeDtypeStruct((nt, nc, w), BF16),
        scratch_shapes=[pltpu.VMEM((w, w), BF16), pltpu.VMEM((nc, ns), F32), pltpu.VMEM((nc, ns), F32)],
        compiler_params=_cparams("parallel"),
        name="s5_core",
    )(x, bd, ps, ex_p, mk_p, nsm, ex_n, mk_n, a_l, d_t)
    return out.reshape(nt, t, LANES)


def _glu_kernel(y_ref, w_ref, b_ref, o_ref):
    y = jnp.concatenate([y_ref[j] for j in range(y_ref.shape[0])], axis=1)
    acc = _dot(y, w_ref[...]) + b_ref[...]
    o_ref[...] = (y.astype(F32) * jax.nn.sigmoid(acc)).astype(o_ref.dtype)


def _glu(y_t, w, b, bm=512):
    nt, t, _ = y_t.shape
    width = nt * LANES
    return pl.pallas_call(
        _glu_kernel,
        grid=(t // bm,),
        in_specs=[
            pl.BlockSpec((nt, bm, LANES), lambda i: (0, i, 0)),
            pl.BlockSpec((width, width), lambda i: (0, 0)),
            pl.BlockSpec((1, width), lambda i: (0, 0)),
        ],
        out_specs=pl.BlockSpec((bm, width), lambda i: (i, 0)),
        out_shape=jax.ShapeDtypeStruct((t, width), BF16),
        compiler_params=_cparams("parallel"),
        name="s5_glu",
    )(y_t, w, b.reshape(1, width).astype(F32))


def _ret_kernel(q_ref, k_ref, v_ref, g_ref, cos_ref, sin_ref, intra_ref, qd_ref, kd_ref, cd_ref,
                gg_ref, gb_ref, o_ref, state_ref, *, chunk, scale):
    @pl.when(pl.program_id(1) == 0)
    def _():
        state_ref[...] = jnp.zeros_like(state_ref)

    half = cos_ref.shape[1]
    hd = 2 * half
    nhs = state_ref.shape[0]

    def body(c, carry):
        rows = pl.ds(pl.multiple_of(c * chunk, chunk), chunk)
        cos, sin = cos_ref[rows, :], sin_ref[rows, :]

        def rot(x):
            x1, x2 = x[:, :half], x[:, half:]
            return jnp.concatenate([x1 * cos - x2 * sin, x1 * sin + x2 * cos], axis=1)

        for hh in range(nhs):
            cols = slice(hh * hd, (hh + 1) * hd)
            q = rot(q_ref[rows, cols].astype(F32))
            k = rot(k_ref[rows, cols].astype(F32)) * scale
            v = v_ref[rows, cols]
            st = state_ref[hh]
            scores = _dot_nt(q.astype(BF16), k.astype(BF16)) * intra_ref[hh]
            out = _dot(scores.astype(BF16), v) + _dot((q * qd_ref[hh]).astype(BF16), st.astype(BF16))
            state_ref[hh] = cd_ref[hh] * st + _dot_tn((k * kd_ref[hh]).astype(BF16), v)
            mu = jnp.mean(out, axis=-1, keepdims=True)
            cen = out - mu
            var = jnp.mean(cen * cen, axis=-1, keepdims=True)
            o = cen * lax.rsqrt(var + EPS) * gg_ref[:, cols] + gb_ref[:, cols]
            gt = g_ref[rows, cols].astype(F32)
            o_ref[rows, cols] = (gt * jax.nn.sigmoid(gt) * o).astype(o_ref.dtype)
        return carry

    lax.fori_loop(0, q_ref.shape[0] // chunk, body, 0)


def _retention(z4, gn_g, gn_b, tb=2048):
    t = z4.shape[0]
    w = z4.shape[1] // 4
    hd = w // RET_HEADS
    half = hd // 2
    c = RET_CHUNK
    tb = min(tb, t)
    pos = jnp.arange(t, dtype=F32)
    freqs = ROPE_BASE ** (-jnp.arange(half, dtype=F32) / half)
    ang = pos[:, None] * freqs[None, :]
    cos, sin = jnp.cos(ang), jnp.sin(ang)
    gamma = 1.0 - 2.0 ** (-5.0 - jnp.arange(RET_HEADS, dtype=F32))
    log_g = jnp.log(gamma)
    idx = jnp.arange(c, dtype=F32)
    rel = idx[:, None] - idx[None, :]
    intra = jnp.where(rel >= 0, jnp.exp(log_g[:, None, None] * jnp.maximum(rel, 0.0)), 0.0)
    q_decay = jnp.exp(log_g[:, None] * (idx + 1.0))[..., None]
    k_decay = jnp.exp(log_g[:, None] * (c - 1.0 - idx))[..., None]
    chunk_decay = jnp.exp(log_g * c)[:, None, None]
    nhs = RET_HEADS_PER_STEP
    ng = RET_HEADS // nhs
    blk = lambda off: pl.BlockSpec((tb, nhs * hd), lambda h, s: (s, off + h))
    per_head = lambda shape: pl.BlockSpec((nhs,) + shape, lambda h, s: (h, 0, 0))
    return pl.pallas_call(
        functools.partial(_ret_kernel, chunk=c, scale=hd ** -0.5),
        grid=(ng, t // tb),
        in_specs=[
            blk(0), blk(ng), blk(2 * ng), blk(3 * ng),
            pl.BlockSpec((tb, half), lambda h, s: (s, 0)),
            pl.BlockSpec((tb, half), lambda h, s: (s, 0)),
            per_head((c, c)), per_head((c, 1)), per_head((c, 1)), per_head((1, 1)),
            pl.BlockSpec((1, nhs * hd), lambda h, s: (0, h)),
            pl.BlockSpec((1, nhs * hd), lambda h, s: (0, h)),
        ],
        out_specs=pl.BlockSpec((tb, nhs * hd), lambda h, s: (s, h)),
        out_shape=jax.ShapeDtypeStruct((t, w), BF16),
        scratch_shapes=[pltpu.VMEM((nhs, hd, hd), F32)],
        compiler_params=_cparams("parallel", "arbitrary"),
        name="retention",
    )(z4, z4, z4, z4, cos, sin, intra, q_decay, k_decay, chunk_decay,
      gn_g.reshape(1, w).astype(F32), gn_b.reshape(1, w).astype(F32))


def _qknorm_kernel(x_ref, gain_ref, ones_ref, o_ref, *, inv_dim, transpose_out):
    x = x_ref[...].astype(F32)
    xx = x * x
    hi = xx.astype(BF16)
    lo = (xx - hi.astype(F32)).astype(BF16)
    ss = _dot(hi, ones_ref[...]) + _dot(lo, ones_ref[...])
    y = x * lax.rsqrt(ss * inv_dim + EPS) * gain_ref[...]
    o_ref[...] = (y.T if transpose_out else y).astype(o_ref.dtype)


def _qknorm(z, gain, col_off, transpose_out, bt=1024, bw=256):
    t = z.shape[0]
    w = gain.shape[0]
    bt = min(bt, t)
    off = col_off // bw
    grp = jnp.arange(bw) // DIFF_HEAD_DIM
    ones = (grp[:, None] == grp[None, :]).astype(BF16)
    if transpose_out:
        out_spec = pl.BlockSpec((bw, bt), lambda i, j: (j, i))
        out_shape = jax.ShapeDtypeStruct((w, t), BF16)
    else:
        out_spec = pl.BlockSpec((bt, bw), lambda i, j: (i, j))
        out_shape = jax.ShapeDtypeStruct((t, w), BF16)
    return pl.pallas_call(
        functools.partial(_qknorm_kernel, inv_dim=1.0 / DIFF_HEAD_DIM, transpose_out=transpose_out),
        grid=(t // bt, w // bw),
        in_specs=[
            pl.BlockSpec((bt, bw), lambda i, j: (i, off + j)),
            pl.BlockSpec((1, bw), lambda i, j: (0, j)),
            pl.BlockSpec((bw, bw), lambda i, j: (0, 0)),
        ],
        out_specs=out_spec,
        out_shape=out_shape,
        compiler_params=_cparams("parallel", "parallel"),
        name="q_norm_t" if transpose_out else "k_norm",
    )(z, gain.reshape(1, w).astype(F32), ones)


def _attn_kernel(qt_ref, k_ref, vt_ref, d0_ref, d1_ref, lam_ref, sg_ref, o_ref, m_ref, acc_ref, va_ref,
                 *, blk, out_scale):
    i = pl.program_id(1)
    hd = DIFF_HEAD_DIM
    dv = 2 * hd
    nhs = va_ref.shape[0]

    @pl.when(i == 0)
    def _():
        for hh in range(nhs):
            for jb in range(va_ref.shape[1]):
                va_ref[hh, jb, 0:dv, :] = vt_ref[hh * dv:(hh + 1) * dv, jb * blk:(jb + 1) * blk]
                va_ref[hh, jb, dv:, :] = jnp.ones((va_ref.shape[2] - dv, blk), va_ref.dtype)

    m_ref[...] = jnp.full(m_ref.shape, -1e30, F32)
    acc_ref[...] = jnp.zeros_like(acc_ref)

    def step(j, bias_refs):
        nsub = len(bias_refs)
        rows = pl.ds(pl.multiple_of(j * blk, blk), nsub * blk)
        for hh in range(nhs):
            for mi in range(2):
                c = 2 * hh + mi
                s = _dot(k_ref[rows, c * hd:(c + 1) * hd], qt_ref[c * hd:(c + 1) * hd, :])
                if bias_refs[0] is not None:
                    s = s + jnp.concatenate([b[hh] for b in bias_refs], axis=0)
                m_old = m_ref[c]
                m_new = jnp.maximum(m_old, jnp.max(s, axis=0, keepdims=True))
                p = jnp.exp2(s - m_new).astype(BF16)
                alpha = jnp.exp2(m_old - m_new)
                pv = _dot(va_ref[hh, j], p[0:blk])
                for u in range(1, nsub):
                    pv = pv + _dot(va_ref[hh, j + u], p[u * blk:(u + 1) * blk])
                acc_ref[c] = alpha * acc_ref[c] + pv
                m_ref[c] = m_new

    n_far = jnp.maximum(i - 1, 0)
    done = 0
    for grp in ATTN_FAR_GROUPS:
        n_grp = (n_far - done) // grp

        def far_group(gi, carry, grp=grp, done=done):
            step(done + grp * gi, (None,) * grp)
            return carry

        lax.fori_loop(0, n_grp, far_group, 0)
        done = done + n_grp * grp

    @pl.when(i >= 1)
    def _():
        step(i - 1, (d1_ref, d0_ref))

    @pl.when(i == 0)
    def _():
        step(0, (d0_ref,))
    for hh in range(nhs):
        a0, a1 = acc_ref[2 * hh], acc_ref[2 * hh + 1]
        o_t = a0[:dv] / a0[dv:dv + 1] - lam_ref[...] * (a1[:dv] / a1[dv:dv + 1])
        o = o_t.T
        ms = jnp.mean(o * o, axis=-1, keepdims=True)
        o_ref[:, hh * dv:(hh + 1) * dv] = (o * lax.rsqrt(ms + EPS) * sg_ref[...] * out_scale).astype(o_ref.dtype)


def _t5_bucket(n):
    max_exact = REL_BUCKETS // 2
    nf = jnp.maximum(n, 1).astype(F32)
    large = max_exact + (jnp.log(nf / max_exact) / math.log(REL_MAX_DIST / max_exact)
                         * (REL_BUCKETS - max_exact)).astype(jnp.int32)
    large = jnp.minimum(large, REL_BUCKETS - 1)
    return jnp.where(n < max_exact, n, large)


def _toeplitz_kernel(w_ref, d0_ref, d1_ref, *, blk):
    for which, o_ref in enumerate((d0_ref, d1_ref)):
        rows = jnp.broadcast_to(w_ref[0, which:which + 1, :], (blk, 2 * blk))
        o_ref[0] = pltpu.roll(rows, blk + 1, 1, stride=1, stride_axis=0)[:, :blk]


def _toeplitz_tiles(w, blk):
    nh = w.shape[0]
    tile = pl.BlockSpec((1, blk, blk), lambda h: (h, 0, 0))
    return pl.pallas_call(
        functools.partial(_toeplitz_kernel, blk=blk),
        grid=(nh,),
        in_specs=[pl.BlockSpec((1, 2, 2 * blk), lambda h: (h, 0, 0))],
        out_specs=[tile, tile],
        out_shape=[jax.ShapeDtypeStruct((nh, blk, blk), F32)] * 2,
        compiler_params=_cparams("parallel"),
        name="bias_tiles",
    )(w)


def _diff_attention(q_t, kn, v_t, rel_bias, lam, sub_g, lambda_init):
    t = kn.shape[0]
    nh = DIFF_HEADS
    dv = 2 * DIFF_HEAD_DIM
    ones_rows = 16
    blk = min(ATTN_BLOCK, t)
    nhs = ATTN_HEADS_PER_STEP
    assert blk >= REL_MAX_DIST
    far_bias = rel_bias[REL_BUCKETS - 1]

    r = jnp.arange(2 * blk)
    shifted = lambda rel: ((rel_bias[_t5_bucket(jnp.maximum(rel, 0))] - far_bias) * LOG2E).T
    rel_d = r - (blk - 1)
    w0 = jnp.where((rel_d >= 0)[None, :], shifted(rel_d), -jnp.inf)
    w1 = shifted(r + 1)
    d0, d1 = _toeplitz_tiles(jnp.stack([w0, w1], 1).astype(F32), blk)
    once = pl.Buffered(1)
    return pl.pallas_call(
        functools.partial(_attn_kernel, blk=blk, out_scale=1.0 - lambda_init),
        grid=(nh // nhs, t // blk),
        in_specs=[
            pl.BlockSpec((nhs * dv, blk), lambda h, i: (h, i)),
            pl.BlockSpec((t, nhs * dv), lambda h, i: (0, h), pipeline_mode=once),
            pl.BlockSpec((nhs * dv, t), lambda h, i: (h, 0), pipeline_mode=once),
            pl.BlockSpec((nhs, blk, blk), lambda h, i: (h, 0, 0), pipeline_mode=once),
            pl.BlockSpec((nhs, blk, blk), lambda h, i: (h, 0, 0), pipeline_mode=once),
            pl.BlockSpec((1, 1), lambda h, i: (0, 0)),
            pl.BlockSpec((1, dv), lambda h, i: (0, 0)),
        ],
        out_specs=pl.BlockSpec((blk, nhs * dv), lambda h, i: (i, h)),
        out_shape=jax.ShapeDtypeStruct((t, nh * dv), BF16),
        scratch_shapes=[pltpu.VMEM((2 * nhs, 1, blk), F32), pltpu.VMEM((2 * nhs, dv + ones_rows, blk), F32),
                        pltpu.VMEM((nhs, t // blk, dv + ones_rows, blk), BF16)],
        compiler_params=_cparams("parallel", "arbitrary"),
        name="diff_attn",
    )(q_t, kn, v_t, d0, d1, jnp.reshape(lam, (1, 1)).astype(F32), sub_g.reshape(1, dv).astype(F32))


def _fold_keys_kernel(sk_ref, w_ref, o_ref):
    w = w_ref[0]
    for b in range(sk_ref.shape[0]):
        cols = slice(b * LANES, (b + 1) * LANES)
        o_ref[cols, :] = _dot_nt(sk_ref[b], w[:, cols]).astype(o_ref.dtype)


def _fold_keys(w_qs, layer, sub_keys, bd=1024, nb=4):
    _, d, n = w_qs.shape
    return pl.pallas_call(
        _fold_keys_kernel,
        grid=(d // bd, n // (nb * LANES)),
        in_specs=[
            pl.BlockSpec((nb, LANES, LANES), lambda i, j: (j, 0, 0)),
            pl.BlockSpec((1, bd, nb * LANES), lambda i, j: (layer, i, j)),
        ],
        out_specs=pl.BlockSpec((nb * LANES, bd), lambda i, j: (j, i)),
        out_shape=jax.ShapeDtypeStruct((n, d), BF16),
        compiler_params=_cparams("parallel", "parallel"),
        name="peer_fold_keys",
    )(sub_keys, w_qs)


def _top_values(s, n):
    vals = []
    for r in range(n):
        m = jnp.max(s, axis=0, keepdims=True)
        vals.append(m)
        if r < n - 1:
            s = jnp.where(s == m, -jnp.inf, s)
    return vals


def _route_kernel(s_ref, e1_ref, e2_ref, c_ref):
    nk = PEER_N_KEYS
    k = PEER_TOPK
    tl = s_ref.shape[1]
    pad = 24
    row = lax.broadcasted_iota(jnp.int32, (pad, tl), 0)
    hrow = lax.broadcasted_iota(jnp.int32, (PEER_HEADS, tl), 0)
    c_all = jnp.zeros((PEER_HEADS, tl), F32)
    for h in range(PEER_HEADS):
        s1 = s_ref[2 * h * nk:(2 * h + 1) * nk, :]
        s2 = s_ref[(2 * h + 1) * nk:(2 * h + 2) * nk, :]
        a = _top_values(s1, k + 1)
        b = _top_values(s2, k + 1)
        ea = [jnp.exp(x - a[0]) for x in a]
        eb = [jnp.exp(x - b[0]) for x in b]
        ea_m = jnp.full((pad, tl), -1.0, F32)
        eb_m = jnp.full((pad, tl), -1.0, F32)
        for r in range(k + 1):
            ea_m = jnp.where(row == r, ea[r], ea_m)
            eb_m = jnp.where(row == r, eb[r], eb_m)
        cand = jnp.concatenate(
            [ea[0] * eb_m, ea[1] * eb_m[:8], ea[2] * eb_m[:8], ea[3] * eb_m[:8],
             eb[0] * ea_m, eb[1] * ea_m[:8], eb[2] * ea_m[:8]], axis=0)
        v = _top_values(cand, k + 1)
        z = v[0]
        for r in range(1, k):
            z = z + v[r]
        rz = 1.0 / z
        e1_ref[h] = jnp.exp(s1 - a[0])
        e2_ref[h] = (jnp.exp(s2 - b[0]) * rz).astype(e2_ref.dtype)
        c_all = jnp.where(hrow == h, 0.5 * (v[k - 1] + v[k]) * rz, c_all)
    c_ref[...] = c_all


def _peer_route(s_t, tl=256):
    n, t = s_t.shape
    tl = min(tl, t)
    nh, nk = PEER_HEADS, PEER_N_KEYS
    return pl.pallas_call(
        _route_kernel,
        grid=(t // tl,),
        in_specs=[pl.BlockSpec((n, tl), lambda i: (0, i))],
        out_specs=[
            pl.BlockSpec((nh, nk, tl), lambda i: (0, 0, i)),
            pl.BlockSpec((nh, nk, tl), lambda i: (0, 0, i)),
            pl.BlockSpec((nh, tl), lambda i: (0, i)),
        ],
        out_shape=[
            jax.ShapeDtypeStruct((nh, nk, t), F32),
            jax.ShapeDtypeStruct((nh, nk, t), BF16),
            jax.ShapeDtypeStruct((nh, t), F32),
        ],
        compiler_params=_cparams("parallel"),
        name="peer_route",
    )(s_t)


def _peer_dense_kernel(u_ref, vt_ref, h_ref, e1_ref, e2_ref, c_ref, o_ref, w_ref):
    e = pl.program_id(1)

    @pl.when(e == 0)
    def _():
        o_ref[...] = jnp.zeros_like(o_ref)

    a_t = _dot_nt(u_ref[...], h_ref[...])
    tm = a_t.shape[1]
    nslab = a_t.shape[0] // LANES
    gdt = e2_ref.dtype
    for ii in range(nslab):
        rows = slice(ii * LANES, (ii + 1) * LANES)
        g = jnp.zeros((LANES, tm), gdt)
        for h in range(PEER_HEADS):
            vv = e2_ref[h] * e1_ref[h, pl.ds(e * nslab + ii, 1), :].astype(gdt)
            g = g + jnp.where(vv >= c_ref[h:h + 1, :].astype(gdt), vv, jnp.zeros_like(vv))
        w_ref[rows, :] = (g * _gelu(a_t[rows, :].astype(gdt))).astype(w_ref.dtype)
    o_ref[...] += _dot(vt_ref[...], w_ref[...])


def _peer_dense(hq, u_tab, v_tab_t, e1_t, e2, c, tm=512, te=512):
    t, d = hq.shape
    ne = u_tab.shape[0]
    tm = min(tm, t)
    nh, nk = PEER_HEADS, PEER_N_KEYS
    once = pl.Buffered(1)
    return pl.pallas_call(
        _peer_dense_kernel,
        grid=(t // tm, ne // te),
        in_specs=[
            pl.BlockSpec((te, d), lambda i, e: (e, 0)),
            pl.BlockSpec((d, te), lambda i, e: (0, e)),
            pl.BlockSpec((tm, d), lambda i, e: (i, 0), pipeline_mode=once),
            pl.BlockSpec((nh, nk, tm), lambda i, e: (0, 0, i), pipeline_mode=once),
            pl.BlockSpec((nh, nk, tm), lambda i, e: (0, 0, i), pipeline_mode=once),
            pl.BlockSpec((nh, tm), lambda i, e: (0, i)),
        ],
        out_specs=pl.BlockSpec((d, tm), lambda i, e: (0, i)),
        out_shape=jax.ShapeDtypeStruct((d, t), F32),
        scratch_shapes=[pltpu.VMEM((te, tm), BF16)],
        compiler_params=_cparams("parallel", "arbitrary"),
        name="peer_dense",
    )(u_tab, v_tab_t, hq, e1_t, e2, c)


def _peer(x, norm_g, w_qs, sub_keys, u_tabs, v_tabs, layer):
    hq = _rmsnorm(x, norm_g)
    nh, nk = PEER_HEADS, PEER_N_KEYS
    s_t = _matmul_nt(_fold_keys(w_qs, layer, sub_keys.reshape(2 * nh, nk, -1)), hq, out_dtype=F32, name="peer_scores")
    e1, e2, c = _peer_route(s_t)
    return _peer_dense(hq, _cast_bf16(u_tabs, layer), _cast_bf16(v_tabs, layer, transpose=True), e1, e2, c)


def _even_layer(x, h, i, w_ins, log_step, a_re, a_im, b_re, b_im, c_re, c_im, d_skip,
                glu_ws, glu_b, gn_g, gn_b, w_outs):
    s5w = d_skip.shape[0]
    u_t = _matmul([h], [_cast_bf16(w_ins, i, 0, s5w)], tile_major=True, name="ev_in_s5")
    z4 = _matmul([h], [_cast_bf16(w_ins, i, s5w)], name="ev_in_ret")
    tables = _s5_tables(log_step, a_re, a_im, b_re, b_im, c_re, c_im, d_skip)
    ya = _glu(_s5_core(u_t, tables), _cast_bf16(glu_ws, i), glu_b)
    yb = _retention(z4, gn_g, gn_b)
    w_out = _cast_bf16(w_outs, i)
    return _matmul([ya, yb], [w_out[:s5w], w_out[s5w:]], resid=x, out_dtype=F32, name="ev_out")


def _odd_layer(x, h, i, w_ins, q_norm_g, k_norm_g, lq1, lk1, lq2, lk2, sub_norm_g, w_outs,
               rel_bias, lambda_init):
    d = x.shape[1]
    z = _matmul([h], [_cast_bf16(w_ins, i, 0, 2 * d)], name="od_in_qk")
    v_t = _matmul_nt(_cast_bf16(w_ins, i, 2 * d, transpose=True), h, name="od_in_vt")
    reps = d // DIFF_HEAD_DIM
    q_t = _qknorm(z, jnp.tile(q_norm_g, reps) * (DIFF_HEAD_DIM ** -0.5 * LOG2E), 0, True)
    kn = _qknorm(z, jnp.tile(k_norm_g, reps), d, False)
    lam = jnp.exp(jnp.sum(lq1 * lk1)) - jnp.exp(jnp.sum(lq2 * lk2)) + lambda_init
    o = _diff_attention(q_t, kn, v_t, rel_bias, lam, sub_norm_g, lambda_init)
    return _matmul([o], [_cast_bf16(w_outs, i)], resid=x, out_dtype=F32, name="od_out")


def kernel(x, ev_norm_g, ev_w_in, s5_log_step, s5_a_re, s5_a_im, s5_b_re, s5_b_im, s5_c_re, s5_c_im,
           s5_d, s5_glu_w, s5_glu_b, ret_gn_g, ret_gn_b, ev_w_out, od_norm_g, od_w_in, diff_q_norm_g,
           diff_k_norm_g, diff_lq1, diff_lk1, diff_lq2, diff_lk2, diff_sub_norm_g, od_w_out, rel_bias,
           ffn_norm_g, peer_w_q, peer_sub_keys, peer_u, peer_v):
    bsz, t, d = x.shape
    depth = ffn_norm_g.shape[0]
    mixer_norm_g = lambda layer: (ev_norm_g if layer % 2 == 0 else od_norm_g)[layer // 2]
    outs = []
    for b in range(bsz):
        xb = x[b]
        h = _rmsnorm(xb, mixer_norm_g(0))
        for layer in range(depth):
            i = layer // 2
            if layer % 2 == 0:
                xb = _even_layer(xb, h, i, ev_w_in, s5_log_step[i], s5_a_re[i], s5_a_im[i],
                                 s5_b_re[i], s5_b_im[i], s5_c_re[i], s5_c_im[i], s5_d[i], s5_glu_w,
                                 s5_glu_b[i], ret_gn_g[i], ret_gn_b[i], ev_w_out)
            else:
                lambda_init = 0.8 - 0.6 * math.exp(-0.3 * layer)
                xb = _odd_layer(xb, h, i, od_w_in, diff_q_norm_g[i], diff_k_norm_g[i],
                                diff_lq1[i], diff_lk1[i], diff_lq2[i], diff_lk2[i], diff_sub_norm_g[i],
                                od_w_out, rel_bias, lambda_init)
            p_t = _peer(xb, ffn_norm_g[layer], peer_w_q, peer_sub_keys[layer], peer_u, peer_v, layer)
            xb, h = _add_norm(xb, p_t, mixer_norm_g(layer + 1) if layer + 1 < depth else None)
        outs.append(xb)
    return jnp.stack(outs, 0)
```

```python
import functools
import math

import jax
import jax.numpy as jnp
from jax import lax
from jax.experimental import pallas as pl
from jax.experimental.pallas import tpu as pltpu

F32 = jnp.float32
BF16 = jnp.bfloat16

EPS = 1e-6
LOG2E = 1.4426950408889634
LANES = 128
VMEM_LIMIT_BYTES = 56 * 1024 * 1024

S5_GROUP = 16
S5_STATE = 64
S5_CHUNK = 16
RET_HEADS = 8
RET_CHUNK = 128
RET_HEADS_PER_STEP = 2
ROPE_BASE = 10000.0
DIFF_HEADS = 16
DIFF_HEAD_DIM = 128
ATTN_BLOCK = 512
ATTN_HEADS_PER_STEP = 2
ATTN_FAR_GROUPS = (4, 2, 1)
REL_BUCKETS = 32
REL_MAX_DIST = 128
PEER_HEADS = 8
PEER_N_KEYS = 128
PEER_TOPK = 16


def _cparams(*sem, flags=None):
    return pltpu.CompilerParams(dimension_semantics=sem, vmem_limit_bytes=VMEM_LIMIT_BYTES, flags=flags)


def _dot(a, b):
    return jnp.dot(a, b, preferred_element_type=F32)


def _dot_nt(a, b):
    return lax.dot_general(a, b, (((1,), (1,)), ((), ())), preferred_element_type=F32)


def _dot_tn(a, b):
    return lax.dot_general(a, b, (((0,), (0,)), ((), ())), preferred_element_type=F32)


def _gelu(x):
    return 0.5 * x * (1.0 + jnp.tanh(0.7978845608028654 * (x + 0.044715 * (x * x * x))))


def _norm_kernel(x_ref, g_ref, o_ref):
    x = x_ref[...]
    r = lax.rsqrt(jnp.mean(x * x, axis=-1, keepdims=True) + EPS)
    o_ref[...] = (x * r * g_ref[...]).astype(o_ref.dtype)


def _rmsnorm(x, g, bt=256):
    t, d = x.shape
    return pl.pallas_call(
        _norm_kernel,
        grid=(t // bt,),
        in_specs=[pl.BlockSpec((bt, d), lambda i: (i, 0)), pl.BlockSpec((1, d), lambda i: (0, 0))],
        out_specs=pl.BlockSpec((bt, d), lambda i: (i, 0)),
        out_shape=jax.ShapeDtypeStruct((t, d), BF16),
        compiler_params=_cparams("parallel"),
        name="rmsnorm",
    )(x, g.reshape(1, d).astype(F32))


def _addnorm_kernel(x_ref, pt_ref, g_ref, xo_ref, h_ref=None):
    x = x_ref[...] + pt_ref[...].T
    xo_ref[...] = x
    if h_ref is not None:
        r = lax.rsqrt(jnp.mean(x * x, axis=-1, keepdims=True) + EPS)
        h_ref[...] = (x * r * g_ref[...]).astype(h_ref.dtype)


def _add_norm(x, p_t, g=None, bt=256):
    t, d = x.shape
    with_norm = g is not None
    row = pl.BlockSpec((bt, d), lambda i: (i, 0))
    g = jnp.ones((d,), F32) if g is None else g
    out_specs = [row, row] if with_norm else [row]
    out_shape = [jax.ShapeDtypeStruct((t, d), F32)] + ([jax.ShapeDtypeStruct((t, d), BF16)] if with_norm else [])
    out = pl.pallas_call(
        _addnorm_kernel,
        grid=(t // bt,),
        in_specs=[row, pl.BlockSpec((d, bt), lambda i: (0, i)), pl.BlockSpec((1, d), lambda i: (0, 0))],
        out_specs=out_specs,
        out_shape=out_shape,
        compiler_params=_cparams("parallel"),
        name="add_norm" if with_norm else "add_update",
    )(x, p_t, g.reshape(1, d).astype(F32))
    return (out[0], out[1]) if with_norm else (out[0], None)


def _cast_kernel(w_ref, o_ref, *, transpose):
    w = w_ref[0]
    o_ref[...] = (w.T if transpose else w).astype(o_ref.dtype)


def _cast_bf16(w, layer, col0=0, ncols=None, transpose=False, br=512, bc=4096):
    _, r, c = w.shape
    ncols = c - col0 if ncols is None else ncols
    bc = math.gcd(math.gcd(bc, ncols), col0) if col0 else math.gcd(bc, ncols)
    br = min(br, r)
    off = col0 // bc
    if transpose:
        out_spec = pl.BlockSpec((bc, br), lambda i, j: (j, i))
        out_shape = jax.ShapeDtypeStruct((ncols, r), BF16)
    else:
        out_spec = pl.BlockSpec((br, bc), lambda i, j: (i, j))
        out_shape = jax.ShapeDtypeStruct((r, ncols), BF16)
    return pl.pallas_call(
        functools.partial(_cast_kernel, transpose=transpose),
        grid=(r // br, ncols // bc),
        in_specs=[pl.BlockSpec((1, br, bc), lambda i, j: (layer, i, off + j))],
        out_specs=out_spec,
        out_shape=out_shape,
        compiler_params=_cparams("parallel", "parallel"),
        name="cast_bf16_t" if transpose else "cast_bf16",
    )(w)


def _mm_kernel(*refs, n_lhs, has_resid, tile_major):
    a_refs, b_refs = refs[:n_lhs], refs[n_lhs:2 * n_lhs]
    r_ref = refs[2 * n_lhs] if has_resid else None
    o_ref = refs[-1]
    acc = _dot(a_refs[0][...], b_refs[0][...])
    for a_ref, b_ref in zip(a_refs[1:], b_refs[1:]):
        acc = acc + _dot(a_ref[...], b_ref[...])
    if has_resid:
        acc = acc + r_ref[...]
    if tile_major:
        for jj in range(o_ref.shape[0]):
            o_ref[jj] = acc[:, jj * LANES:(jj + 1) * LANES].astype(o_ref.dtype)
    else:
        o_ref[...] = acc.astype(o_ref.dtype)


def _matmul(lhs, rhs, *, resid=None, out_dtype=BF16, bm=512, bn=1024, tile_major=False, name="mm"):
    m, n = lhs[0].shape[0], rhs[0].shape[1]
    bm, bn = min(bm, m), min(bn, n)
    in_specs = [pl.BlockSpec((bm, a.shape[1]), lambda i, j: (i, 0)) for a in lhs]
    in_specs += [pl.BlockSpec((b.shape[0], bn), lambda i, j: (0, j)) for b in rhs]
    args = list(lhs) + list(rhs)
    if resid is not None:
        in_specs.append(pl.BlockSpec((bm, bn), lambda i, j: (i, j)))
        args.append(resid)
    if tile_major:
        out_spec = pl.BlockSpec((bn // LANES, bm, LANES), lambda i, j: (j, i, 0))
        out_shape = jax.ShapeDtypeStruct((n // LANES, m, LANES), out_dtype)
    else:
        out_spec = pl.BlockSpec((bm, bn), lambda i, j: (i, j))
        out_shape = jax.ShapeDtypeStruct((m, n), out_dtype)
    kern = functools.partial(_mm_kernel, n_lhs=len(lhs), has_resid=resid is not None, tile_major=tile_major)
    return pl.pallas_call(
        kern, grid=(m // bm, n // bn), in_specs=in_specs, out_specs=out_spec, out_shape=out_shape,
        compiler_params=_cparams("parallel", "parallel"), name=name,
    )(*args)


def _mm_nt_kernel(w_ref, h_ref, o_ref):
    o_ref[...] = _dot_nt(w_ref[...], h_ref[...]).astype(o_ref.dtype)


def _matmul_nt(w_t, h, *, bm=512, bn=1024, out_dtype=BF16, name="mm_nt"):
    n, k = w_t.shape
    m = h.shape[0]
    bm, bn = min(bm, m), min(bn, n)
    return pl.pallas_call(
        _mm_nt_kernel,
        grid=(m // bm, n // bn),
        in_specs=[pl.BlockSpec((bn, k), lambda i, j: (j, 0)), pl.BlockSpec((bm, k), lambda i, j: (i, 0))],
        out_specs=pl.BlockSpec((bn, bm), lambda i, j: (j, i)),
        out_shape=jax.ShapeDtypeStruct((n, m), out_dtype),
        compiler_params=_cparams("parallel", "parallel"),
        name=name,
    )(w_t, h)


def _s5_tables(log_step, a_re, a_im, b_re, b_im, c_re, c_im, d_skip):
    L = S5_CHUNK
    g, p = a_re.shape
    gt = LANES // S5_GROUP
    nt = g // gt
    hp = lax.Precision.HIGHEST
    step = jnp.exp(log_step)[:, None]
    lr, li = a_re, a_im
    mag = jnp.exp(lr * step)
    abar_re = mag * jnp.cos(li * step)
    abar_im = mag * jnp.sin(li * step)
    den = lr * lr + li * li
    num_re = abar_re - 1.0
    f_re = (num_re * lr + abar_im * li) / den
    f_im = (abar_im * lr - num_re * li) / den
    bb_re = f_re[..., None] * b_re - f_im[..., None] * b_im
    bb_im = f_re[..., None] * b_im + f_im[..., None] * b_re
    k = jnp.arange(L + 1, dtype=F32)[:, None, None]
    pmag = jnp.exp(k * (lr * step)[None])
    pang = k * (li * step)[None]
    pw_re = pmag * jnp.cos(pang)
    pw_im = pmag * jnp.sin(pang)
    ns = 2 * gt * p

    def same_group(row_g, col_g):
        return (row_g[:, None] == col_g[None, :]).astype(F32)

    pwt_re, pwt_im = pw_re[:L].transpose(1, 2, 0), pw_im[:L].transpose(1, 2, 0)
    ct_re, ct_im = c_re.transpose(0, 2, 1), c_im.transpose(0, 2, 1)
    cpt_re = ct_re[:, :, None, :] * pwt_re[..., None] - ct_im[:, :, None, :] * pwt_im[..., None]
    cpt_im = ct_re[:, :, None, :] * pwt_im[..., None] + ct_im[:, :, None, :] * pwt_re[..., None]
    a_t = jnp.concatenate([cpt_re, -cpt_im], axis=1).reshape(g, 2 * p, L * S5_GROUP)
    b_t = jnp.concatenate([bb_re, bb_im], axis=1).transpose(0, 2, 1)
    kkt = jnp.einsum("gip,gpx->gix", b_t, a_t, precision=hp)
    kki = kkt.reshape(nt, gt, S5_GROUP, L, S5_GROUP).transpose(0, 3, 1, 2, 4)
    kki = kki.reshape(nt, L, LANES, S5_GROUP)
    lane_g = jnp.arange(LANES) // S5_GROUP
    bd = (jnp.tile(kki, (1, 1, 1, gt)) * same_group(lane_g, lane_g)).astype(BF16)

    k_rev = (L - 1) - k[:L]
    pmag_rev = jnp.exp(k_rev * (lr * step)[None])
    pang_rev = k_rev * (li * step)[None]
    pwr, pwi = pmag_rev * jnp.cos(pang_rev), pmag_rev * jnp.sin(pang_rev)
    pb_re = pwr[..., None] * bb_re[None] - pwi[..., None] * bb_im[None]
    pb_im = pwr[..., None] * bb_im[None] + pwi[..., None] * bb_re[None]
    pb = jnp.stack([pb_re, pb_im], 0).reshape(2, L, nt, gt, p, S5_GROUP)
    ps = pb.transpose(2, 1, 3, 5, 0, 4).reshape(nt, L * LANES, 2 * p).astype(BF16)
    rr, cc = jnp.arange(2 * p), jnp.arange(ns)
    ex_p = ((rr[:, None] // p == cc[None, :] // (gt * p)) & (rr[:, None] % p == cc[None, :] % p))
    row_g = (jnp.arange(L * LANES) % LANES) // S5_GROUP
    col_g = (cc % (gt * p)) // p
    p_parts = (ps, ex_p.astype(BF16), same_group(row_g, col_g).astype(BF16))

    qr, qi = pw_re[1:L + 1], pw_im[1:L + 1]
    cn_re = c_re[None] * qr[:, :, None, :] - c_im[None] * qi[:, :, None, :]
    cn_im = c_re[None] * qi[:, :, None, :] + c_im[None] * qr[:, :, None, :]
    nb = jnp.stack([cn_re, -cn_im], 0).reshape(2, L, nt, gt, S5_GROUP, p)
    nsm = nb.transpose(2, 0, 3, 5, 1, 4).reshape(nt, ns, L * S5_GROUP).astype(BF16)
    rr, cc = jnp.arange(L * S5_GROUP), jnp.arange(L * LANES)
    ex_n = ((rr[:, None] // S5_GROUP == cc[None, :] // LANES)
            & (rr[:, None] % S5_GROUP == cc[None, :] % S5_GROUP))
    row_g = (jnp.arange(ns) % (gt * p)) // p
    col_g = (cc % LANES) // S5_GROUP
    n_parts = (nsm, ex_n.astype(BF16), same_group(row_g, col_g).astype(BF16))

    a_l = jnp.stack([pw_re[L], pw_im[L]], 0).reshape(2, nt, gt * p).transpose(1, 0, 2)
    d_t = jnp.tile(d_skip.reshape(nt, 1, LANES), (1, 1, L))
    return bd, p_parts, n_parts, a_l, d_t


def _s5_kernel(x_ref, bd_ref, ps_ref, exp_ref, mkp_ref, ns_ref, exn_ref, mkn_ref, al_ref, d_ref, o_ref,
               m_ref, sloc_ref, sprev_ref):
    x = x_ref[0]
    nc, ns = sloc_ref.shape
    half = ns // 2
    L = bd_ref.shape[1]
    p_op = _dot(ps_ref[0], exp_ref[...]).astype(BF16) * mkp_ref[...]
    n_op = _dot(ns_ref[0], exn_ref[...]).astype(BF16) * mkn_ref[...]
    for tau in range(L):
        for t in range(L):
            blk = bd_ref[0, t - tau] if t >= tau else jnp.zeros((LANES, LANES), m_ref.dtype)
            m_ref[tau * LANES:(tau + 1) * LANES, t * LANES:(t + 1) * LANES] = blk
    sloc_ref[...] = _dot(x, p_op)
    a_re = al_ref[0, 0:1, :]
    a_im = al_ref[0, 1:2, :]

    def body(c, carry):
        s_re, s_im = carry
        row = pl.ds(c, 1)
        sprev_ref[row, 0:half] = s_re
        sprev_ref[row, half:ns] = s_im
        l_re = sloc_ref[row, 0:half]
        l_im = sloc_ref[row, half:ns]
        return (a_re * s_re - a_im * s_im + l_re, a_re * s_im + a_im * s_re + l_im)

    zero = jnp.zeros((1, half), F32)
    lax.fori_loop(0, nc, body, (zero, zero))
    y = _dot(x, m_ref[...]) + _dot(sprev_ref[...].astype(BF16), n_op)
    y = y + d_ref[0] * x.astype(F32)
    o_ref[0] = _gelu(y).astype(o_ref.dtype)


def _s5_core(u_t, tables):
    bd, (ps, ex_p, mk_p), (nsm, ex_n, mk_n), a_l, d_t = tables
    nt, t, _ = u_t.shape
    L = S5_CHUNK
    nc, w, ns = t // L, L * LANES, mk_p.shape[1]
    x = u_t.reshape(nt, nc, w)
    per_tile = lambda a: pl.BlockSpec((1,) + a.shape[1:], lambda j: (j,) + (0,) * (a.ndim - 1))
    shared = lambda a: pl.BlockSpec(a.shape, lambda j: (0,) * a.ndim, pipeline_mode=pl.Buffered(1))
    out = pl.pallas_call(
        _s5_kernel,
        grid=(nt,),
        in_specs=[
            pl.BlockSpec((1, nc, w), lambda j: (j, 0, 0)),
            per_tile(bd),
            per_tile(ps), shared(ex_p), shared(mk_p),
            per_tile(nsm), shared(ex_n), shared(mk_n),
            per_tile(a_l),
            per_tile(d_t),
        ],
        out_specs=pl.BlockSpec((1, nc, w), lambda j: (j, 0, 0)),
        out_shape=jax.ShapeDtypeStruct((nt, nc, w), BF16),
        scratch_shapes=[pltpu.VMEM((w, w), BF16), pltpu.VMEM((nc, ns), F32), pltpu.VMEM((nc, ns), F32)],
        compiler_params=_cparams("parallel"),
        name="s5_core",
    )(x, bd, ps, ex_p, mk_p, nsm, ex_n, mk_n, a_l, d_t)
    return out.reshape(nt, t, LANES)


def _glu_kernel(y_ref, w_ref, b_ref, o_ref):
    y = jnp.concatenate([y_ref[j] for j in range(y_ref.shape[0])], axis=1)
    acc = _dot(y, w_ref[...]) + b_ref[...]
    o_ref[...] = (y.astype(F32) * jax.nn.sigmoid(acc)).astype(o_ref.dtype)


def _glu(y_t, w, b, bm=512):
    nt, t, _ = y_t.shape
    width = nt * LANES
    return pl.pallas_call(
        _glu_kernel,
        grid=(t // bm,),
        in_specs=[
            pl.BlockSpec((nt, bm, LANES), lambda i: (0, i, 0)),
            pl.BlockSpec((width, width), lambda i: (0, 0)),
            pl.BlockSpec((1, width), lambda i: (0, 0)),
        ],
        out_specs=pl.BlockSpec((bm, width), lambda i: (i, 0)),
        out_shape=jax.ShapeDtypeStruct((t, width), BF16),
        compiler_params=_cparams("parallel"),
        name="s5_glu",
    )(y_t, w, b.reshape(1, width).astype(F32))


def _ret_kernel(q_ref, k_ref, v_ref, g_ref, cos_ref, sin_ref, intra_ref, qd_ref, kd_ref, cd_ref,
                gg_ref, gb_ref, o_ref, state_ref, *, chunk, scale):
    @pl.when(pl.program_id(1) == 0)
    def _():
        state_ref[...] = jnp.zeros_like(state_ref)

    half = cos_ref.shape[1]
    hd = 2 * half
    nhs = state_ref.shape[0]

    def body(c, carry):
        rows = pl.ds(pl.multiple_of(c * chunk, chunk), chunk)
        cos, sin = cos_ref[rows, :], sin_ref[rows, :]

        def rot(x):
            x1, x2 = x[:, :half], x[:, half:]
            return jnp.concatenate([x1 * cos - x2 * sin, x1 * sin + x2 * cos], axis=1)

        for hh in range(nhs):
            cols = slice(hh * hd, (hh + 1) * hd)
            q = rot(q_ref[rows, cols].astype(F32))
            k = rot(k_ref[rows, cols].astype(F32)) * scale
            v = v_ref[rows, cols]
            st = state_ref[hh]
            scores = _dot_nt(q.astype(BF16), k.astype(BF16)) * intra_ref[hh]
            out = _dot(scores.astype(BF16), v) + _dot((q * qd_ref[hh]).astype(BF16), st.astype(BF16))
            state_ref[hh] = cd_ref[hh] * st + _dot_tn((k * kd_ref[hh]).astype(BF16), v)
            mu = jnp.mean(out, axis=-1, keepdims=True)
            cen = out - mu
            var = jnp.mean(cen * cen, axis=-1, keepdims=True)
            o = cen * lax.rsqrt(var + EPS) * gg_ref[:, cols] + gb_ref[:, cols]
            gt = g_ref[rows, cols].astype(F32)
            o_ref[rows, cols] = (gt * jax.nn.sigmoid(gt) * o).astype(o_ref.dtype)
        return carry

    lax.fori_loop(0, q_ref.shape[0] // chunk, body, 0)


def _retention(z4, gn_g, gn_b, tb=2048):
    t = z4.shape[0]
    w = z4.shape[1] // 4
    hd = w // RET_HEADS
    half = hd // 2
    c = RET_CHUNK
    tb = min(tb, t)
    pos = jnp.arange(t, dtype=F32)
    freqs = ROPE_BASE ** (-jnp.arange(half, dtype=F32) / half)
    ang = pos[:, None] * freqs[None, :]
    cos, sin = jnp.cos(ang), jnp.sin(ang)
    gamma = 1.0 - 2.0 ** (-5.0 - jnp.arange(RET_HEADS, dtype=F32))
    log_g = jnp.log(gamma)
    idx = jnp.arange(c, dtype=F32)
    rel = idx[:, None] - idx[None, :]
    intra = jnp.where(rel >= 0, jnp.exp(log_g[:, None, None] * jnp.maximum(rel, 0.0)), 0.0)
    q_decay = jnp.exp(log_g[:, None] * (idx + 1.0))[..., None]
    k_decay = jnp.exp(log_g[:, None] * (c - 1.0 - idx))[..., None]
    chunk_decay = jnp.exp(log_g * c)[:, None, None]
    nhs = RET_HEADS_PER_STEP
    ng = RET_HEADS // nhs
    blk = lambda off: pl.BlockSpec((tb, nhs * hd), lambda h, s: (s, off + h))
    per_head = lambda shape: pl.BlockSpec((nhs,) + shape, lambda h, s: (h, 0, 0))
    return pl.pallas_call(
        functools.partial(_ret_kernel, chunk=c, scale=hd ** -0.5),
        grid=(ng, t // tb),
        in_specs=[
            blk(0), blk(ng), blk(2 * ng), blk(3 * ng),
            pl.BlockSpec((tb, half), lambda h, s: (s, 0)),
            pl.BlockSpec((tb, half), lambda h, s: (s, 0)),
            per_head((c, c)), per_head((c, 1)), per_head((c, 1)), per_head((1, 1)),
            pl.BlockSpec((1, nhs * hd), lambda h, s: (0, h)),
            pl.BlockSpec((1, nhs * hd), lambda h, s: (0, h)),
        ],
        out_specs=pl.BlockSpec((tb, nhs * hd), lambda h, s: (s, h)),
        out_shape=jax.ShapeDtypeStruct((t, w), BF16),
        scratch_shapes=[pltpu.VMEM((nhs, hd, hd), F32)],
        compiler_params=_cparams("parallel", "arbitrary"),
        name="retention",
    )(z4, z4, z4, z4, cos, sin, intra, q_decay, k_decay, chunk_decay,
      gn_g.reshape(1, w).astype(F32), gn_b.reshape(1, w).astype(F32))


def _qknorm_kernel(x_ref, gain_ref, ones_ref, o_ref, *, inv_dim, transpose_out):
    x = x_ref[...].astype(F32)
    xx = x * x
    hi = xx.astype(BF16)
    lo = (xx - hi.astype(F32)).astype(BF16)
    ss = _dot(hi, ones_ref[...]) + _dot(lo, ones_ref[...])
    y = x * lax.rsqrt(ss * inv_dim + EPS) * gain_ref[...]
    o_ref[...] = (y.T if transpose_out else y).astype(o_ref.dtype)


def _qknorm(z, gain, col_off, transpose_out, bt=1024, bw=256):
    t = z.shape[0]
    w = gain.shape[0]
    bt = min(bt, t)
    off = col_off // bw
    grp = jnp.arange(bw) // DIFF_HEAD_DIM
    ones = (grp[:, None] == grp[None, :]).astype(BF16)
    if transpose_out:
        out_spec = pl.BlockSpec((bw, bt), lambda i, j: (j, i))
        out_shape = jax.ShapeDtypeStruct((w, t), BF16)
    else:
        out_spec = pl.BlockSpec((bt, bw), lambda i, j: (i, j))
        out_shape = jax.ShapeDtypeStruct((t, w), BF16)
    return pl.pallas_call(
        functools.partial(_qknorm_kernel, inv_dim=1.0 / DIFF_HEAD_DIM, transpose_out=transpose_out),
        grid=(t // bt, w // bw),
        in_specs=[
            pl.BlockSpec((bt, bw), lambda i, j: (i, off + j)),
            pl.BlockSpec((1, bw), lambda i, j: (0, j)),
            pl.BlockSpec((bw, bw), lambda i, j: (0, 0)),
        ],
        out_specs=out_spec,
        out_shape=out_shape,
        compiler_params=_cparams("parallel", "parallel"),
        name="q_norm_t" if transpose_out else "k_norm",
    )(z, gain.reshape(1, w).astype(F32), ones)


def _attn_kernel(qt_ref, k_ref, vt_ref, d0_ref, d1_ref, lam_ref, sg_ref, o_ref, m_ref, acc_ref, va_ref,
                 *, blk, out_scale):
    i = pl.program_id(1)
    hd = DIFF_HEAD_DIM
    dv = 2 * hd
    nhs = va_ref.shape[0]

    @pl.when(i == 0)
    def _():
        for hh in range(nhs):
            for jb in range(va_ref.shape[1]):
                va_ref[hh, jb, 0:dv, :] = vt_ref[hh * dv:(hh + 1) * dv, jb * blk:(jb + 1) * blk]
                va_ref[hh, jb, dv:, :] = jnp.ones((va_ref.shape[2] - dv, blk), va_ref.dtype)

    m_ref[...] = jnp.full(m_ref.shape, -1e30, F32)
    acc_ref[...] = jnp.zeros_like(acc_ref)

    def step(j, bias_refs):
        nsub = len(bias_refs)
        rows = pl.ds(pl.multiple_of(j * blk, blk), nsub * blk)
        for hh in range(nhs):
            for mi in range(2):
                c = 2 * hh + mi
                s = _dot(k_ref[rows, c * hd:(c + 1) * hd], qt_ref[c * hd:(c + 1) * hd, :])
                if bias_refs[0] is not None:
                    s = s + jnp.concatenate([b[hh] for b in bias_refs], axis=0)
                m_old = m_ref[c]
                m_new = jnp.maximum(m_old, jnp.max(s, axis=0, keepdims=True))
                p = jnp.exp2(s - m_new).astype(BF16)
                alpha = jnp.exp2(m_old - m_new)
                pv = _dot(va_ref[hh, j], p[0:blk])
                for u in range(1, nsub):
                    pv = pv + _dot(va_ref[hh, j + u], p[u * blk:(u + 1) * blk])
                acc_ref[c] = alpha * acc_ref[c] + pv
                m_ref[c] = m_new

    n_far = jnp.maximum(i - 1, 0)
    done = 0
    for grp in ATTN_FAR_GROUPS:
        n_grp = (n_far - done) // grp

        def far_group(gi, carry, grp=grp, done=done):
            step(done + grp * gi, (None,) * grp)
            return carry

        lax.fori_loop(0, n_grp, far_group, 0)
        done = done + n_grp * grp

    @pl.when(i >= 1)
    def _():
        step(i - 1, (d1_ref, d0_ref))

    @pl.when(i == 0)
    def _():
        step(0, (d0_ref,))
    for hh in range(nhs):
        a0, a1 = acc_ref[2 * hh], acc_ref[2 * hh + 1]
        o_t = a0[:dv] / a0[dv:dv + 1] - lam_ref[...] * (a1[:dv] / a1[dv:dv + 1])
        o = o_t.T
        ms = jnp.mean(o * o, axis=-1, keepdims=True)
        o_ref[:, hh * dv:(hh + 1) * dv] = (o * lax.rsqrt(ms + EPS) * sg_ref[...] * out_scale).astype(o_ref.dtype)


def _t5_bucket(n):
    max_exact = REL_BUCKETS // 2
    nf = jnp.maximum(n, 1).astype(F32)
    large = max_exact + (jnp.log(nf / max_exact) / math.log(REL_MAX_DIST / max_exact)
                         * (REL_BUCKETS - max_exact)).astype(jnp.int32)
    large = jnp.minimum(large, REL_BUCKETS - 1)
    return jnp.where(n < max_exact, n, large)


def _toeplitz_kernel(w_ref, d0_ref, d1_ref, *, blk):
    for which, o_ref in enumerate((d0_ref, d1_ref)):
        rows = jnp.broadcast_to(w_ref[0, which:which + 1, :], (blk, 2 * blk))
        o_ref[0] = pltpu.roll(rows, blk + 1, 1, stride=1, stride_axis=0)[:, :blk]


def _toeplitz_tiles(w, blk):
    nh = w.shape[0]
    tile = pl.BlockSpec((1, blk, blk), lambda h: (h, 0, 0))
    return pl.pallas_call(
        functools.partial(_toeplitz_kernel, blk=blk),
        grid=(nh,),
        in_specs=[pl.BlockSpec((1, 2, 2 * blk), lambda h: (h, 0, 0))],
        out_specs=[tile, tile],
        out_shape=[jax.ShapeDtypeStruct((nh, blk, blk), F32)] * 2,
        compiler_params=_cparams("parallel"),
        name="bias_tiles",
    )(w)


def _diff_attention(q_t, kn, v_t, rel_bias, lam, sub_g, lambda_init):
    t = kn.shape[0]
    nh = DIFF_HEADS
    dv = 2 * DIFF_HEAD_DIM
    ones_rows = 16
    blk = min(ATTN_BLOCK, t)
    nhs = ATTN_HEADS_PER_STEP
    assert blk >= REL_MAX_DIST
    far_bias = rel_bias[REL_BUCKETS - 1]

    r = jnp.arange(2 * blk)
    shifted = lambda rel: ((rel_bias[_t5_bucket(jnp.maximum(rel, 0))] - far_bias) * LOG2E).T
    rel_d = r - (blk - 1)
    w0 = jnp.where((rel_d >= 0)[None, :], shifted(rel_d), -jnp.inf)
    w1 = shifted(r + 1)
    d0, d1 = _toeplitz_tiles(jnp.stack([w0, w1], 1).astype(F32), blk)
    once = pl.Buffered(1)
    return pl.pallas_call(
        functools.partial(_attn_kernel, blk=blk, out_scale=1.0 - lambda_init),
        grid=(nh // nhs, t // blk),
        in_specs=[
            pl.BlockSpec((nhs * dv, blk), lambda h, i: (h, i)),
            pl.BlockSpec((t, nhs * dv), lambda h, i: (0, h), pipeline_mode=once),
            pl.BlockSpec((nhs * dv, t), lambda h, i: (h, 0), pipeline_mode=once),
            pl.BlockSpec((nhs, blk, blk), lambda h, i: (h, 0, 0), pipeline_mode=once),
            pl.BlockSpec((nhs, blk, blk), lambda h, i: (h, 0, 0), pipeline_mode=once),
            pl.BlockSpec((1, 1), lambda h, i: (0, 0)),
            pl.BlockSpec((1, dv), lambda h, i: (0, 0)),
        ],
        out_specs=pl.BlockSpec((blk, nhs * dv), lambda h, i: (i, h)),
        out_shape=jax.ShapeDtypeStruct((t, nh * dv), BF16),
        scratch_shapes=[pltpu.VMEM((2 * nhs, 1, blk), F32), pltpu.VMEM((2 * nhs, dv + ones_rows, blk), F32),
                        pltpu.VMEM((nhs, t // blk, dv + ones_rows, blk), BF16)],
        compiler_params=_cparams("parallel", "arbitrary"),
        name="diff_attn",
    )(q_t, kn, v_t, d0, d1, jnp.reshape(lam, (1, 1)).astype(F32), sub_g.reshape(1, dv).astype(F32))


def _fold_keys_kernel(sk_ref, w_ref, o_ref):
    w = w_ref[0]
    for b in range(sk_ref.shape[0]):
        cols = slice(b * LANES, (b + 1) * LANES)
        o_ref[cols, :] = _dot_nt(sk_ref[b], w[:, cols]).astype(o_ref.dtype)


def _fold_keys(w_qs, layer, sub_keys, bd=1024, nb=4):
    _, d, n = w_qs.shape
    return pl.pallas_call(
        _fold_keys_kernel,
        grid=(d // bd, n // (nb * LANES)),
        in_specs=[
            pl.BlockSpec((nb, LANES, LANES), lambda i, j: (j, 0, 0)),
            pl.BlockSpec((1, bd, nb * LANES), lambda i, j: (layer, i, j)),
        ],
        out_specs=pl.BlockSpec((nb * LANES, bd), lambda i, j: (j, i)),
        out_shape=jax.ShapeDtypeStruct((n, d), BF16),
        compiler_params=_cparams("parallel", "parallel"),
        name="peer_fold_keys",
    )(sub_keys, w_qs)


def _top_values(s, n):
    vals = []
    for r in range(n):
        m = jnp.max(s, axis=0, keepdims=True)
        vals.append(m)
        if r < n - 1:
            s = jnp.where(s == m, -jnp.inf, s)
    return vals


def _route_kernel(s_ref, e1_ref, e2_ref, c_ref):
    nk = PEER_N_KEYS
    k = PEER_TOPK
    tl = s_ref.shape[1]
    pad = 24
    row = lax.broadcasted_iota(jnp.int32, (pad, tl), 0)
    hrow = lax.broadcasted_iota(jnp.int32, (PEER_HEADS, tl), 0)
    c_all = jnp.zeros((PEER_HEADS, tl), F32)
    for h in range(PEER_HEADS):
        s1 = s_ref[2 * h * nk:(2 * h + 1) * nk, :]
        s2 = s_ref[(2 * h + 1) * nk:(2 * h + 2) * nk, :]
        a = _top_values(s1, k + 1)
        b = _top_values(s2, k + 1)
        ea = [jnp.exp(x - a[0]) for x in a]
        eb = [jnp.exp(x - b[0]) for x in b]
        ea_m = jnp.full((pad, tl), -1.0, F32)
        eb_m = jnp.full((pad, tl), -1.0, F32)
        for r in range(k + 1):
            ea_m = jnp.where(row == r, ea[r], ea_m)
            eb_m = jnp.where(row == r, eb[r], eb_m)
        cand = jnp.concatenate(
            [ea[0] * eb_m, ea[1] * eb_m[:8], ea[2] * eb_m[:8], ea[3] * eb_m[:8],
             eb[0] * ea_m, eb[1] * ea_m[:8], eb[2] * ea_m[:8]], axis=0)
        v = _top_values(cand, k + 1)
        z = v[0]
        for r in range(1, k):
            z = z + v[r]
        rz = 1.0 / z
        e1_ref[h] = jnp.exp(s1 - a[0])
        e2_ref[h] = (jnp.exp(s2 - b[0]) * rz).astype(e2_ref.dtype)
        c_all = jnp.where(hrow == h, 0.5 * (v[k - 1] + v[k]) * rz, c_all)
    c_ref[...] = c_all


def _peer_route(s_t, tl=128):
    n, t = s_t.shape
    tl = min(tl, t)
    nh, nk = PEER_HEADS, PEER_N_KEYS
    return pl.pallas_call(
        _route_kernel,
        grid=(t // tl,),
        in_specs=[pl.BlockSpec((n, tl), lambda i: (0, i))],
        out_specs=[
            pl.BlockSpec((nh, nk, tl), lambda i: (0, 0, i)),
            pl.BlockSpec((nh, nk, tl), lambda i: (0, 0, i)),
            pl.BlockSpec((nh, tl), lambda i: (0, i)),
        ],
        out_shape=[
            jax.ShapeDtypeStruct((nh, nk, t), F32),
            jax.ShapeDtypeStruct((nh, nk, t), BF16),
            jax.ShapeDtypeStruct((nh, t), F32),
        ],
        compiler_params=_cparams("parallel"),
        name="peer_route",
    )(s_t)


def _peer_dense_kernel(u_ref, vt_ref, h_ref, e1_ref, e2_ref, c_ref, o_ref, w_ref):
    e = pl.program_id(1)

    @pl.when(e == 0)
    def _():
        o_ref[...] = jnp.zeros_like(o_ref)

    a_t = _dot_nt(u_ref[...], h_ref[...])
    tm = a_t.shape[1]
    nslab = a_t.shape[0] // LANES
    gdt = e2_ref.dtype
    for ii in range(nslab):
        rows = slice(ii * LANES, (ii + 1) * LANES)
        g = jnp.zeros((LANES, tm), gdt)
        for h in range(PEER_HEADS):
            vv = e2_ref[h] * e1_ref[h, pl.ds(e * nslab + ii, 1), :].astype(gdt)
            g = g + jnp.where(vv >= c_ref[h:h + 1, :].astype(gdt), vv, jnp.zeros_like(vv))
        w_ref[rows, :] = (g * _gelu(a_t[rows, :].astype(gdt))).astype(w_ref.dtype)
    o_ref[...] += _dot(vt_ref[...], w_ref[...])


def _peer_dense(hq, u_tab, v_tab_t, e1_t, e2, c, tm=512, te=512):
    t, d = hq.shape
    ne = u_tab.shape[0]
    tm = min(tm, t)
    nh, nk = PEER_HEADS, PEER_N_KEYS
    once = pl.Buffered(1)
    return pl.pallas_call(
        _peer_dense_kernel,
        grid=(t // tm, ne // te),
        in_specs=[
            pl.BlockSpec((te, d), lambda i, e: (e, 0)),
            pl.BlockSpec((d, te), lambda i, e: (0, e)),
            pl.BlockSpec((tm, d), lambda i, e: (i, 0), pipeline_mode=once),
            pl.BlockSpec((nh, nk, tm), lambda i, e: (0, 0, i), pipeline_mode=once),
            pl.BlockSpec((nh, nk, tm), lambda i, e: (0, 0, i), pipeline_mode=once),
            pl.BlockSpec((nh, tm), lambda i, e: (0, i)),
        ],
        out_specs=pl.BlockSpec((d, tm), lambda i, e: (0, i)),
        out_shape=jax.ShapeDtypeStruct((d, t), F32),
        scratch_shapes=[pltpu.VMEM((te, tm), BF16)],
        compiler_params=_cparams("parallel", "arbitrary"),
        name="peer_dense",
    )(u_tab, v_tab_t, hq, e1_t, e2, c)


def _peer(x, norm_g, w_qs, sub_keys, u_tabs, v_tabs, layer):
    hq = _rmsnorm(x, norm_g)
    nh, nk = PEER_HEADS, PEER_N_KEYS
    s_t = _matmul_nt(_fold_keys(w_qs, layer, sub_keys.reshape(2 * nh, nk, -1)), hq, out_dtype=F32, name="peer_scores")
    e1, e2, c = _peer_route(s_t)
    return _peer_dense(hq, _cast_bf16(u_tabs, layer), _cast_bf16(v_tabs, layer, transpose=True), e1, e2, c)


def _even_layer(x, h, i, w_ins, log_step, a_re, a_im, b_re, b_im, c_re, c_im, d_skip,
                glu_ws, glu_b, gn_g, gn_b, w_outs):
    s5w = d_skip.shape[0]
    u_t = _matmul([h], [_cast_bf16(w_ins, i, 0, s5w)], tile_major=True, name="ev_in_s5")
    z4 = _matmul([h], [_cast_bf16(w_ins, i, s5w)], name="ev_in_ret")
    tables = _s5_tables(log_step, a_re, a_im, b_re, b_im, c_re, c_im, d_skip)
    ya = _glu(_s5_core(u_t, tables), _cast_bf16(glu_ws, i), glu_b)
    yb = _retention(z4, gn_g, gn_b)
    w_out = _cast_bf16(w_outs, i)
    return _matmul([ya, yb], [w_out[:s5w], w_out[s5w:]], resid=x, out_dtype=F32, name="ev_out")


def _odd_layer(x, h, i, w_ins, q_norm_g, k_norm_g, lq1, lk1, lq2, lk2, sub_norm_g, w_outs,
               rel_bias, lambda_init):
    d = x.shape[1]
    z = _matmul([h], [_cast_bf16(w_ins, i, 0, 2 * d)], name="od_in_qk")
    v_t = _matmul_nt(_cast_bf16(w_ins, i, 2 * d, transpose=True), h, name="od_in_vt")
    reps = d // DIFF_HEAD_DIM
    q_t = _qknorm(z, jnp.tile(q_norm_g, reps) * (DIFF_HEAD_DIM ** -0.5 * LOG2E), 0, True)
    kn = _qknorm(z, jnp.tile(k_norm_g, reps), d, False)
    lam = jnp.exp(jnp.sum(lq1 * lk1)) - jnp.exp(jnp.sum(lq2 * lk2)) + lambda_init
    o = _diff_attention(q_t, kn, v_t, rel_bias, lam, sub_norm_g, lambda_init)
    return _matmul([o], [_cast_bf16(w_outs, i)], resid=x, out_dtype=F32, name="od_out")


def kernel(x, ev_norm_g, ev_w_in, s5_log_step, s5_a_re, s5_a_im, s5_b_re, s5_b_im, s5_c_re, s5_c_im,
           s5_d, s5_glu_w, s5_glu_b, ret_gn_g, ret_gn_b, ev_w_out, od_norm_g, od_w_in, diff_q_norm_g,
           diff_k_norm_g, diff_lq1, diff_lk1, diff_lq2, diff_lk2, diff_sub_norm_g, od_w_out, rel_bias,
           ffn_norm_g, peer_w_q, peer_sub_keys, peer_u, peer_v):
    bsz, t, d = x.shape
    depth = ffn_norm_g.shape[0]
    mixer_norm_g = lambda layer: (ev_norm_g if layer % 2 == 0 else od_norm_g)[layer // 2]
    outs = []
    for b in range(bsz):
        xb = x[b]
        h = _rmsnorm(xb, mixer_norm_g(0))
        for layer in range(depth):
            i = layer // 2
            if layer % 2 == 0:
                xb = _even_layer(xb, h, i, ev_w_in, s5_log_step[i], s5_a_re[i], s5_a_im[i],
                                 s5_b_re[i], s5_b_im[i], s5_c_re[i], s5_c_im[i], s5_d[i], s5_glu_w,
                                 s5_glu_b[i], ret_gn_g[i], ret_gn_b[i], ev_w_out)
            else:
                lambda_init = 0.8 - 0.6 * math.exp(-0.3 * layer)
                xb = _odd_layer(xb, h, i, od_w_in, diff_q_norm_g[i], diff_k_norm_g[i],
                                diff_lq1[i], diff_lk1[i], diff_lq2[i], diff_lk2[i], diff_sub_norm_g[i],
                                od_w_out, rel_bias, lambda_init)
            p_t = _peer(xb, ffn_norm_g[layer], peer_w_q, peer_sub_keys[layer], peer_u, peer_v, layer)
            xb, h = _add_norm(xb, p_t, mixer_norm_g(layer + 1) if layer + 1 < depth else None)
        outs.append(xb)
    return jnp.stack(outs, 0)
```
